```python
import jax
import jax.numpy as jnp
from jax import lax
import numpy as np

D_MODEL = 1024
BATCH = 8
SEQ = 4096
DEPTH = 2
DEC_BATCH = 16
DEC_SEQ = 4096
PAST_LEN = 128

NORM_EPS = 1e-6
ROPE_THETA = 10000.0
LRU_WIDTH = 256
LRU_BLOCKS = 4
CONV_WIDTH = 4
LRU_C = 8.0
DIL_HEADS = 4
DIL_HEAD_DIM = 64
DIL_WIDTH = DIL_HEADS * DIL_HEAD_DIM
DIL_PATTERNS = ((128, 1), (512, 4), (2048, 16))
MLA_HEADS = 8
MLA_NOPE = 64
MLA_ROPE = 32
MLA_V = 64
MLA_Q_RANK = 256
MLA_KV_RANK = 128
MLA_WIDTH = MLA_HEADS * MLA_V
Q_BLOCK = 128
D_MIX = LRU_WIDTH + DIL_WIDTH + MLA_WIDTH
IN_SPLITS = (LRU_WIDTH, LRU_WIDTH, DIL_WIDTH, DIL_WIDTH, DIL_WIDTH, MLA_Q_RANK, MLA_KV_RANK, MLA_ROPE)
IN_COLS = 2 * LRU_WIDTH + 3 * DIL_WIDTH + MLA_Q_RANK + MLA_KV_RANK + MLA_ROPE
D_FF = 2816
N_EXPERTS = 8
TOP_K = 2
EXPERT_FF = 3584
N_DENSE = (DEPTH + 1) // 2
N_MOE = DEPTH // 2

kernel_name = 'hybrid_bidir_lru_dilated_mla_encoder'


def rmsnorm(x, g):
    xf = x.astype(jnp.float32)
    y = xf * lax.rsqrt(jnp.mean(xf * xf, axis=-1, keepdims=True) + NORM_EPS)
    return (y * g.astype(jnp.float32)).astype(x.dtype)


def rope(x):
    S, dim = x.shape[1], x.shape[-1]
    half = dim // 2
    freqs = jnp.power(jnp.float32(ROPE_THETA), -jnp.arange(half, dtype=jnp.float32) * 2.0 / dim)
    ang = jnp.arange(S, dtype=jnp.float32)[:, None] * freqs[None, :]
    cos = jnp.cos(ang)[None, :, None, :]
    sin = jnp.sin(ang)[None, :, None, :]
    xf = x.astype(jnp.float32)
    x1, x2 = xf[..., :half], xf[..., half:]
    out = jnp.concatenate([x1 * cos - x2 * sin, x2 * cos + x1 * sin], axis=-1)
    return out.astype(x.dtype)


def depthwise_conv(x, w, b):
    S = x.shape[1]
    left = CONV_WIDTH // 2
    right = CONV_WIDTH - 1 - left
    xp = jnp.pad(x, ((0, 0), (left, right), (0, 0)))
    y = xp[:, 0:S] * w[0]
    for k in range(1, CONV_WIDTH):
        y = y + xp[:, k:k + S] * w[k]
    return y + b


def block_diag(x, w):
    B, S, C = x.shape
    nb = w.shape[0]
    return jnp.einsum('bsnc,ncd->bsnd', x.reshape(B, S, nb, C // nb), w).reshape(B, S, C)


def _linear_combine(e1, e2):
    a1, b1 = e1
    a2, b2 = e2
    return (a1 * a2, a2 * b1 + b2)


def rglru(x, w_a, b_a, w_i, b_i, lam, reverse):
    xf = x.astype(jnp.float32)
    r = jax.nn.sigmoid(block_diag(xf, w_a.astype(jnp.float32)) + b_a.astype(jnp.float32))
    i = jax.nn.sigmoid(block_diag(xf, w_i.astype(jnp.float32)) + b_i.astype(jnp.float32))
    log_a = -LRU_C * r * jax.nn.softplus(-lam.astype(jnp.float32))
    a = jnp.exp(log_a)
    b = jnp.sqrt(-jnp.expm1(2.0 * log_a)) * (i * xf)
    _, h = lax.associative_scan(_linear_combine, (a, b), reverse=reverse, axis=1)
    return h.astype(x.dtype)


def banded_attention(q, k, v, half):
    N, L, H, dh = q.shape
    blk = half
    nb = -(-L // blk)
    Lp = nb * blk
    pad = Lp - L
    qb = jnp.pad(q.astype(jnp.float32), ((0, 0), (0, pad), (0, 0), (0, 0))).reshape(N, nb, blk, H, dh)
    kp = jnp.pad(k.astype(jnp.float32), ((0, 0), (blk, blk + pad), (0, 0), (0, 0)))
    vp = jnp.pad(v.astype(jnp.float32), ((0, 0), (blk, blk + pad), (0, 0), (0, 0)))

    def neighbourhood(t):
        return jnp.concatenate([t[:, i * blk:i * blk + Lp].reshape(N, nb, blk, H, dh) for i in range(3)], axis=2)

    kb = neighbourhood(kp)
    vb = neighbourhood(vp)
    s = jnp.einsum('nbqhd,nbkhd->nbhqk', qb, kb) * (dh ** -0.5)
    qpos = jnp.arange(nb)[:, None] * blk + jnp.arange(blk)[None, :]
    kpos = jnp.arange(nb)[:, None] * blk + jnp.arange(3 * blk)[None, :] - blk
    valid = ((jnp.abs(qpos[:, :, None] - kpos[:, None, :]) <= half)
             & (kpos[:, None, :] >= 0) & (kpos[:, None, :] < L))
    s = jnp.where(valid[None, :, None], s, -1e30)
    m = jnp.max(s, axis=-1, keepdims=True)
    e = jnp.exp(s - m)
    den = jnp.sum(e, axis=-1, keepdims=True)
    o = jnp.einsum('nbhqk,nbkhd->nbqhd', e, vb) / den.transpose(0, 1, 3, 2, 4)
    lse = (m + jnp.log(den))[..., 0].transpose(0, 1, 3, 2)
    return o.reshape(N, Lp, H, dh)[:, :L], lse.reshape(N, Lp, H)[:, :L]


def dilated_attention(q, k, v):
    B, S, H, dh = q.shape
    outs, lses = [], []
    for window, dil in DIL_PATTERNS:
        half = window // (2 * dil)
        Ls = S // dil

        def to_sub(t):
            return t.reshape(B, Ls, dil, H, dh).transpose(0, 2, 1, 3, 4).reshape(B * dil, Ls, H, dh)

        o, lse = banded_attention(to_sub(q), to_sub(k), to_sub(v), half)
        outs.append(o.reshape(B, dil, Ls, H, dh).transpose(0, 2, 1, 3, 4).reshape(B, S, H, dh))
        lses.append(lse.reshape(B, dil, Ls, H).transpose(0, 2, 1, 3).reshape(B, S, H))
    wts = jax.nn.softmax(jnp.stack(lses, axis=0), axis=0)
    out = jnp.sum(wts[..., None] * jnp.stack(outs, axis=0), axis=0)
    return out.astype(q.dtype)


def mla_attention(q_nope, q_rope, k_nope, k_rope, v):
    B, S, H, _ = q_nope.shape
    nq = S // Q_BLOCK
    scale = (MLA_NOPE + MLA_ROPE) ** -0.5
    kn = k_nope.astype(jnp.float32)
    kr = k_rope.astype(jnp.float32)
    vf = v.astype(jnp.float32)

    def blocks(t):
        return t.reshape(B, nq, Q_BLOCK, H, t.shape[-1]).transpose(1, 0, 2, 3, 4)

    def one_block(args):
        qn, qr = args
        s = (jnp.einsum('bqhd,bkhd->bhqk', qn.astype(jnp.float32), kn)
             + jnp.einsum('bqhd,bkd->bhqk', qr.astype(jnp.float32), kr)) * scale
        p = jax.nn.softmax(s, axis=-1)
        return jnp.einsum('bhqk,bkhd->bqhd', p, vf)

    o = lax.map(one_block, (blocks(q_nope), blocks(q_rope)))
    return o.transpose(1, 0, 2, 3, 4).reshape(B, S, H * MLA_V).astype(q_nope.dtype)


def token_mixers(h, w_in, conv_w, conv_b, rg_w_a, rg_b_a, rg_w_i, rg_b_i, rg_lambda,
                 q_norm_g, w_uq, kv_norm_g, w_ukv, out_norm_g):
    B, S, _ = h.shape
    p = h @ w_in
    offs = [int(o) for o in np.cumsum(IN_SPLITS)[:-1]]
    xr, yr, qd, kd, vd, cq, ckv, kr = jnp.split(p, offs, axis=-1)

    xc = depthwise_conv(xr, conv_w, conv_b)
    h_fwd = rglru(xc, rg_w_a[0], rg_b_a[0], rg_w_i[0], rg_b_i[0], rg_lambda[0], reverse=False)
    h_bwd = rglru(xc, rg_w_a[1], rg_b_a[1], rg_w_i[1], rg_b_i[1], rg_lambda[1], reverse=True)
    lru_out = (h_fwd + h_bwd) * jax.nn.gelu(yr)

    q = rope(qd.reshape(B, S, DIL_HEADS, DIL_HEAD_DIM))
    k = rope(kd.reshape(B, S, DIL_HEADS, DIL_HEAD_DIM))
    v = vd.reshape(B, S, DIL_HEADS, DIL_HEAD_DIM)
    dil_out = dilated_attention(q, k, v).reshape(B, S, DIL_WIDTH)

    qm = (rmsnorm(cq, q_norm_g) @ w_uq).reshape(B, S, MLA_HEADS, MLA_NOPE + MLA_ROPE)
    q_nope, q_rope = qm[..., :MLA_NOPE], rope(qm[..., MLA_NOPE:])
    kvm = (rmsnorm(ckv, kv_norm_g) @ w_ukv).reshape(B, S, MLA_HEADS, MLA_NOPE + MLA_V)
    k_nope, v_m = kvm[..., :MLA_NOPE], kvm[..., MLA_NOPE:]
    k_rope = rope(kr[:, :, None, :])[:, :, 0, :]
    mla_out = mla_attention(q_nope, q_rope, k_nope, k_rope, v_m)

    g_l, g_d, g_m = jnp.split(out_norm_g, [LRU_WIDTH, LRU_WIDTH + DIL_WIDTH])
    return jnp.concatenate([rmsnorm(lru_out, g_l), rmsnorm(dil_out, g_d), rmsnorm(mla_out, g_m)], axis=-1)


def swiglu(h, w_gate, w_up, w_down):
    return (jax.nn.silu(h @ w_gate) * (h @ w_up)) @ w_down


def moe_swiglu(h, router_w, w_gate, w_up, w_down):
    B, S, D = h.shape
    t = h.reshape(B * S, D)
    logits = (t @ router_w).astype(jnp.float32)
    top_v, top_i = lax.top_k(logits, TOP_K)
    top_w = jax.nn.softmax(top_v, axis=-1)
    gates = jnp.sum(jax.nn.one_hot(top_i, N_EXPERTS, dtype=jnp.float32) * top_w[..., None], axis=1)
    out = jnp.zeros_like(t)
    for e in range(N_EXPERTS):
        out = out + gates[:, e:e + 1].astype(t.dtype) * swiglu(t, w_gate[e], w_up[e], w_down[e])
    return out.reshape(B, S, D)


def setup_inputs(seed: int = 0) -> dict:
    key = jax.random.key(seed)
    ks = jax.random.split(key, 32)
    f32 = jnp.float32

    def nrm(i, shape, scale):
        return jax.random.normal(ks[i], shape, f32) * scale

    def gain(i, shape):
        return 1.0 + 0.02 * jax.random.normal(ks[i], shape, f32)

    bs = LRU_WIDTH // LRU_BLOCKS
    a0 = jax.random.uniform(ks[10], (DEPTH, 2, LRU_WIDTH), f32, 0.9, 0.999)
    base = a0 ** (1.0 / LRU_C)
    rg_lambda = jnp.log(base) - jnp.log1p(-base)
    return {
        'x_prompt': jax.random.normal(ks[0], (BATCH, SEQ, D_MODEL), f32),
        'x_sample': jax.random.normal(ks[1], (DEC_BATCH, DEC_SEQ, D_MODEL), f32),
        'norm_mix_g': gain(2, (DEPTH, D_MODEL)),
        'w_in': nrm(3, (DEPTH, D_MODEL, IN_COLS), D_MODEL ** -0.5),
        'conv_w': nrm(4, (DEPTH, CONV_WIDTH, LRU_WIDTH), CONV_WIDTH ** -0.5),
        'conv_b': nrm(5, (DEPTH, LRU_WIDTH), 0.02),
        'rg_w_a': nrm(6, (DEPTH, 2, LRU_BLOCKS, bs, bs), bs ** -0.5),
        'rg_b_a': nrm(7, (DEPTH, 2, LRU_WIDTH), 0.1),
        'rg_w_i': nrm(8, (DEPTH, 2, LRU_BLOCKS, bs, bs), bs ** -0.5),
        'rg_b_i': nrm(9, (DEPTH, 2, LRU_WIDTH), 0.1),
        'rg_lambda': rg_lambda,
        'mla_q_norm_g': gain(11, (DEPTH, MLA_Q_RANK)),
        'mla_w_uq': nrm(12, (DEPTH, MLA_Q_RANK, MLA_HEADS * (MLA_NOPE + MLA_ROPE)), MLA_Q_RANK ** -0.5),
        'mla_kv_norm_g': gain(13, (DEPTH, MLA_KV_RANK)),
        'mla_w_ukv': nrm(14, (DEPTH, MLA_KV_RANK, MLA_HEADS * (MLA_NOPE + MLA_V)), MLA_KV_RANK ** -0.5),
        'mix_out_norm_g': gain(15, (DEPTH, D_MIX)),
        'w_out': nrm(16, (DEPTH, D_MIX, D_MODEL), D_MIX ** -0.5),
        'norm_ffn_g': gain(17, (DEPTH, D_MODEL)),
        'ffn_w_gate': nrm(18, (N_DENSE, D_MODEL, D_FF), D_MODEL ** -0.5),
        'ffn_w_up': nrm(19, (N_DENSE, D_MODEL, D_FF), D_MODEL ** -0.5),
        'ffn_w_down': nrm(20, (N_DENSE, D_FF, D_MODEL), D_FF ** -0.5),
        'router_w': nrm(21, (N_MOE, D_MODEL, N_EXPERTS), D_MODEL ** -0.5),
        'moe_w_gate': nrm(22, (N_MOE, N_EXPERTS, D_MODEL, EXPERT_FF), D_MODEL ** -0.5),
        'moe_w_up': nrm(23, (N_MOE, N_EXPERTS, D_MODEL, EXPERT_FF), D_MODEL ** -0.5),
        'moe_w_down': nrm(24, (N_MOE, N_EXPERTS, EXPERT_FF, D_MODEL), EXPERT_FF ** -0.5),
        'final_norm_g': gain(25, (D_MODEL,)),
    }


def reference(x_prompt, x_sample, norm_mix_g, w_in, conv_w, conv_b, rg_w_a, rg_b_a, rg_w_i, rg_b_i,
              rg_lambda, mla_q_norm_g, mla_w_uq, mla_kv_norm_g, mla_w_ukv, mix_out_norm_g, w_out,
              norm_ffn_g, ffn_w_gate, ffn_w_up, ffn_w_down, router_w, moe_w_gate, moe_w_up, moe_w_down,
              final_norm_g):
    def trunk(x):
        for l in range(DEPTH):
            h = rmsnorm(x, norm_mix_g[l])
            mix = token_mixers(h, w_in[l], conv_w[l], conv_b[l], rg_w_a[l], rg_b_a[l], rg_w_i[l], rg_b_i[l],
                               rg_lambda[l], mla_q_norm_g[l], mla_w_uq[l], mla_kv_norm_g[l], mla_w_ukv[l],
                               mix_out_norm_g[l])
            x = x + mix @ w_out[l]
            h = rmsnorm(x, norm_ffn_g[l])
            j = l // 2
            if l % 2 == 0:
                x = x + swiglu(h, ffn_w_gate[j], ffn_w_up[j], ffn_w_down[j])
            else:
                x = x + moe_swiglu(h, router_w[j], moe_w_gate[j], moe_w_up[j], moe_w_down[j])
        return rmsnorm(x, final_norm_g)

    y_prompt = trunk(x_prompt)
    y_sample = trunk(x_sample)
    return (y_prompt, y_sample)
```

```python
import functools

import jax
import jax.numpy as jnp
from jax import lax
from jax.experimental import pallas as pl
from jax.experimental.pallas import tpu as pltpu

F32 = jnp.float32
BF16 = jnp.bfloat16

D_MODEL = 1024
NORM_EPS = 1e-6
ROPE_THETA = 10000.0
LRU_WIDTH = 256
LRU_BLOCKS = 4
CONV_WIDTH = 4
LRU_C = 8.0
DIL_HEADS = 4
DIL_HEAD_DIM = 64
DIL_WIDTH = DIL_HEADS * DIL_HEAD_DIM
DIL_PATTERNS = ((128, 1), (512, 4), (2048, 16))
MLA_HEADS = 8
MLA_NOPE = 64
MLA_ROPE = 32
MLA_V = 64
MLA_Q_RANK = 256
MLA_KV_RANK = 128
MLA_WIDTH = MLA_HEADS * MLA_V
N_EXPERTS = 8
TOP_K = 2

LANES = 128
MLA_HEAD_PAD = LANES
VMEM_LIMIT = 56 * 1024 * 1024

_C_XR, _C_YR, _C_Q, _C_QS, _C_K, _C_KS, _C_V, _C_CQ, _C_CKV, _C_KR, _C_KRS, _C_END = (
    0, 256, 512, 768, 1024, 1280, 1536, 1792, 2048, 2176, 2304, 2432)


def _cparams(sem):
    return pltpu.CompilerParams(dimension_semantics=sem, vmem_limit_bytes=VMEM_LIMIT)


def _rms(x, g, n):
    ms = jnp.sum(x * x, axis=-1, keepdims=True) * (1.0 / n)
    return x * lax.rsqrt(ms + NORM_EPS) * g


def _gelu_tanh(x):
    return 0.5 * x * (1.0 + jnp.tanh(0.7978845608028654 * (x + 0.044715 * (x * x * x))))


def _dot(a, b):
    return jnp.dot(a, b, preferred_element_type=F32)


def _dot_nt(a, b):
    return lax.dot_general(a, b, (((1,), (1,)), ((), ())), preferred_element_type=F32)


def _in_proj_kernel(x_ref, g_ref, w1_ref, gq_ref, wuq_ref, gkv_ref, wukv_ref,
                    cosd_ref, sind_ref, cosm_ref, sinm_ref,
                    xr_ref, gy_ref, q_ref, k_ref, v_ref, qm_ref, km_ref, vm_ref):
    h = _rms(x_ref[...], g_ref[...], D_MODEL).astype(BF16)
    p = _dot(h, w1_ref[...])
    xr_ref[...] = p[:, _C_XR:_C_YR]
    gy_ref[...] = _gelu_tanh(p[:, _C_YR:_C_Q]).astype(BF16)
    cosd, sind = cosd_ref[...], sind_ref[...]
    q = p[:, _C_Q:_C_QS] * cosd + p[:, _C_QS:_C_K] * sind
    q_ref[...] = (q * (DIL_HEAD_DIM ** -0.5)).astype(BF16)
    k_ref[...] = (p[:, _C_K:_C_KS] * cosd + p[:, _C_KS:_C_V] * sind).astype(BF16)
    v_ref[...] = p[:, _C_V:_C_CQ].astype(BF16)

    cosm, sinm = cosm_ref[...], sinm_ref[...]
    cqn = _rms(p[:, _C_CQ:_C_CKV], gq_ref[...], MLA_Q_RANK).astype(BF16)
    qq = _dot(cqn, wuq_ref[...])
    ckvn = _rms(p[:, _C_CKV:_C_KR], gkv_ref[...], MLA_KV_RANK).astype(BF16)
    kv = _dot(ckvn, wukv_ref[...])
    kr = p[:, _C_KR:_C_KRS] * cosm + p[:, _C_KRS:_C_END] * sinm
    scale = (MLA_NOPE + MLA_ROPE) ** -0.5
    nq = MLA_HEADS * MLA_HEAD_PAD
    for hh in range(MLA_HEADS):
        lo, hi = hh * MLA_HEAD_PAD, (hh + 1) * MLA_HEAD_PAD
        qh = qq[:, lo:hi] * cosm + qq[:, nq + lo:nq + hi] * sinm
        qm_ref[:, lo:hi] = (qh * scale).astype(BF16)
        km_ref[:, lo:hi] = (kv[:, lo:hi] + kr).astype(BF16)
    vm_ref[...] = kv[:, nq:].astype(BF16)


def _in_proj(x2d, lw, tabs, B, S, tm):
    T = B * S
    ns = S // tm
    tok = lambda b, s: (b * ns + s, 0)
    pos = lambda b, s: (s, 0)
    const = lambda b, s: (0, 0)

    def full(a):
        return pl.BlockSpec(a.shape, const)

    def tokspec(c):
        return pl.BlockSpec((tm, c), tok)

    out_shape = (
        jax.ShapeDtypeStruct((S, B * LRU_WIDTH), F32),
        jax.ShapeDtypeStruct((T, LRU_WIDTH), BF16),
        jax.ShapeDtypeStruct((T, DIL_WIDTH), BF16),
        jax.ShapeDtypeStruct((T, DIL_WIDTH), BF16),
        jax.ShapeDtypeStruct((T, DIL_WIDTH), BF16),
        jax.ShapeDtypeStruct((T, MLA_HEADS * MLA_HEAD_PAD), BF16),
        jax.ShapeDtypeStruct((T, MLA_HEADS * MLA_HEAD_PAD), BF16),
        jax.ShapeDtypeStruct((T, MLA_WIDTH), BF16),
    )
    out_specs = (
        pl.BlockSpec((tm, LRU_WIDTH), lambda b, s: (s, b)),
        tokspec(LRU_WIDTH), tokspec(DIL_WIDTH), tokspec(DIL_WIDTH), tokspec(DIL_WIDTH),
        tokspec(MLA_HEADS * MLA_HEAD_PAD), tokspec(MLA_HEADS * MLA_HEAD_PAD), tokspec(MLA_WIDTH),
    )
    in_specs = [
        tokspec(D_MODEL), full(lw['g_mix']), full(lw['w1']), full(lw['g_q']), full(lw['w_uq']),
        full(lw['g_kv']), full(lw['w_ukv']),
        pl.BlockSpec((tm, DIL_WIDTH), pos), pl.BlockSpec((tm, DIL_WIDTH), pos),
        pl.BlockSpec((tm, LANES), pos), pl.BlockSpec((tm, LANES), pos),
    ]
    return pl.pallas_call(
        _in_proj_kernel, grid=(B, ns), in_specs=in_specs, out_specs=out_specs, out_shape=out_shape,
        compiler_params=_cparams(("parallel", "parallel")), name="in_proj",
    )(x2d, lw['g_mix'], lw['w1'], lw['g_q'], lw['w_uq'], lw['g_kv'], lw['w_ukv'],
      tabs['cosd'], tabs['sind'], tabs['cosm'], tabs['sinm'])


def _lru_kernel(xf_ref, xfp_ref, xfn_ref, xb_ref, xbp_ref, xbn_ref, cw_ref, cb_ref, wg_ref, bg_ref,
                lam_ref, hf_ref, hb_ref, xpad, a_f, b_f, a_b, b_b, hcar, *, B, R):
    i = pl.program_id(0)
    n = pl.num_programs(0)
    tc = R // B

    @pl.when(i == 0)
    def _():
        hcar[...] = jnp.zeros_like(hcar)

    def prep(x_ref, xp_ref, xn_ref, ci, d, a_s, b_s):
        xpad[0:2 * B, :] = jnp.where(ci > 0, xp_ref[...], 0.0)
        xpad[2 * B:2 * B + R, :] = x_ref[...]
        xpad[2 * B + R:3 * B + R, :] = jnp.where(ci < n - 1, xn_ref[0:B, :], 0.0)
        xc = cb_ref[...] + xpad[0:R, :] * cw_ref[0:1, :]
        for kk in range(1, CONV_WIDTH):
            xc = xc + xpad[kk * B:kk * B + R, :] * cw_ref[kk:kk + 1, :]
        g = _dot(xc.astype(BF16), wg_ref[d]) + bg_ref[d]
        r = jax.nn.sigmoid(g[:, :LRU_WIDTH])
        ig = jax.nn.sigmoid(g[:, LRU_WIDTH:])
        lam = lam_ref[d]
        softplus = jnp.maximum(-lam, 0.0) + jnp.log(1.0 + jnp.exp(-jnp.abs(lam)))
        log_a = (-LRU_C) * r * softplus
        a_s[...] = jnp.exp(log_a)
        b_s[...] = jnp.sqrt(1.0 - jnp.exp(2.0 * log_a)) * (ig * xc)

    prep(xf_ref, xfp_ref, xfn_ref, i, 0, a_f, b_f)
    prep(xb_ref, xbp_ref, xbn_ref, n - 1 - i, 1, a_b, b_b)

    def body(s, carry):
        hf, hb = carry
        rf = pl.multiple_of(s * B, B)
        hf = a_f[pl.ds(rf, B), :] * hf + b_f[pl.ds(rf, B), :]
        hf_ref[pl.ds(rf, B), :] = hf
        rb = pl.multiple_of((tc - 1 - s) * B, B)
        hb = a_b[pl.ds(rb, B), :] * hb + b_b[pl.ds(rb, B), :]
        hb_ref[pl.ds(rb, B), :] = hb
        return hf, hb

    hf, hb = lax.fori_loop(0, tc, body, (hcar[0], hcar[1]), unroll=8)
    hcar[0] = hf
    hcar[1] = hb


def _lru(xr_t, lw, B, S):
    rows = S * B
    R = 1024
    HB = 2 * B
    n = rows // R
    nh = rows // HB
    x2 = xr_t.reshape(rows, LRU_WIDTH)
    rb = R // HB

    main_f = pl.BlockSpec((R, LRU_WIDTH), lambda i: (i, 0))
    prev_f = pl.BlockSpec((HB, LRU_WIDTH), lambda i: (jnp.maximum(i * rb - 1, 0), 0))
    next_f = pl.BlockSpec((HB, LRU_WIDTH), lambda i: (jnp.minimum((i + 1) * rb, nh - 1), 0))
    main_b = pl.BlockSpec((R, LRU_WIDTH), lambda i: (n - 1 - i, 0))
    prev_b = pl.BlockSpec((HB, LRU_WIDTH), lambda i: (jnp.maximum((n - 1 - i) * rb - 1, 0), 0))
    next_b = pl.BlockSpec((HB, LRU_WIDTH), lambda i: (jnp.minimum((n - i) * rb, nh - 1), 0))

    def full(a):
        nd = a.ndim
        return pl.BlockSpec(a.shape, lambda i: (0,) * nd)

    hf, hb = pl.pallas_call(
        functools.partial(_lru_kernel, B=B, R=R),
        grid=(n,),
        in_specs=[main_f, prev_f, next_f, main_b, prev_b, next_b,
                  full(lw['conv_w']), full(lw['conv_b']), full(lw['rg_w']), full(lw['rg_b']), full(lw['rg_lam'])],
        out_specs=(main_f, main_b),
        out_shape=(jax.ShapeDtypeStruct((rows, LRU_WIDTH), F32),) * 2,
        scratch_shapes=[pltpu.VMEM((R + 3 * B, LRU_WIDTH), F32)] + [pltpu.VMEM((R, LRU_WIDTH), F32)] * 4
        + [pltpu.VMEM((2, B, LRU_WIDTH), F32)],
        compiler_params=_cparams(("arbitrary",)), name="rglru",
    )(x2, x2, x2, x2, x2, x2, lw['conv_w'], lw['conv_b'], lw['rg_w'], lw['rg_b'], lw['rg_lam'])
    return hf.reshape(S, B * LRU_WIDTH), hb.reshape(S, B * LRU_WIDTH)


BAND_HALF = 64


def _band_kernel(q_ref, k_ref, kp_ref, kn_ref, v_ref, vp_ref, vn_ref, o_ref, lse_ref, kcat, vcat, *, tq):
    i = pl.program_id(2)
    nblk = pl.num_programs(2)
    hb = BAND_HALF
    kcat[0:hb, :] = kp_ref[0]
    kcat[hb:hb + tq, :] = k_ref[0]
    kcat[hb + tq:, :] = kn_ref[0]
    vcat[0:hb, :] = vp_ref[0]
    vcat[hb:hb + tq, :] = v_ref[0]
    vcat[hb + tq:, :] = vn_ref[0]
    q = q_ref[0]
    nk = tq + 2 * hb
    row = lax.broadcasted_iota(jnp.int32, (tq, nk), 0)
    col = lax.broadcasted_iota(jnp.int32, (tq, nk), 1)
    dist = col - hb - row
    valid = ((jnp.abs(dist) <= hb) & ((col >= hb) | (i > 0)) & ((col < tq + hb) | (i < nblk - 1)))
    head = lax.broadcasted_iota(jnp.int32, (1, DIL_WIDTH), 1) // DIL_HEAD_DIM
    acc_o = jnp.zeros((tq, DIL_WIDTH), F32)
    acc_l = jnp.zeros((tq, DIL_WIDTH), F32)
    kc = kcat[...]
    vc = vcat[...]
    for h in range(DIL_HEADS):
        hm = head == h
        qh = jnp.where(hm, q, jnp.zeros_like(q))
        s = jnp.where(valid, _dot_nt(qh, kc), -1e30)
        m = jnp.max(s, axis=-1, keepdims=True)
        e = jnp.exp(s - m)
        l = jnp.sum(e, axis=-1, keepdims=True)
        o = _dot(e.astype(BF16), vc) * (1.0 / l)
        acc_o = jnp.where(hm, o, acc_o)
        acc_l = jnp.where(hm, m + jnp.log(l), acc_l)
    o_ref[0] = acc_o.astype(BF16)
    lse_ref[0] = acc_l


def _banded(q, k, v, B, S, dil):
    Ls = S // dil
    tq = min(256, Ls)
    nb = Ls // tq
    nhb = Ls // BAND_HALF
    r64 = tq // BAND_HALF
    view = lambda a: a.reshape(B, Ls, dil * DIL_WIDTH)
    W = DIL_WIDTH
    own = pl.BlockSpec((1, tq, W), lambda b, r, i: (b, i, r))
    prev = pl.BlockSpec((1, BAND_HALF, W), lambda b, r, i: (b, jnp.maximum(i * r64 - 1, 0), r))
    nxt = pl.BlockSpec((1, BAND_HALF, W), lambda b, r, i: (b, jnp.minimum((i + 1) * r64, nhb - 1), r))
    qv, kv, vv = view(q), view(k), view(v)
    o, lse = pl.pallas_call(
        functools.partial(_band_kernel, tq=tq),
        grid=(B, dil, nb),
        in_specs=[own, own, prev, nxt, own, prev, nxt],
        out_specs=(own, own),
        out_shape=(jax.ShapeDtypeStruct((B, Ls, dil * W), BF16), jax.ShapeDtypeStruct((B, Ls, dil * W), F32)),
        scratch_shapes=[pltpu.VMEM((tq + 2 * BAND_HALF, W), BF16)] * 2,
        compiler_params=_cparams(("parallel", "parallel", "parallel")), name=f"band_d{dil}",
    )(qv, kv, kv, kv, vv, vv, vv)
    return o.reshape(B * S, W), lse.reshape(B * S, W)


def _mla_kernel(q_ref, k_ref, v_ref, o_ref):
    lane = lax.broadcasted_iota(jnp.int32, (1, LANES), 1)
    for pair in range(MLA_HEADS // 2):
        vv = v_ref[0, :, pair * LANES:(pair + 1) * LANES]
        outs = []
        for sub in range(2):
            h = 2 * pair + sub
            lo, hi = h * MLA_HEAD_PAD, (h + 1) * MLA_HEAD_PAD
            s = _dot_nt(q_ref[0, :, lo:hi], k_ref[0, :, lo:hi])
            m = jnp.max(s, axis=-1, keepdims=True)
            e = jnp.exp(s - m)
            l = jnp.sum(e, axis=-1, keepdims=True)
            outs.append(_dot(e.astype(BF16), vv) * (1.0 / l))
        o_ref[0, :, pair * LANES:(pair + 1) * LANES] = jnp.where(lane < MLA_V, outs[0], outs[1]).astype(BF16)


def _mla(qm, km, vm, B, S, tq):
    QW = MLA_HEADS * MLA_HEAD_PAD
    q3, k3, v3 = qm.reshape(B, S, QW), km.reshape(B, S, QW), vm.reshape(B, S, MLA_WIDTH)
    o = pl.pallas_call(
        _mla_kernel, grid=(B, S // tq),
        in_specs=[pl.BlockSpec((1, tq, QW), lambda b, i: (b, i, 0)),
                  pl.BlockSpec((1, S, QW), lambda b, i: (b, 0, 0)),
                  pl.BlockSpec((1, S, MLA_WIDTH), lambda b, i: (b, 0, 0))],
        out_specs=pl.BlockSpec((1, tq, MLA_WIDTH), lambda b, i: (b, i, 0)),
        out_shape=jax.ShapeDtypeStruct((B, S, MLA_WIDTH), BF16),
        compiler_params=_cparams(("parallel", "arbitrary")), name="mla_attn",
    )(q3, k3, v3)
    return o.reshape(B * S, MLA_WIDTH)


def _mix_out_kernel(*refs, moe):
    (x_ref, hf_ref, hb_ref, gy_ref, o1_ref, o2_ref, o3_ref, l1_ref, l2_ref, l3_ref, mo_ref,
     gl_ref, gd_ref, gm_ref, wout_ref, gffn_ref) = refs[:16]
    if moe:
        rw_ref, x1_ref, h2_ref, gates_ref = refs[16:]
    else:
        x1_ref, h2_ref = refs[16:]
    lru = (hf_ref[...] + hb_ref[...]) * gy_ref[...].astype(F32)
    l1, l2, l3 = l1_ref[...], l2_ref[...], l3_ref[...]
    mx = jnp.maximum(jnp.maximum(l1, l2), l3)
    w1, w2, w3 = jnp.exp(l1 - mx), jnp.exp(l2 - mx), jnp.exp(l3 - mx)
    dil = (w1 * o1_ref[...].astype(F32) + w2 * o2_ref[...].astype(F32) + w3 * o3_ref[...].astype(F32)) / (w1 + w2 + w3)
    mix = jnp.concatenate([
        _rms(lru, gl_ref[...], LRU_WIDTH), _rms(dil, gd_ref[...], DIL_WIDTH),
        _rms(mo_ref[...].astype(F32), gm_ref[...], MLA_WIDTH)], axis=-1).astype(BF16)
    x1 = x_ref[...] + _dot(mix, wout_ref[...])
    x1_ref[...] = x1
    h2 = _rms(x1, gffn_ref[...], D_MODEL)
    h2_ref[...] = h2.astype(BF16)
    if moe:
        logits = jnp.dot(h2, rw_ref[...], preferred_element_type=F32, precision=lax.Precision.HIGHEST)
        lane = lax.broadcasted_iota(jnp.int32, logits.shape, 1)
        neg = -jnp.inf
        lg = jnp.where(lane < N_EXPERTS, logits, neg)
        m1 = jnp.max(lg, axis=-1, keepdims=True)
        i1 = jnp.min(jnp.where(lg == m1, lane, LANES), axis=-1, keepdims=True)
        lg2 = jnp.where(lane == i1, neg, lg)
        m2 = jnp.max(lg2, axis=-1, keepdims=True)
        i2 = jnp.min(jnp.where(lg2 == m2, lane, LANES), axis=-1, keepdims=True)
        e2 = jnp.exp(m2 - m1)
        den = 1.0 + e2
        gates_ref[...] = jnp.where(lane == i1, 1.0 / den, 0.0) + jnp.where(lane == i2, e2 / den, 0.0)


def _mix_out(x2d, hf_t, hb_t, gy, bands, mo, lw, B, S, tm, moe):
    T = B * S
    ns = S // tm
    tok = lambda b, s: (b * ns + s, 0)
    const = lambda b, s: (0, 0)
    tokspec = lambda c: pl.BlockSpec((tm, c), tok)
    tmaj = pl.BlockSpec((tm, LRU_WIDTH), lambda b, s: (s, b))
    full = lambda a: pl.BlockSpec(a.shape, const)
    (o1, l1), (o2, l2), (o3, l3) = bands
    args = [x2d, hf_t, hb_t, gy, o1, o2, o3, l1, l2, l3, mo,
            lw['g_l'], lw['g_d'], lw['g_m'], lw['w_out'], lw['g_ffn']]
    in_specs = [tokspec(D_MODEL), tmaj, tmaj, tokspec(LRU_WIDTH)] + [tokspec(DIL_WIDTH)] * 6 + [
        tokspec(MLA_WIDTH), full(lw['g_l']), full(lw['g_d']), full(lw['g_m']), full(lw['w_out']), full(lw['g_ffn'])]
    out_shape = [jax.ShapeDtypeStruct((T, D_MODEL), F32), jax.ShapeDtypeStruct((T, D_MODEL), BF16)]
    out_specs = [tokspec(D_MODEL), tokspec(D_MODEL)]
    if moe:
        args.append(lw['router_w'])
        in_specs.append(full(lw['router_w']))
        out_shape.append(jax.ShapeDtypeStruct((T, LANES), F32))
        out_specs.append(tokspec(LANES))
    return pl.pallas_call(
        functools.partial(_mix_out_kernel, moe=moe), grid=(B, ns), in_specs=in_specs,
        out_specs=tuple(out_specs), out_shape=tuple(out_shape),
        compiler_params=_cparams(("parallel", "parallel")), name="mix_out",
    )(*args)


def _ffn_kernel(*refs, final):
    if final:
        h_ref, x1_ref, wg_ref, wu_ref, wd_ref, gfin_ref, o_ref = refs
    else:
        h_ref, x1_ref, wg_ref, wu_ref, wd_ref, o_ref = refs
    f = pl.program_id(1)

    @pl.when(f == 0)
    def _():
        o_ref[...] = x1_ref[...]

    h = h_ref[...]
    g = _dot(h, wg_ref[...])
    u = _dot(h, wu_ref[...])
    a = (g * jax.nn.sigmoid(g) * u).astype(BF16)
    o_ref[...] += _dot(a, wd_ref[...])

    if final:
        @pl.when(f == pl.num_programs(1) - 1)
        def _():
            o_ref[...] = _rms(o_ref[...], gfin_ref[...], D_MODEL)


def _ffn(h2, x1, wg, wu, wd, gfin, tm, tf):
    T = h2.shape[0]
    F = wg.shape[1]
    final = gfin is not None
    args = [h2, x1, wg, wu, wd] + ([gfin] if final else [])
    in_specs = [pl.BlockSpec((tm, D_MODEL), lambda i, f: (i, 0)), pl.BlockSpec((tm, D_MODEL), lambda i, f: (i, 0)),
                pl.BlockSpec((D_MODEL, tf), lambda i, f: (0, f)), pl.BlockSpec((D_MODEL, tf), lambda i, f: (0, f)),
                pl.BlockSpec((tf, D_MODEL), lambda i, f: (f, 0))]
    if final:
        in_specs.append(pl.BlockSpec(gfin.shape, lambda i, f: (0, 0)))
    return pl.pallas_call(
        functools.partial(_ffn_kernel, final=final), grid=(T // tm, F // tf), in_specs=in_specs,
        out_specs=pl.BlockSpec((tm, D_MODEL), lambda i, f: (i, 0)),
        out_shape=jax.ShapeDtypeStruct((T, D_MODEL), F32),
        compiler_params=_cparams(("parallel", "arbitrary")), name="ffn",
    )(*args)


def _moe_kernel(*refs, final):
    if final:
        h_ref, x1_ref, gates_ref, wg_ref, wu_ref, wd_ref, gfin_ref, o_ref = refs
    else:
        h_ref, x1_ref, gates_ref, wg_ref, wu_ref, wd_ref, o_ref = refs
    e = pl.program_id(1)
    f = pl.program_id(2)

    @pl.when((e == 0) & (f == 0))
    def _():
        o_ref[...] = x1_ref[...]

    gates = gates_ref[...]
    lane = lax.broadcasted_iota(jnp.int32, gates.shape, 1)
    gate = jnp.sum(jnp.where(lane == e, gates, 0.0), axis=-1, keepdims=True)
    h = h_ref[...]
    g = _dot(h, wg_ref[0])
    u = _dot(h, wu_ref[0])
    a = (g * jax.nn.sigmoid(g) * u * gate).astype(BF16)
    o_ref[...] += _dot(a, wd_ref[0])

    if final:
        @pl.when((e == pl.num_programs(1) - 1) & (f == pl.num_programs(2) - 1))
        def _():
            o_ref[...] = _rms(o_ref[...], gfin_ref[...], D_MODEL)


def _moe(h2, x1, gates, wg, wu, wd, gfin, tm, tf):
    T = h2.shape[0]
    E, _, F = wg.shape
    final = gfin is not None
    args = [h2, x1, gates, wg, wu, wd] + ([gfin] if final else [])
    tokmap = lambda i, e, f: (i, 0)
    in_specs = [pl.BlockSpec((tm, D_MODEL), tokmap), pl.BlockSpec((tm, D_MODEL), tokmap),
                pl.BlockSpec((tm, LANES), tokmap),
                pl.BlockSpec((1, D_MODEL, tf), lambda i, e, f: (e, 0, f)),
                pl.BlockSpec((1, D_MODEL, tf), lambda i, e, f: (e, 0, f)),
                pl.BlockSpec((1, tf, D_MODEL), lambda i, e, f: (e, f, 0))]
    if final:
        in_specs.append(pl.BlockSpec(gfin.shape, lambda i, e, f: (0, 0)))
    return pl.pallas_call(
        functools.partial(_moe_kernel, final=final), grid=(T // tm, E, F // tf), in_specs=in_specs,
        out_specs=pl.BlockSpec((tm, D_MODEL), tokmap),
        out_shape=jax.ShapeDtypeStruct((T, D_MODEL), F32),
        compiler_params=_cparams(("parallel", "arbitrary", "arbitrary")), name="moe",
    )(*args)


def _swap_halves(w, heads, dim):
    k = w.shape[0]
    return w.reshape(k, heads, 2, dim // 2)[:, :, ::-1, :].reshape(k, heads * dim)


def _pad_heads(w, heads, dim, at=0):
    k = w.shape[0]
    w = w.reshape(k, heads, dim)
    w = jnp.pad(w, ((0, 0), (0, 0), (at, MLA_HEAD_PAD - at - dim)))
    return w.reshape(k, heads * MLA_HEAD_PAD)


def _block_diag(w):
    nb, bs, _ = w.shape
    eye = jnp.eye(nb, dtype=w.dtype)
    return (eye[:, None, :, None] * w[:, :, None, :]).reshape(nb * bs, nb * bs)


def _rope_tables(S):
    def cs(dim):
        half = dim // 2
        freqs = jnp.power(jnp.float32(ROPE_THETA), -jnp.arange(half, dtype=F32) * 2.0 / dim)
        ang = jnp.arange(S, dtype=F32)[:, None] * freqs[None, :]
        c, s = jnp.cos(ang), jnp.sin(ang)
        return jnp.concatenate([c, c], -1), jnp.concatenate([-s, s], -1)

    cd, sd = cs(DIL_HEAD_DIM)
    cm, sm = cs(MLA_ROPE)
    ones = jnp.ones((S, MLA_NOPE), F32)
    pad = MLA_HEAD_PAD - MLA_NOPE - MLA_ROPE
    return dict(
        cosd=jnp.tile(cd, (1, DIL_HEADS)), sind=jnp.tile(sd, (1, DIL_HEADS)),
        cosm=jnp.concatenate([ones, cm, jnp.ones((S, pad), F32)], -1),
        sinm=jnp.concatenate([0.0 * ones, sm, jnp.zeros((S, pad), F32)], -1))


def _prep_layer(l, p):
    w_in = p['w_in'][l]
    offs = [0, 256, 512, 768, 1024, 1280, 1536, 1664, 1696]
    xr, yr, qd, kd, vd, cq, ckv, kr = [w_in[:, a:b] for a, b in zip(offs[:-1], offs[1:])]
    kr_pad = _pad_heads(kr, 1, MLA_ROPE, at=MLA_NOPE)
    krs_pad = _pad_heads(_swap_halves(kr, 1, MLA_ROPE), 1, MLA_ROPE, at=MLA_NOPE)
    w1 = jnp.concatenate([xr, yr, qd, _swap_halves(qd, DIL_HEADS, DIL_HEAD_DIM), kd,
                          _swap_halves(kd, DIL_HEADS, DIL_HEAD_DIM), vd, cq, ckv, kr_pad, krs_pad], axis=1)

    w_uq = p['mla_w_uq'][l].reshape(MLA_Q_RANK, MLA_HEADS, MLA_NOPE + MLA_ROPE)
    qn, qr = w_uq[:, :, :MLA_NOPE], w_uq[:, :, MLA_NOPE:]
    qr_sw = qr.reshape(MLA_Q_RANK, MLA_HEADS, 2, MLA_ROPE // 2)[:, :, ::-1, :].reshape(qr.shape)
    padq = lambda a, b: jnp.pad(jnp.concatenate([a, b], -1), ((0, 0), (0, 0), (0, MLA_HEAD_PAD - MLA_NOPE - MLA_ROPE))
                                ).reshape(MLA_Q_RANK, MLA_HEADS * MLA_HEAD_PAD)
    w_uq2 = jnp.concatenate([padq(qn, qr), padq(jnp.zeros_like(qn), qr_sw)], axis=1)

    w_ukv = p['mla_w_ukv'][l].reshape(MLA_KV_RANK, MLA_HEADS, MLA_NOPE + MLA_V)
    kn = _pad_heads(w_ukv[:, :, :MLA_NOPE].reshape(MLA_KV_RANK, -1), MLA_HEADS, MLA_NOPE)
    vm = w_ukv[:, :, MLA_NOPE:].reshape(MLA_KV_RANK, MLA_WIDTH)
    w_ukv2 = jnp.concatenate([kn, vm], axis=1)

    rg_w = jnp.stack([jnp.concatenate([_block_diag(p['rg_w_a'][l, d]), _block_diag(p['rg_w_i'][l, d])], axis=1)
                      for d in range(2)])
    rg_b = jnp.stack([jnp.concatenate([p['rg_b_a'][l, d], p['rg_b_i'][l, d]])[None, :] for d in range(2)])
    g_out = p['mix_out_norm_g'][l]
    row = lambda v: v.reshape(1, -1).astype(F32)
    return dict(
        g_mix=row(p['norm_mix_g'][l]), w1=w1.astype(BF16), g_q=row(p['mla_q_norm_g'][l]), w_uq=w_uq2.astype(BF16),
        g_kv=row(p['mla_kv_norm_g'][l]), w_ukv=w_ukv2.astype(BF16),
        conv_w=p['conv_w'][l].astype(F32), conv_b=row(p['conv_b'][l]), rg_w=rg_w.astype(BF16), rg_b=rg_b.astype(F32),
        rg_lam=p['rg_lambda'][l].reshape(2, 1, LRU_WIDTH).astype(F32),
        g_l=row(g_out[:LRU_WIDTH]), g_d=row(g_out[LRU_WIDTH:LRU_WIDTH + DIL_WIDTH]),
        g_m=row(g_out[LRU_WIDTH + DIL_WIDTH:]), w_out=p['w_out'][l].astype(BF16), g_ffn=row(p['norm_ffn_g'][l]))


def _trunk(x, layers, p, depth):
    B, S, _ = x.shape
    tabs = _rope_tables(S)
    x2d = x.reshape(B * S, D_MODEL)
    tm = 512
    for l in range(depth):
        lw = layers[l]
        moe = l % 2 == 1
        last = l == depth - 1
        gfin = p['final_norm_g'].reshape(1, D_MODEL) if last else None
        xr_t, gy, q, k, v, qm, km, vm = _in_proj(x2d, lw, tabs, B, S, tm)
        hf_t, hb_t = _lru(xr_t, lw, B, S)
        bands = [_banded(q, k, v, B, S, dil) for _, dil in DIL_PATTERNS]
        mo = _mla(qm, km, vm, B, S, 256)
        j = l // 2
        if moe:
            lw = dict(lw, router_w=jnp.pad(p['router_w'][j], ((0, 0), (0, LANES - N_EXPERTS))))
            x1, h2, gates = _mix_out(x2d, hf_t, hb_t, gy, bands, mo, lw, B, S, tm, True)
            x2d = _moe(h2, x1, gates, layers[l]['moe_wg'], layers[l]['moe_wu'], layers[l]['moe_wd'], gfin, tm, 896)
        else:
            x1, h2 = _mix_out(x2d, hf_t, hb_t, gy, bands, mo, lw, B, S, tm, False)
            x2d = _ffn(h2, x1, layers[l]['ffn_wg'], layers[l]['ffn_wu'], layers[l]['ffn_wd'], gfin, tm, 1408)
    return x2d.reshape(B, S, D_MODEL)


def kernel(x_prompt, x_sample, norm_mix_g, w_in, conv_w, conv_b, rg_w_a, rg_b_a, rg_w_i, rg_b_i, rg_lambda, mla_q_norm_g, mla_w_uq, mla_kv_norm_g, mla_w_ukv, mix_out_norm_g, w_out, norm_ffn_g, ffn_w_gate, ffn_w_up, ffn_w_down, router_w, moe_w_gate, moe_w_up, moe_w_down, final_norm_g):
    p = dict(norm_mix_g=norm_mix_g, w_in=w_in, conv_w=conv_w, conv_b=conv_b, rg_w_a=rg_w_a, rg_b_a=rg_b_a,
             rg_w_i=rg_w_i, rg_b_i=rg_b_i, rg_lambda=rg_lambda, mla_q_norm_g=mla_q_norm_g, mla_w_uq=mla_w_uq,
             mla_kv_norm_g=mla_kv_norm_g, mla_w_ukv=mla_w_ukv, mix_out_norm_g=mix_out_norm_g, w_out=w_out,
             norm_ffn_g=norm_ffn_g, router_w=router_w, final_norm_g=final_norm_g)
    depth = w_in.shape[0]
    layers = []
    for l in range(depth):
        lw = _prep_layer(l, p)
        j = l // 2
        if l % 2 == 1:
            lw.update(moe_wg=moe_w_gate[j].astype(BF16), moe_wu=moe_w_up[j].astype(BF16),
                      moe_wd=moe_w_down[j].astype(BF16))
        else:
            lw.update(ffn_wg=ffn_w_gate[j].astype(BF16), ffn_wu=ffn_w_up[j].astype(BF16),
                      ffn_wd=ffn_w_down[j].astype(BF16))
        layers.append(lw)
    return (_trunk(x_prompt, layers, p, depth), _trunk(x_sample, layers, p, depth))
```

```python
import functools

import jax
import jax.numpy as jnp
from jax import lax
from jax.experimental import pallas as pl
from jax.experimental.pallas import tpu as pltpu

F32 = jnp.float32
BF16 = jnp.bfloat16

D_MODEL = 1024
NORM_EPS = 1e-6
ROPE_THETA = 10000.0
LRU_WIDTH = 256
LRU_BLOCKS = 4
CONV_WIDTH = 4
LRU_C = 8.0
DIL_HEADS = 4
DIL_HEAD_DIM = 64
DIL_WIDTH = DIL_HEADS * DIL_HEAD_DIM
DIL_PATTERNS = ((128, 1), (512, 4), (2048, 16))
MLA_HEADS = 8
MLA_NOPE = 64
MLA_ROPE = 32
MLA_V = 64
MLA_Q_RANK = 256
MLA_KV_RANK = 128
MLA_WIDTH = MLA_HEADS * MLA_V
N_EXPERTS = 8
TOP_K = 2

LANES = 128
MLA_HEAD_PAD = LANES
VMEM_LIMIT = 56 * 1024 * 1024

_C_XR, _C_YR, _C_Q, _C_QS, _C_K, _C_KS, _C_V, _C_CQ, _C_CKV, _C_KR, _C_KRS, _C_END = (
    0, 256, 512, 768, 1024, 1280, 1536, 1792, 2048, 2176, 2304, 2432)


def _cparams(sem):
    return pltpu.CompilerParams(dimension_semantics=sem, vmem_limit_bytes=VMEM_LIMIT)


def _rms(x, g, n):
    ms = jnp.sum(x * x, axis=-1, keepdims=True) * (1.0 / n)
    return x * lax.rsqrt(ms + NORM_EPS) * g


def _gelu_tanh(x):
    return 0.5 * x * (1.0 + jnp.tanh(0.7978845608028654 * (x + 0.044715 * (x * x * x))))


def _dot(a, b):
    return jnp.dot(a, b, preferred_element_type=F32)


def _dot_nt(a, b):
    return lax.dot_general(a, b, (((1,), (1,)), ((), ())), preferred_element_type=F32)


def _in_proj_kernel(x_ref, g_ref, w1_ref, gq_ref, wuq_ref, gkv_ref, wukv_ref,
                    cosd_ref, sind_ref, cosm_ref, sinm_ref,
                    xr_ref, gy_ref, q_ref, k_ref, v_ref, qm_ref, km_ref, vm_ref):
    h = _rms(x_ref[...], g_ref[...], D_MODEL).astype(BF16)
    p = _dot(h, w1_ref[...])
    xr_ref[...] = p[:, _C_XR:_C_YR]
    gy_ref[...] = _gelu_tanh(p[:, _C_YR:_C_Q]).astype(BF16)
    cosd, sind = cosd_ref[...], sind_ref[...]
    q = p[:, _C_Q:_C_QS] * cosd + p[:, _C_QS:_C_K] * sind
    q_ref[...] = (q * (DIL_HEAD_DIM ** -0.5)).astype(BF16)
    k_ref[...] = (p[:, _C_K:_C_KS] * cosd + p[:, _C_KS:_C_V] * sind).astype(BF16)
    v_ref[...] = p[:, _C_V:_C_CQ].astype(BF16)

    cosm, sinm = cosm_ref[...], sinm_ref[...]
    cqn = _rms(p[:, _C_CQ:_C_CKV], gq_ref[...], MLA_Q_RANK).astype(BF16)
    qq = _dot(cqn, wuq_ref[...])
    ckvn = _rms(p[:, _C_CKV:_C_KR], gkv_ref[...], MLA_KV_RANK).astype(BF16)
    kv = _dot(ckvn, wukv_ref[...])
    kr = p[:, _C_KR:_C_KRS] * cosm + p[:, _C_KRS:_C_END] * sinm
    scale = (MLA_NOPE + MLA_ROPE) ** -0.5
    nq = MLA_HEADS * MLA_HEAD_PAD
    for hh in range(MLA_HEADS):
        lo, hi = hh * MLA_HEAD_PAD, (hh + 1) * MLA_HEAD_PAD
        qh = qq[:, lo:hi] * cosm + qq[:, nq + lo:nq + hi] * sinm
        qm_ref[:, lo:hi] = (qh * scale).astype(BF16)
        km_ref[:, lo:hi] = (kv[:, lo:hi] + kr).astype(BF16)
    vm_ref[...] = kv[:, nq:].astype(BF16)


def _in_proj(x2d, lw, tabs, B, S, tm):
    T = B * S
    ns = S // tm
    tok = lambda b, s: (b * ns + s, 0)
    pos = lambda b, s: (s, 0)
    const = lambda b, s: (0, 0)

    def full(a):
        return pl.BlockSpec(a.shape, const)

    def tokspec(c):
        return pl.BlockSpec((tm, c), tok)

    out_shape = (
        jax.ShapeDtypeStruct((S, B * LRU_WIDTH), F32),
        jax.ShapeDtypeStruct((T, LRU_WIDTH), BF16),
        jax.ShapeDtypeStruct((T, DIL_WIDTH), BF16),
        jax.ShapeDtypeStruct((T, DIL_WIDTH), BF16),
        jax.ShapeDtypeStruct((T, DIL_WIDTH), BF16),
        jax.ShapeDtypeStruct((T, MLA_HEADS * MLA_HEAD_PAD), BF16),
        jax.ShapeDtypeStruct((T, MLA_HEADS * MLA_HEAD_PAD), BF16),
        jax.ShapeDtypeStruct((T, MLA_WIDTH), BF16),
    )
    out_specs = (
        pl.BlockSpec((tm, LRU_WIDTH), lambda b, s: (s, b)),
        tokspec(LRU_WIDTH), tokspec(DIL_WIDTH), tokspec(DIL_WIDTH), tokspec(DIL_WIDTH),
        tokspec(MLA_HEADS * MLA_HEAD_PAD), tokspec(MLA_HEADS * MLA_HEAD_PAD), tokspec(MLA_WIDTH),
    )
    in_specs = [
        tokspec(D_MODEL), full(lw['g_mix']), full(lw['w1']), full(lw['g_q']), full(lw['w_uq']),
        full(lw['g_kv']), full(lw['w_ukv']),
        pl.BlockSpec((tm, DIL_WIDTH), pos), pl.BlockSpec((tm, DIL_WIDTH), pos),
        pl.BlockSpec((tm, LANES), pos), pl.BlockSpec((tm, LANES), pos),
    ]
    return pl.pallas_call(
        _in_proj_kernel, grid=(B, ns), in_specs=in_specs, out_specs=out_specs, out_shape=out_shape,
        compiler_params=_cparams(("parallel", "parallel")), name="in_proj",
    )(x2d, lw['g_mix'], lw['w1'], lw['g_q'], lw['w_uq'], lw['g_kv'], lw['w_ukv'],
      tabs['cosd'], tabs['sind'], tabs['cosm'], tabs['sinm'])


def _lru_kernel(xf_ref, xfp_ref, xfn_ref, xb_ref, xbp_ref, xbn_ref, cw_ref, cb_ref, wg_ref, bg_ref,
                lam_ref, hf_ref, hb_ref, xpad, a_f, b_f, a_b, b_b, hcar, *, B, R):
    i = pl.program_id(0)
    n = pl.num_programs(0)
    tc = R // B

    @pl.when(i == 0)
    def _():
        hcar[...] = jnp.zeros_like(hcar)

    def prep(x_ref, xp_ref, xn_ref, ci, d, a_s, b_s):
        xpad[0:2 * B, :] = jnp.where(ci > 0, xp_ref[...], 0.0)
        xpad[2 * B:2 * B + R, :] = x_ref[...]
        xpad[2 * B + R:3 * B + R, :] = jnp.where(ci < n - 1, xn_ref[0:B, :], 0.0)
        xc = cb_ref[...] + xpad[0:R, :] * cw_ref[0:1, :]
        for kk in range(1, CONV_WIDTH):
            xc = xc + xpad[kk * B:kk * B + R, :] * cw_ref[kk:kk + 1, :]
        g = _dot(xc.astype(BF16), wg_ref[d]) + bg_ref[d]
        r = jax.nn.sigmoid(g[:, :LRU_WIDTH])
        ig = jax.nn.sigmoid(g[:, LRU_WIDTH:])
        lam = lam_ref[d]
        softplus = jnp.maximum(-lam, 0.0) + jnp.log(1.0 + jnp.exp(-jnp.abs(lam)))
        log_a = (-LRU_C) * r * softplus
        a_s[...] = jnp.exp(log_a)
        b_s[...] = jnp.sqrt(1.0 - jnp.exp(2.0 * log_a)) * (ig * xc)

    prep(xf_ref, xfp_ref, xfn_ref, i, 0, a_f, b_f)
    prep(xb_ref, xbp_ref, xbn_ref, n - 1 - i, 1, a_b, b_b)

    def body(s, carry):
        hf, hb = carry
        rf = pl.multiple_of(s * B, B)
        hf = a_f[pl.ds(rf, B), :] * hf + b_f[pl.ds(rf, B), :]
        hf_ref[pl.ds(rf, B), :] = hf
        rb = pl.multiple_of((tc - 1 - s) * B, B)
        hb = a_b[pl.ds(rb, B), :] * hb + b_b[pl.ds(rb, B), :]
        hb_ref[pl.ds(rb, B), :] = hb
        return hf, hb

    hf, hb = lax.fori_loop(0, tc, body, (hcar[0], hcar[1]), unroll=8)
    hcar[0] = hf
    hcar[1] = hb


def _lru(xr_t, lw, B, S):
    rows = S * B
    R = 1024
    HB = 2 * B
    n = rows // R
    nh = rows // HB
    x2 = xr_t.reshape(rows, LRU_WIDTH)
    rb = R // HB

    main_f = pl.BlockSpec((R, LRU_WIDTH), lambda i: (i, 0))
    prev_f = pl.BlockSpec((HB, LRU_WIDTH), lambda i: (jnp.maximum(i * rb - 1, 0), 0))
    next_f = pl.BlockSpec((HB, LRU_WIDTH), lambda i: (jnp.minimum((i + 1) * rb, nh - 1), 0))
    main_b = pl.BlockSpec((R, LRU_WIDTH), lambda i: (n - 1 - i, 0))
    prev_b = pl.BlockSpec((HB, LRU_WIDTH), lambda i: (jnp.maximum((n - 1 - i) * rb - 1, 0), 0))
    next_b = pl.BlockSpec((HB, LRU_WIDTH), lambda i: (jnp.minimum((n - i) * rb, nh - 1), 0))

    def full(a):
        nd = a.ndim
        return pl.BlockSpec(a.shape, lambda i: (0,) * nd)

    hf, hb = pl.pallas_call(
        functools.partial(_lru_kernel, B=B, R=R),
        grid=(n,),
        in_specs=[main_f, prev_f, next_f, main_b, prev_b, next_b,
                  full(lw['conv_w']), full(lw['conv_b']), full(lw['rg_w']), full(lw['rg_b']), full(lw['rg_lam'])],
        out_specs=(main_f, main_b),
        out_shape=(jax.ShapeDtypeStruct((rows, LRU_WIDTH), F32),) * 2,
        scratch_shapes=[pltpu.VMEM((R + 3 * B, LRU_WIDTH), F32)] + [pltpu.VMEM((R, LRU_WIDTH), F32)] * 4
        + [pltpu.VMEM((2, B, LRU_WIDTH), F32)],
        compiler_params=_cparams(("arbitrary",)), name="rglru",
    )(x2, x2, x2, x2, x2, x2, lw['conv_w'], lw['conv_b'], lw['rg_w'], lw['rg_b'], lw['rg_lam'])
    return hf.reshape(S, B * LRU_WIDTH), hb.reshape(S, B * LRU_WIDTH)


BAND_HALF = 64


def _band_kernel(q_ref, k_ref, kp_ref, kn_ref, v_ref, vp_ref, vn_ref, o_ref, lse_ref, kcat, vcat, *, tq):
    i = pl.program_id(2)
    nblk = pl.num_programs(2)
    hb = BAND_HALF
    kcat[0:hb, :] = kp_ref[0]
    kcat[hb:hb + tq, :] = k_ref[0]
    kcat[hb + tq:, :] = kn_ref[0]
    vcat[0:hb, :] = vp_ref[0]
    vcat[hb:hb + tq, :] = v_ref[0]
    vcat[hb + tq:, :] = vn_ref[0]
    q = q_ref[0]
    nk = tq + 2 * hb
    row = lax.broadcasted_iota(jnp.int32, (tq, nk), 0)
    col = lax.broadcasted_iota(jnp.int32, (tq, nk), 1)
    dist = col - hb - row
    valid = ((jnp.abs(dist) <= hb) & ((col >= hb) | (i > 0)) & ((col < tq + hb) | (i < nblk - 1)))
    head = lax.broadcasted_iota(jnp.int32, (1, DIL_WIDTH), 1) // DIL_HEAD_DIM
    acc_o = jnp.zeros((tq, DIL_WIDTH), F32)
    acc_l = jnp.zeros((tq, DIL_WIDTH), F32)
    kc = kcat[...]
    vc = vcat[...]
    for h in range(DIL_HEADS):
        hm = head == h
        qh = jnp.where(hm, q, jnp.zeros_like(q))
        s = jnp.where(valid, _dot_nt(qh, kc), -1e30)
        m = jnp.max(s, axis=-1, keepdims=True)
        e = jnp.exp(s - m)
        l = jnp.sum(e, axis=-1, keepdims=True)
        o = _dot(e.astype(BF16), vc) * (1.0 / l)
        acc_o = jnp.where(hm, o, acc_o)
        acc_l = jnp.where(hm, m + jnp.log(l), acc_l)
    o_ref[0] = acc_o.astype(BF16)
    lse_ref[0] = acc_l


def _banded(q, k, v, B, S, dil):
    Ls = S // dil
    tq = min(256, Ls)
    nb = Ls // tq
    nhb = Ls // BAND_HALF
    r64 = tq // BAND_HALF
    view = lambda a: a.reshape(B, Ls, dil * DIL_WIDTH)
    W = DIL_WIDTH
    own = pl.BlockSpec((1, tq, W), lambda b, r, i: (b, i, r))
    prev = pl.BlockSpec((1, BAND_HALF, W), lambda b, r, i: (b, jnp.maximum(i * r64 - 1, 0), r))
    nxt = pl.BlockSpec((1, BAND_HALF, W), lambda b, r, i: (b, jnp.minimum((i + 1) * r64, nhb - 1), r))
    qv, kv, vv = view(q), view(k), view(v)
    o, lse = pl.pallas_call(
        functools.partial(_band_kernel, tq=tq),
        grid=(B, dil, nb),
        in_specs=[own, own, prev, nxt, own, prev, nxt],
        out_specs=(own, own),
        out_shape=(jax.ShapeDtypeStruct((B, Ls, dil * W), BF16), jax.ShapeDtypeStruct((B, Ls, dil * W), F32)),
        scratch_shapes=[pltpu.VMEM((tq + 2 * BAND_HALF, W), BF16)] * 2,
        compiler_params=_cparams(("parallel", "parallel", "parallel")), name=f"band_d{dil}",
    )(qv, kv, kv, kv, vv, vv, vv)
    return o.reshape(B * S, W), lse.reshape(B * S, W)


def _mla_kernel(q_ref, k_ref, v_ref, o_ref):
    lane = lax.broadcasted_iota(jnp.int32, (1, LANES), 1)
    for pair in range(MLA_HEADS // 2):
        vv = v_ref[0, :, pair * LANES:(pair + 1) * LANES]
        outs = []
        for sub in range(2):
            h = 2 * pair + sub
            lo, hi = h * MLA_HEAD_PAD, (h + 1) * MLA_HEAD_PAD
            s = _dot_nt(q_ref[0, :, lo:hi], k_ref[0, :, lo:hi])
            m = jnp.max(s, axis=-1, keepdims=True)
            e = jnp.exp(s - m)
            l = jnp.sum(e, axis=-1, keepdims=True)
            outs.append(_dot(e.astype(BF16), vv) * (1.0 / l))
        o_ref[0, :, pair * LANES:(pair + 1) * LANES] = jnp.where(lane < MLA_V, outs[0], outs[1]).astype(BF16)


def _mla(qm, km, vm, B, S, tq):
    QW = MLA_HEADS * MLA_HEAD_PAD
    q3, k3, v3 = qm.reshape(B, S, QW), km.reshape(B, S, QW), vm.reshape(B, S, MLA_WIDTH)
    o = pl.pallas_call(
        _mla_kernel, grid=(B, S // tq),
        in_specs=[pl.BlockSpec((1, tq, QW), lambda b, i: (b, i, 0)),
                  pl.BlockSpec((1, S, QW), lambda b, i: (b, 0, 0)),
                  pl.BlockSpec((1, S, MLA_WIDTH), lambda b, i: (b, 0, 0))],
        out_specs=pl.BlockSpec((1, tq, MLA_WIDTH), lambda b, i: (b, i, 0)),
        out_shape=jax.ShapeDtypeStruct((B, S, MLA_WIDTH), BF16),
        compiler_params=_cparams(("parallel", "arbitrary")), name="mla_attn",
    )(q3, k3, v3)
    return o.reshape(B * S, MLA_WIDTH)


def _mix_out_kernel(*refs, moe):
    (x_ref, hf_ref, hb_ref, gy_ref, o1_ref, o2_ref, o3_ref, l1_ref, l2_ref, l3_ref, mo_ref,
     gl_ref, gd_ref, gm_ref, wout_ref, gffn_ref) = refs[:16]
    if moe:
        rw_ref, x1_ref, h2_ref, route_ref, cnt_ref, carry = refs[16:]
    else:
        x1_ref, h2_ref = refs[16:]
    lru = (hf_ref[...] + hb_ref[...]) * gy_ref[...].astype(F32)
    l1, l2, l3 = l1_ref[...], l2_ref[...], l3_ref[...]
    mx = jnp.maximum(jnp.maximum(l1, l2), l3)
    w1, w2, w3 = jnp.exp(l1 - mx), jnp.exp(l2 - mx), jnp.exp(l3 - mx)
    dil = (w1 * o1_ref[...].astype(F32) + w2 * o2_ref[...].astype(F32) + w3 * o3_ref[...].astype(F32)) / (w1 + w2 + w3)
    mix = jnp.concatenate([
        _rms(lru, gl_ref[...], LRU_WIDTH), _rms(dil, gd_ref[...], DIL_WIDTH),
        _rms(mo_ref[...].astype(F32), gm_ref[...], MLA_WIDTH)], axis=-1).astype(BF16)
    x1 = x_ref[...] + _dot(mix, wout_ref[...])
    x1_ref[...] = x1
    h2 = _rms(x1, gffn_ref[...], D_MODEL)
    h2_ref[...] = h2.astype(h2_ref.dtype)
    if moe:
        first = (pl.program_id(0) == 0) & (pl.program_id(1) == 0)

        @pl.when(first)
        def _():
            carry[...] = jnp.zeros_like(carry)

        logits = jnp.dot(h2, rw_ref[...], preferred_element_type=F32, precision=lax.Precision.HIGHEST)
        tm = logits.shape[0]
        lane = lax.broadcasted_iota(jnp.int32, logits.shape, 1)
        neg = -jnp.inf
        lg = jnp.where(lane < N_EXPERTS, logits, neg)
        m1 = jnp.max(lg, axis=-1, keepdims=True)
        i1 = jnp.min(jnp.where(lg == m1, lane, LANES), axis=-1, keepdims=True)
        lg2 = jnp.where(lane == i1, neg, lg)
        m2 = jnp.max(lg2, axis=-1, keepdims=True)
        i2 = jnp.min(jnp.where(lg2 == m2, lane, LANES), axis=-1, keepdims=True)
        e2 = jnp.exp(m2 - m1)
        den = 1.0 + e2
        sel1, sel2 = lane == i1, lane == i2
        onehot = jnp.where(sel1 | sel2, 1.0, 0.0)
        tri = (lax.broadcasted_iota(jnp.int32, (tm, tm), 0) > lax.broadcasted_iota(jnp.int32, (tm, tm), 1))
        before = _dot(jnp.where(tri, 1.0, 0.0).astype(BF16), onehot.astype(BF16)) + carry[0:1, :]
        rank1 = jnp.sum(jnp.where(sel1, before, 0.0), axis=-1, keepdims=True)
        rank2 = jnp.sum(jnp.where(sel2, before, 0.0), axis=-1, keepdims=True)
        carry[...] = carry[...] + jnp.sum(onehot, axis=0, keepdims=True)
        cnt_ref[...] = carry[...]
        cols = (i1.astype(F32), i2.astype(F32), 1.0 / den, e2 / den, rank1, rank2)
        route = jnp.zeros(logits.shape, F32)
        for c, val in enumerate(cols):
            route = jnp.where(lane == c, val, route)
        route_ref[...] = route


def _mix_out(x2d, hf_t, hb_t, gy, bands, mo, lw, B, S, tm, moe):
    T = B * S
    ns = S // tm
    tok = lambda b, s: (b * ns + s, 0)
    const = lambda b, s: (0, 0)
    tokspec = lambda c: pl.BlockSpec((tm, c), tok)
    tmaj = pl.BlockSpec((tm, LRU_WIDTH), lambda b, s: (s, b))
    full = lambda a: pl.BlockSpec(a.shape, const)
    (o1, l1), (o2, l2), (o3, l3) = bands
    args = [x2d, hf_t, hb_t, gy, o1, o2, o3, l1, l2, l3, mo,
            lw['g_l'], lw['g_d'], lw['g_m'], lw['w_out'], lw['g_ffn']]
    in_specs = [tokspec(D_MODEL), tmaj, tmaj, tokspec(LRU_WIDTH)] + [tokspec(DIL_WIDTH)] * 6 + [
        tokspec(MLA_WIDTH), full(lw['g_l']), full(lw['g_d']), full(lw['g_m']), full(lw['w_out']), full(lw['g_ffn'])]
    out_shape = [jax.ShapeDtypeStruct((T, D_MODEL), F32), jax.ShapeDtypeStruct((T, D_MODEL), F32 if moe else BF16)]
    out_specs = [tokspec(D_MODEL), tokspec(D_MODEL)]
    scratch = []
    if moe:
        args.append(lw['router_w'])
        in_specs.append(full(lw['router_w']))
        out_shape += [jax.ShapeDtypeStruct((T, LANES), F32), jax.ShapeDtypeStruct((8, LANES), F32)]
        out_specs += [tokspec(LANES), pl.BlockSpec((8, LANES), const)]
        scratch = [pltpu.VMEM((8, LANES), F32)]
    sem = ("arbitrary", "arbitrary") if moe else ("parallel", "parallel")
    return pl.pallas_call(
        functools.partial(_mix_out_kernel, moe=moe), grid=(B, ns), in_specs=in_specs,
        out_specs=tuple(out_specs), out_shape=tuple(out_shape), scratch_shapes=scratch,
        compiler_params=_cparams(sem), name="mix_out",
    )(*args)


def _ffn_kernel(*refs, final):
    if final:
        h_ref, x1_ref, wg_ref, wu_ref, wd_ref, gfin_ref, o_ref = refs
    else:
        h_ref, x1_ref, wg_ref, wu_ref, wd_ref, o_ref = refs
    f = pl.program_id(1)

    @pl.when(f == 0)
    def _():
        o_ref[...] = x1_ref[...]

    h = h_ref[...]
    g = _dot(h, wg_ref[...])
    u = _dot(h, wu_ref[...])
    a = (g * jax.nn.sigmoid(g) * u).astype(BF16)
    o_ref[...] += _dot(a, wd_ref[...])

    if final:
        @pl.when(f == pl.num_programs(1) - 1)
        def _():
            o_ref[...] = _rms(o_ref[...], gfin_ref[...], D_MODEL)


def _ffn(h2, x1, wg, wu, wd, gfin, tm, tf):
    T = h2.shape[0]
    F = wg.shape[1]
    final = gfin is not None
    args = [h2, x1, wg, wu, wd] + ([gfin] if final else [])
    in_specs = [pl.BlockSpec((tm, D_MODEL), lambda i, f: (i, 0)), pl.BlockSpec((tm, D_MODEL), lambda i, f: (i, 0)),
                pl.BlockSpec((D_MODEL, tf), lambda i, f: (0, f)), pl.BlockSpec((D_MODEL, tf), lambda i, f: (0, f)),
                pl.BlockSpec((tf, D_MODEL), lambda i, f: (f, 0))]
    if final:
        in_specs.append(pl.BlockSpec(gfin.shape, lambda i, f: (0, 0)))
    return pl.pallas_call(
        functools.partial(_ffn_kernel, final=final), grid=(T // tm, F // tf), in_specs=in_specs,
        out_specs=pl.BlockSpec((tm, D_MODEL), lambda i, f: (i, 0)),
        out_shape=jax.ShapeDtypeStruct((T, D_MODEL), F32),
        compiler_params=_cparams(("parallel", "arbitrary")), name="ffn",
    )(*args)


MOE_BLOCK_ROWS = 512
DISPATCH_TOKENS = 1024
COMBINE_TOKENS = 512


def _dispatch_kernel(pos_ref, h_hbm, xs_in, xs_out, sem, *, tmd):
    del xs_in
    base = pl.program_id(0) * tmd

    def body(t, c):
        src = h_hbm.at[pl.ds(base + t, 1), :]
        pltpu.make_async_copy(src, xs_out.at[pl.ds(pos_ref[0, 0, 2 * t], 1), :], sem).start()
        pltpu.make_async_copy(src, xs_out.at[pl.ds(pos_ref[0, 0, 2 * t + 1], 1), :], sem).start()
        return c

    lax.fori_loop(0, tmd, body, 0, unroll=4)
    for _ in range(TOP_K):
        pltpu.make_async_copy(h_hbm.at[pl.ds(0, tmd), :], xs_out.at[pl.ds(0, tmd), :], sem).wait()


def _dispatch(h2, pos, n_rows):
    T = h2.shape[0]
    tmd = DISPATCH_TOKENS
    pos3 = pos.reshape(T // tmd, 1, TOP_K * tmd)
    xs0 = jnp.zeros((n_rows, D_MODEL), F32)
    return pl.pallas_call(
        functools.partial(_dispatch_kernel, tmd=tmd), grid=(T // tmd,),
        in_specs=[pl.BlockSpec((1, 1, TOP_K * tmd), lambda i: (i, 0, 0), memory_space=pltpu.SMEM),
                  pl.BlockSpec(memory_space=pl.ANY), pl.BlockSpec(memory_space=pl.ANY)],
        out_specs=pl.BlockSpec(memory_space=pl.ANY),
        out_shape=jax.ShapeDtypeStruct((n_rows, D_MODEL), F32),
        scratch_shapes=[pltpu.SemaphoreType.DMA(())],
        input_output_aliases={2: 0},
        compiler_params=_cparams(("arbitrary",)), name="moe_dispatch",
    )(pos3, h2, xs0)


def _gmm_kernel(be_ref, xs_ref, wg_ref, wu_ref, wd_ref, y_ref, xb):
    del be_ref
    f = pl.program_id(1)

    @pl.when(f == 0)
    def _():
        xb[...] = xs_ref[...].astype(BF16)

    h = xb[...]
    g = _dot(h, wg_ref[0])
    u = _dot(h, wu_ref[0])
    part = _dot((g * jax.nn.sigmoid(g) * u).astype(BF16), wd_ref[0])

    @pl.when(f == 0)
    def _():
        y_ref[...] = part

    @pl.when(f > 0)
    def _():
        y_ref[...] += part


def _gmm(xs, block_expert, wg, wu, wd, tf):
    P = xs.shape[0]
    bm = MOE_BLOCK_ROWS
    F = wg.shape[2]
    grid_spec = pltpu.PrefetchScalarGridSpec(
        num_scalar_prefetch=1, grid=(P // bm, F // tf),
        in_specs=[pl.BlockSpec((bm, D_MODEL), lambda j, f, be: (j, 0)),
                  pl.BlockSpec((1, D_MODEL, tf), lambda j, f, be: (be[j], 0, f)),
                  pl.BlockSpec((1, D_MODEL, tf), lambda j, f, be: (be[j], 0, f)),
                  pl.BlockSpec((1, tf, D_MODEL), lambda j, f, be: (be[j], f, 0))],
        out_specs=pl.BlockSpec((bm, D_MODEL), lambda j, f, be: (j, 0)),
        scratch_shapes=[pltpu.VMEM((bm, D_MODEL), BF16)])
    return pl.pallas_call(
        _gmm_kernel, grid_spec=grid_spec, out_shape=jax.ShapeDtypeStruct((P, D_MODEL), F32),
        compiler_params=_cparams(("parallel", "arbitrary")), name="moe_gmm",
    )(block_expert, xs, wg, wu, wd)


def _combine_kernel(*refs, tmc, final):
    if final:
        pos_ref, route_ref, x1_ref, y_hbm, gfin_ref, o_ref, ybuf, sem = refs
    else:
        pos_ref, route_ref, x1_ref, y_hbm, o_ref, ybuf, sem = refs

    def body(t, c):
        for k in range(TOP_K):
            pltpu.make_async_copy(y_hbm.at[pl.ds(pos_ref[0, 0, TOP_K * t + k], 1), :],
                                  ybuf.at[k, pl.ds(t, 1), :], sem).start()
        return c

    lax.fori_loop(0, tmc, body, 0, unroll=4)
    for k in range(TOP_K):
        pltpu.make_async_copy(y_hbm.at[pl.ds(0, tmc), :], ybuf.at[k], sem).wait()
    route = route_ref[...]
    out = x1_ref[...] + route[:, 2:3] * ybuf[0] + route[:, 3:4] * ybuf[1]
    if final:
        out = _rms(out, gfin_ref[...], D_MODEL)
    o_ref[...] = out


def _combine(pos, route, x1, y, gfin):
    T = x1.shape[0]
    tmc = COMBINE_TOKENS
    final = gfin is not None
    pos3 = pos.reshape(T // tmc, 1, TOP_K * tmc)
    args = [pos3, route, x1, y] + ([gfin] if final else [])
    in_specs = [pl.BlockSpec((1, 1, TOP_K * tmc), lambda i: (i, 0, 0), memory_space=pltpu.SMEM),
                pl.BlockSpec((tmc, LANES), lambda i: (i, 0)), pl.BlockSpec((tmc, D_MODEL), lambda i: (i, 0)),
                pl.BlockSpec(memory_space=pl.ANY)]
    if final:
        in_specs.append(pl.BlockSpec(gfin.shape, lambda i: (0, 0)))
    return pl.pallas_call(
        functools.partial(_combine_kernel, tmc=tmc, final=final), grid=(T // tmc,), in_specs=in_specs,
        out_specs=pl.BlockSpec((tmc, D_MODEL), lambda i: (i, 0)),
        out_shape=jax.ShapeDtypeStruct((T, D_MODEL), F32),
        scratch_shapes=[pltpu.VMEM((TOP_K, tmc, D_MODEL), F32), pltpu.SemaphoreType.DMA(())],
        compiler_params=_cparams(("arbitrary",)), name="moe_combine",
    )(*args)


def _moe(h2, x1, route, counts, wg, wu, wd, gfin, tf):
    T = h2.shape[0]
    bm = MOE_BLOCK_ROWS
    n_blocks = TOP_K * T // bm + N_EXPERTS
    cnt = counts[0, :N_EXPERTS].astype(jnp.int32)
    padded = (cnt + bm - 1) // bm * bm
    ends = jnp.cumsum(padded)
    starts = ends - padded
    experts = route[:, 0:TOP_K].astype(jnp.int32)
    pos = starts[experts] + route[:, 4:4 + TOP_K].astype(jnp.int32)
    block_expert = jnp.minimum(
        jnp.searchsorted(ends, jnp.arange(n_blocks, dtype=jnp.int32) * bm, side='right'), N_EXPERTS - 1
    ).astype(jnp.int32)
    xs = _dispatch(h2, pos, n_blocks * bm)
    y = _gmm(xs, block_expert, wg, wu, wd, tf)
    return _combine(pos, route, x1, y, gfin)


def _swap_halves(w, heads, dim):
    k = w.shape[0]
    return w.reshape(k, heads, 2, dim // 2)[:, :, ::-1, :].reshape(k, heads * dim)


def _pad_heads(w, heads, dim, at=0):
    k = w.shape[0]
    w = w.reshape(k, heads, dim)
    w = jnp.pad(w, ((0, 0), (0, 0), (at, MLA_HEAD_PAD - at - dim)))
    return w.reshape(k, heads * MLA_HEAD_PAD)


def _block_diag(w):
    nb, bs, _ = w.shape
    eye = jnp.eye(nb, dtype=w.dtype)
    return (eye[:, None, :, None] * w[:, :, None, :]).reshape(nb * bs, nb * bs)


def _rope_tables(S):
    def cs(dim):
        half = dim // 2
        freqs = jnp.power(jnp.float32(ROPE_THETA), -jnp.arange(half, dtype=F32) * 2.0 / dim)
        ang = jnp.arange(S, dtype=F32)[:, None] * freqs[None, :]
        c, s = jnp.cos(ang), jnp.sin(ang)
        return jnp.concatenate([c, c], -1), jnp.concatenate([-s, s], -1)

    cd, sd = cs(DIL_HEAD_DIM)
    cm, sm = cs(MLA_ROPE)
    ones = jnp.ones((S, MLA_NOPE), F32)
    pad = MLA_HEAD_PAD - MLA_NOPE - MLA_ROPE
    return dict(
        cosd=jnp.tile(cd, (1, DIL_HEADS)), sind=jnp.tile(sd, (1, DIL_HEADS)),
        cosm=jnp.concatenate([ones, cm, jnp.ones((S, pad), F32)], -1),
        sinm=jnp.concatenate([0.0 * ones, sm, jnp.zeros((S, pad), F32)], -1))


def _prep_layer(l, p):
    w_in = p['w_in'][l]
    offs = [0, 256, 512, 768, 1024, 1280, 1536, 1664, 1696]
    xr, yr, qd, kd, vd, cq, ckv, kr = [w_in[:, a:b] for a, b in zip(offs[:-1], offs[1:])]
    kr_pad = _pad_heads(kr, 1, MLA_ROPE, at=MLA_NOPE)
    krs_pad = _pad_heads(_swap_halves(kr, 1, MLA_ROPE), 1, MLA_ROPE, at=MLA_NOPE)
    w1 = jnp.concatenate([xr, yr, qd, _swap_halves(qd, DIL_HEADS, DIL_HEAD_DIM), kd,
                          _swap_halves(kd, DIL_HEADS, DIL_HEAD_DIM), vd, cq, ckv, kr_pad, krs_pad], axis=1)

    w_uq = p['mla_w_uq'][l].reshape(MLA_Q_RANK, MLA_HEADS, MLA_NOPE + MLA_ROPE)
    qn, qr = w_uq[:, :, :MLA_NOPE], w_uq[:, :, MLA_NOPE:]
    qr_sw = qr.reshape(MLA_Q_RANK, MLA_HEADS, 2, MLA_ROPE // 2)[:, :, ::-1, :].reshape(qr.shape)
    padq = lambda a, b: jnp.pad(jnp.concatenate([a, b], -1), ((0, 0), (0, 0), (0, MLA_HEAD_PAD - MLA_NOPE - MLA_ROPE))
                                ).reshape(MLA_Q_RANK, MLA_HEADS * MLA_HEAD_PAD)
    w_uq2 = jnp.concatenate([padq(qn, qr), padq(jnp.zeros_like(qn), qr_sw)], axis=1)

    w_ukv = p['mla_w_ukv'][l].reshape(MLA_KV_RANK, MLA_HEADS, MLA_NOPE + MLA_V)
    kn = _pad_heads(w_ukv[:, :, :MLA_NOPE].reshape(MLA_KV_RANK, -1), MLA_HEADS, MLA_NOPE)
    vm = w_ukv[:, :, MLA_NOPE:].reshape(MLA_KV_RANK, MLA_WIDTH)
    w_ukv2 = jnp.concatenate([kn, vm], axis=1)

    rg_w = jnp.stack([jnp.concatenate([_block_diag(p['rg_w_a'][l, d]), _block_diag(p['rg_w_i'][l, d])], axis=1)
                      for d in range(2)])
    rg_b = jnp.stack([jnp.concatenate([p['rg_b_a'][l, d], p['rg_b_i'][l, d]])[None, :] for d in range(2)])
    g_out = p['mix_out_norm_g'][l]
    row = lambda v: v.reshape(1, -1).astype(F32)
    return dict(
        g_mix=row(p['norm_mix_g'][l]), w1=w1.astype(BF16), g_q=row(p['mla_q_norm_g'][l]), w_uq=w_uq2.astype(BF16),
        g_kv=row(p['mla_kv_norm_g'][l]), w_ukv=w_ukv2.astype(BF16),
        conv_w=p['conv_w'][l].astype(F32), conv_b=row(p['conv_b'][l]), rg_w=rg_w.astype(BF16), rg_b=rg_b.astype(F32),
        rg_lam=p['rg_lambda'][l].reshape(2, 1, LRU_WIDTH).astype(F32),
        g_l=row(g_out[:LRU_WIDTH]), g_d=row(g_out[LRU_WIDTH:LRU_WIDTH + DIL_WIDTH]),
        g_m=row(g_out[LRU_WIDTH + DIL_WIDTH:]), w_out=p['w_out'][l].astype(BF16), g_ffn=row(p['norm_ffn_g'][l]))


def _trunk(x, layers, p, depth):
    B, S, _ = x.shape
    tabs = _rope_tables(S)
    x2d = x.reshape(B * S, D_MODEL)
    tm = 512
    for l in range(depth):
        lw = layers[l]
        moe = l % 2 == 1
        last = l == depth - 1
        gfin = p['final_norm_g'].reshape(1, D_MODEL) if last else None
        xr_t, gy, q, k, v, qm, km, vm = _in_proj(x2d, lw, tabs, B, S, tm)
        hf_t, hb_t = _lru(xr_t, lw, B, S)
        bands = [_banded(q, k, v, B, S, dil) for _, dil in DIL_PATTERNS]
        mo = _mla(qm, km, vm, B, S, 256)
        j = l // 2
        if moe:
            lw = dict(lw, router_w=jnp.pad(p['router_w'][j], ((0, 0), (0, LANES - N_EXPERTS))))
            x1, h2, route, counts = _mix_out(x2d, hf_t, hb_t, gy, bands, mo, lw, B, S, tm, True)
            x2d = _moe(h2, x1, route, counts, layers[l]['moe_wg'], layers[l]['moe_wu'], layers[l]['moe_wd'], gfin, 896)
        else:
            x1, h2 = _mix_out(x2d, hf_t, hb_t, gy, bands, mo, lw, B, S, tm, False)
            x2d = _ffn(h2, x1, layers[l]['ffn_wg'], layers[l]['ffn_wu'], layers[l]['ffn_wd'], gfin, tm, 1408)
    return x2d.reshape(B, S, D_MODEL)


def kernel(x_prompt, x_sample, norm_mix_g, w_in, conv_w, conv_b, rg_w_a, rg_b_a, rg_w_i, rg_b_i, rg_lambda, mla_q_norm_g, mla_w_uq, mla_kv_norm_g, mla_w_ukv, mix_out_norm_g, w_out, norm_ffn_g, ffn_w_gate, ffn_w_up, ffn_w_down, router_w, moe_w_gate, moe_w_up, moe_w_down, final_norm_g):
    p = dict(norm_mix_g=norm_mix_g, w_in=w_in, conv_w=conv_w, conv_b=conv_b, rg_w_a=rg_w_a, rg_b_a=rg_b_a,
             rg_w_i=rg_w_i, rg_b_i=rg_b_i, rg_lambda=rg_lambda, mla_q_norm_g=mla_q_norm_g, mla_w_uq=mla_w_uq,
             mla_kv_norm_g=mla_kv_norm_g, mla_w_ukv=mla_w_ukv, mix_out_norm_g=mix_out_norm_g, w_out=w_out,
             norm_ffn_g=norm_ffn_g, router_w=router_w, final_norm_g=final_norm_g)
    depth = w_in.shape[0]
    layers = []
    for l in range(depth):
        lw = _prep_layer(l, p)
        j = l // 2
        if l % 2 == 1:
            lw.update(moe_wg=moe_w_gate[j].astype(BF16), moe_wu=moe_w_up[j].astype(BF16),
                      moe_wd=moe_w_down[j].astype(BF16))
        else:
            lw.update(ffn_wg=ffn_w_gate[j].astype(BF16), ffn_wu=ffn_w_up[j].astype(BF16),
                      ffn_wd=ffn_w_down[j].astype(BF16))
        layers.append(lw)
    return (_trunk(x_prompt, layers, p, depth), _trunk(x_sample, layers, p, depth))
```

```python
import functools
import math

import jax
import jax.numpy as jnp
from jax import lax
from jax.experimental import pallas as pl
from jax.experimental.pallas import tpu as pltpu

F32 = jnp.float32
BF16 = jnp.bfloat16

D_MODEL = 1024
NORM_EPS = 1e-6
ROPE_THETA = 10000.0
LRU_WIDTH = 256
LRU_BLOCKS = 4
CONV_WIDTH = 4
LRU_C = 8.0
DIL_HEADS = 4
DIL_HEAD_DIM = 64
DIL_WIDTH = DIL_HEADS * DIL_HEAD_DIM
DIL_PATTERNS = ((128, 1), (512, 4), (2048, 16))
MLA_HEADS = 8
MLA_NOPE = 64
MLA_ROPE = 32
MLA_V = 64
MLA_Q_RANK = 256
MLA_KV_RANK = 128
MLA_WIDTH = MLA_HEADS * MLA_V
N_EXPERTS = 8
TOP_K = 2

LANES = 128
SUBLANES = 8
MLA_HEAD_PAD = LANES
VMEM_LIMIT = 56 * 1024 * 1024
LOG2E = math.log2(math.e)

_C_XR, _C_YR, _C_Q, _C_QS, _C_K, _C_KS, _C_V, _C_CQ, _C_CKV, _C_KR, _C_KRS, _C_END = (
    0, 256, 512, 768, 1024, 1280, 1536, 1792, 2048, 2176, 2304, 2432)


def _cparams(sem):
    return pltpu.CompilerParams(dimension_semantics=sem, vmem_limit_bytes=VMEM_LIMIT)


def _rms(x, g, n):
    ms = jnp.sum(x * x, axis=-1, keepdims=True) * (1.0 / n)
    return x * lax.rsqrt(ms + NORM_EPS) * g


def _gelu_tanh(x):
    return 0.5 * x * (1.0 + jnp.tanh(0.7978845608028654 * (x + 0.044715 * (x * x * x))))


def _dot(a, b):
    return jnp.dot(a, b, preferred_element_type=F32)


def _dot_nt(a, b):
    return lax.dot_general(a, b, (((1,), (1,)), ((), ())), preferred_element_type=F32)


def _lane_groups(width):
    return [slice(c * LANES, (c + 1) * LANES) for c in range(width // LANES)]


def _in_proj_kernel(x_ref, g_ref, w1_ref, gq_ref, wuq_ref, gkv_ref, wukv_ref,
                    cosd_ref, sind_ref, cosm_ref, sinm_ref,
                    xr_ref, gy_ref, q_ref, k_ref, v_ref, qm_ref, km_ref, vm_ref):
    h = _rms(x_ref[...], g_ref[...], D_MODEL).astype(BF16)
    p = _dot(h, w1_ref[...])
    xr_ref[...] = p[:, _C_XR:_C_YR]
    gy_ref[...] = _gelu_tanh(p[:, _C_YR:_C_Q]).astype(BF16)
    cosd, sind = cosd_ref[...], sind_ref[...]
    q = p[:, _C_Q:_C_QS] * cosd + p[:, _C_QS:_C_K] * sind
    q_ref[...] = (q * (DIL_HEAD_DIM ** -0.5 * LOG2E)).astype(BF16)
    k_ref[...] = (p[:, _C_K:_C_KS] * cosd + p[:, _C_KS:_C_V] * sind).astype(BF16)
    v_ref[...] = p[:, _C_V:_C_CQ].astype(BF16)

    cosm, sinm = cosm_ref[...], sinm_ref[...]
    cqn = _rms(p[:, _C_CQ:_C_CKV], gq_ref[...], MLA_Q_RANK).astype(BF16)
    qq = _dot(cqn, wuq_ref[...])
    ckvn = _rms(p[:, _C_CKV:_C_KR], gkv_ref[...], MLA_KV_RANK).astype(BF16)
    kv = _dot(ckvn, wukv_ref[...])
    kr = p[:, _C_KR:_C_KRS] * cosm + p[:, _C_KRS:_C_END] * sinm
    scale = (MLA_NOPE + MLA_ROPE) ** -0.5 * LOG2E
    nq = MLA_HEADS * MLA_HEAD_PAD
    for hh in range(MLA_HEADS):
        lo, hi = hh * MLA_HEAD_PAD, (hh + 1) * MLA_HEAD_PAD
        qh = qq[:, lo:hi] * cosm + qq[:, nq + lo:nq + hi] * sinm
        qm_ref[:, lo:hi] = (qh * scale).astype(BF16)
        km_ref[:, lo:hi] = (kv[:, lo:hi] + kr).astype(BF16)
    vm_ref[...] = kv[:, nq:].astype(BF16)


def _in_proj(x2d, lw, tabs, B, S, tm):
    T = B * S
    ns = S // tm
    tok = lambda b, s: (b * ns + s, 0)
    pos = lambda b, s: (s, 0)
    const = lambda b, s: (0, 0)

    def full(a):
        return pl.BlockSpec(a.shape, const)

    def tokspec(c):
        return pl.BlockSpec((tm, c), tok)

    out_shape = (
        jax.ShapeDtypeStruct((T, LRU_WIDTH), F32),
        jax.ShapeDtypeStruct((T, LRU_WIDTH), BF16),
        jax.ShapeDtypeStruct((T, DIL_WIDTH), BF16),
        jax.ShapeDtypeStruct((T, DIL_WIDTH), BF16),
        jax.ShapeDtypeStruct((T, DIL_WIDTH), BF16),
        jax.ShapeDtypeStruct((T, MLA_HEADS * MLA_HEAD_PAD), BF16),
        jax.ShapeDtypeStruct((T, MLA_HEADS * MLA_HEAD_PAD), BF16),
        jax.ShapeDtypeStruct((T, MLA_WIDTH), BF16),
    )
    out_specs = (
        tokspec(LRU_WIDTH), tokspec(LRU_WIDTH), tokspec(DIL_WIDTH), tokspec(DIL_WIDTH), tokspec(DIL_WIDTH),
        tokspec(MLA_HEADS * MLA_HEAD_PAD), tokspec(MLA_HEADS * MLA_HEAD_PAD), tokspec(MLA_WIDTH),
    )
    in_specs = [
        tokspec(D_MODEL), full(lw['g_mix']), full(lw['w1']), full(lw['g_q']), full(lw['w_uq']),
        full(lw['g_kv']), full(lw['w_ukv']),
        pl.BlockSpec((tm, DIL_WIDTH), pos), pl.BlockSpec((tm, DIL_WIDTH), pos),
        pl.BlockSpec((tm, LANES), pos), pl.BlockSpec((tm, LANES), pos),
    ]
    return pl.pallas_call(
        _in_proj_kernel, grid=(B, ns), in_specs=in_specs, out_specs=out_specs, out_shape=out_shape,
        compiler_params=_cparams(("parallel", "parallel")), name="in_proj",
    )(x2d, lw['g_mix'], lw['w1'], lw['g_q'], lw['w_uq'], lw['g_kv'], lw['w_ukv'],
      tabs['cosd'], tabs['sind'], tabs['cosm'], tabs['sinm'])


LRU_CHUNK_ROWS = 1024
LRU_HALO = SUBLANES


def _lru_kernel(xf_ref, xfp_ref, xfn_ref, xb_ref, xbp_ref, xbn_ref, cw_ref, cb_ref, wg_ref, bg_ref,
                lam_ref, hf_ref, hb_ref, xpad, a_f, b_f, a_b, b_b, hs_f, hs_b, hcar, *, B, tc):
    i = pl.program_id(0)
    n = pl.num_programs(0)
    R = tc * B
    left = CONV_WIDTH // 2
    groups = _lane_groups(LRU_WIDTH)

    @pl.when(i == 0)
    def _():
        hcar[...] = jnp.zeros_like(hcar)

    def prep(x_ref, xp_ref, xn_ref, ci, d, a_s, b_s):
        for c, cs in enumerate(groups):
            for b in range(B):
                xpad[c, pl.ds(b, left, stride=B), :] = jnp.where(ci > 0, xp_ref[b, LRU_HALO - left:LRU_HALO, cs], 0.0)
                xpad[c, pl.ds(left * B + b, tc, stride=B), :] = x_ref[b, :, cs]
                xpad[c, pl.ds((left + tc) * B + b, 1), :] = jnp.where(ci < n - 1, xn_ref[b, 0:1, cs], 0.0)
        halves = []
        for c, cs in enumerate(groups):
            xc = cb_ref[:, cs] + xpad[c, 0:R, :] * cw_ref[0:1, cs]
            for kk in range(1, CONV_WIDTH):
                xc = xc + xpad[c, kk * B:kk * B + R, :] * cw_ref[kk:kk + 1, cs]
            halves.append(xc)
        xc = jnp.concatenate(halves, axis=-1)
        g = _dot(xc.astype(BF16), wg_ref[d]) + bg_ref[d]
        r = jax.nn.sigmoid(g[:, :LRU_WIDTH])
        ig = jax.nn.sigmoid(g[:, LRU_WIDTH:])
        lam = lam_ref[d]
        softplus = jnp.maximum(-lam, 0.0) + jnp.log(1.0 + jnp.exp(-jnp.abs(lam)))
        log_a = (-LRU_C) * r * softplus
        a_s[...] = jnp.exp(log_a)
        b_s[...] = jnp.sqrt(1.0 - jnp.exp(2.0 * log_a)) * (ig * xc)

    prep(xf_ref, xfp_ref, xfn_ref, i, 0, a_f, b_f)
    prep(xb_ref, xbp_ref, xbn_ref, n - 1 - i, 1, a_b, b_b)

    def body(s, carry):
        hf, hb = carry
        rf = pl.multiple_of(s * B, B)
        hf = a_f[pl.ds(rf, B), :] * hf + b_f[pl.ds(rf, B), :]
        rb = pl.multiple_of((tc - 1 - s) * B, B)
        hb = a_b[pl.ds(rb, B), :] * hb + b_b[pl.ds(rb, B), :]
        for c, cs in enumerate(groups):
            hs_f[c, pl.ds(rf, B), :] = hf[:, cs]
            hs_b[c, pl.ds(rb, B), :] = hb[:, cs]
        return hf, hb

    hf, hb = lax.fori_loop(0, tc, body, (hcar[0], hcar[1]), unroll=8)
    hcar[0] = hf
    hcar[1] = hb
    for c, cs in enumerate(groups):
        for b in range(B):
            hf_ref[b, :, cs] = hs_f[c, pl.ds(b, tc, stride=B), :]
            hb_ref[b, :, cs] = hs_b[c, pl.ds(b, tc, stride=B), :]


def _lru(xr, lw, B, S):
    tc = LRU_CHUNK_ROWS // B
    n = S // tc
    nh = S // LRU_HALO
    hb_per = tc // LRU_HALO
    x3 = xr.reshape(B, S, LRU_WIDTH)
    W = LRU_WIDTH

    main_f = pl.BlockSpec((B, tc, W), lambda i: (0, i, 0))
    prev_f = pl.BlockSpec((B, LRU_HALO, W), lambda i: (0, jnp.maximum(i * hb_per - 1, 0), 0))
    next_f = pl.BlockSpec((B, LRU_HALO, W), lambda i: (0, jnp.minimum((i + 1) * hb_per, nh - 1), 0))
    main_b = pl.BlockSpec((B, tc, W), lambda i: (0, n - 1 - i, 0))
    prev_b = pl.BlockSpec((B, LRU_HALO, W), lambda i: (0, jnp.maximum((n - 1 - i) * hb_per - 1, 0), 0))
    next_b = pl.BlockSpec((B, LRU_HALO, W), lambda i: (0, jnp.minimum((n - i) * hb_per, nh - 1), 0))

    def full(a):
        nd = a.ndim
        return pl.BlockSpec(a.shape, lambda i: (0,) * nd)

    R = tc * B
    ng = W // LANES
    hf, hb = pl.pallas_call(
        functools.partial(_lru_kernel, B=B, tc=tc),
        grid=(n,),
        in_specs=[main_f, prev_f, next_f, main_b, prev_b, next_b,
                  full(lw['conv_w']), full(lw['conv_b']), full(lw['rg_w']), full(lw['rg_b']), full(lw['rg_lam'])],
        out_specs=(main_f, main_b),
        out_shape=(jax.ShapeDtypeStruct((B, S, W), F32),) * 2,
        scratch_shapes=[pltpu.VMEM((ng, R + (CONV_WIDTH - 1) * B, LANES), F32)] + [pltpu.VMEM((R, W), F32)] * 4
        + [pltpu.VMEM((ng, R, LANES), F32)] * 2 + [pltpu.VMEM((2, B, W), F32)],
        compiler_params=_cparams(("arbitrary",)), name="rglru",
    )(x3, x3, x3, x3, x3, x3, lw['conv_w'], lw['conv_b'], lw['rg_w'], lw['rg_b'], lw['rg_lam'])
    return hf.reshape(B * S, W), hb.reshape(B * S, W)


BAND_HALF = 64
DIL_TILE = 1024
DIL_QBLOCK = 256


def _dil_kernel(q_ref, kp_ref, k_ref, kn_ref, vp_ref, v_ref, vn_ref, o_ref,
                qs, kcat, vcat, acc_m, acc_l, acc_o, *, S):
    TT = DIL_TILE
    tile_start = pl.program_id(1) * TT
    groups = _lane_groups(DIL_WIDTH)
    for c, cs in enumerate(groups):
        qs[c] = q_ref[0, :, cs].astype(F32)
        for j, (kr, vr) in enumerate(((kp_ref, vp_ref), (k_ref, v_ref), (kn_ref, vn_ref))):
            kcat[c, j * TT:(j + 1) * TT, :] = kr[0, :, cs].astype(F32)
            vcat[c, j * TT:(j + 1) * TT, :] = vr[0, :, cs].astype(F32)
    acc_m[...] = jnp.full(acc_m.shape, -1e30, F32)
    acc_l[...] = jnp.zeros(acc_l.shape, F32)
    acc_o[...] = jnp.zeros(acc_o.shape, F32)
    head = lax.broadcasted_iota(jnp.int32, (1, DIL_WIDTH), 1) // DIL_HEAD_DIM

    for _, dil in DIL_PATTERNS:
        nq = min(DIL_QBLOCK, TT // dil)
        nsub = TT // dil // nq
        nk = nq + 2 * BAND_HALF
        row = lax.broadcasted_iota(jnp.int32, (nq, nk), 0)
        col = lax.broadcasted_iota(jnp.int32, (nq, nk), 1)
        band = jnp.abs(col - BAND_HALF - row) <= BAND_HALF

        def rows(start, n, dil=dil):
            return pl.ds(pl.multiple_of(start, SUBLANES), n) if dil == 1 else pl.ds(start, n, stride=dil)

        def ld(ref, idx):
            return jnp.concatenate([ref[c, idx, :] for c in range(len(groups))], axis=-1)

        def st(ref, idx, val):
            for c, cs in enumerate(groups):
                ref[c, idx, :] = val[:, cs]

        def block(it, carry, dil=dil, nq=nq, nsub=nsub, nk=nk, band=band, col=col, rows=rows):
            r, j = (0, it) if dil == 1 else (it, 0)
            off = j * nq * dil + r
            qi = rows(off, nq)
            ki = rows(off + TT - BAND_HALF * dil, nk)
            kpos = tile_start + off - BAND_HALF * dil + col * dil
            valid = band & (kpos >= 0) & (kpos < S)
            q = ld(qs, qi).astype(BF16)
            k = ld(kcat, ki).astype(BF16)
            v = ld(vcat, ki).astype(BF16)
            m_b = jnp.zeros((nq, DIL_WIDTH), F32)
            l_b = jnp.zeros((nq, DIL_WIDTH), F32)
            o_b = jnp.zeros((nq, DIL_WIDTH), F32)
            per = DIL_HEADS if nq * DIL_HEADS <= DIL_QBLOCK else 1
            for h0 in range(0, DIL_HEADS, per):
                hs = range(h0, h0 + per)
                qh = jnp.concatenate([jnp.where(head == h, q, jnp.zeros_like(q)) for h in hs], axis=0)
                s = jnp.where(jnp.concatenate([valid] * per, axis=0), _dot_nt(qh, k), -1e30)
                m = jnp.max(s, axis=-1, keepdims=True)
                e = jnp.exp2(s - m)
                l = jnp.sum(e, axis=-1, keepdims=True)
                o = _dot(e.astype(BF16), v)
                for n_, h in enumerate(hs):
                    hm = head == h
                    sl = slice(n_ * nq, (n_ + 1) * nq)
                    m_b = jnp.where(hm, m[sl], m_b)
                    l_b = jnp.where(hm, l[sl], l_b)
                    o_b = jnp.where(hm, o[sl], o_b)
            m_old = ld(acc_m, qi)
            m_new = jnp.maximum(m_old, m_b)
            w_old = jnp.exp2(m_old - m_new)
            w_b = jnp.exp2(m_b - m_new)
            st(acc_l, qi, w_old * ld(acc_l, qi) + w_b * l_b)
            st(acc_o, qi, w_old * ld(acc_o, qi) + w_b * o_b)
            st(acc_m, qi, m_new)
            return carry

        lax.fori_loop(0, dil * nsub, block, 0)

    for c, cs in enumerate(groups):
        o_ref[0, :, cs] = (acc_o[c] / acc_l[c]).astype(BF16)


def _dilated(q, k, v, B, S):
    TT = DIL_TILE
    nt = S // TT
    W = DIL_WIDTH
    ng = W // LANES
    q3, k3, v3 = (a.reshape(B, S, W) for a in (q, k, v))
    own = pl.BlockSpec((1, TT, W), lambda b, i: (b, i, 0))
    prev = pl.BlockSpec((1, TT, W), lambda b, i: (b, jnp.maximum(i - 1, 0), 0))
    nxt = pl.BlockSpec((1, TT, W), lambda b, i: (b, jnp.minimum(i + 1, nt - 1), 0))
    o = pl.pallas_call(
        functools.partial(_dil_kernel, S=S), grid=(B, nt),
        in_specs=[own, prev, own, nxt, prev, own, nxt], out_specs=own,
        out_shape=jax.ShapeDtypeStruct((B, S, W), BF16),
        scratch_shapes=[pltpu.VMEM((ng, TT, LANES), F32), pltpu.VMEM((ng, 3 * TT, LANES), F32),
                        pltpu.VMEM((ng, 3 * TT, LANES), F32)] + [pltpu.VMEM((ng, TT, LANES), F32)] * 3,
        compiler_params=_cparams(("parallel", "parallel")), name="dilated_attn",
    )(q3, k3, k3, k3, v3, v3, v3)
    return o.reshape(B * S, W)


def _mla_kernel(q_ref, k_ref, v_ref, o_ref):
    lane = lax.broadcasted_iota(jnp.int32, (1, LANES), 1)
    for pair in range(MLA_HEADS // 2):
        vv = v_ref[0, :, pair * LANES:(pair + 1) * LANES]
        outs = []
        for sub in range(2):
            h = 2 * pair + sub
            lo, hi = h * MLA_HEAD_PAD, (h + 1) * MLA_HEAD_PAD
            s = _dot_nt(q_ref[0, :, lo:hi], k_ref[0, :, lo:hi])
            m = jnp.max(s, axis=-1, keepdims=True)
            e = jnp.exp2(s - m)
            l = jnp.sum(e, axis=-1, keepdims=True)
            outs.append(_dot(e.astype(BF16), vv) * (1.0 / l))
        o_ref[0, :, pair * LANES:(pair + 1) * LANES] = jnp.where(lane < MLA_V, outs[0], outs[1]).astype(BF16)


def _mla(qm, km, vm, B, S, tq):
    QW = MLA_HEADS * MLA_HEAD_PAD
    q3, k3, v3 = qm.reshape(B, S, QW), km.reshape(B, S, QW), vm.reshape(B, S, MLA_WIDTH)
    o = pl.pallas_call(
        _mla_kernel, grid=(B, S // tq),
        in_specs=[pl.BlockSpec((1, tq, QW), lambda b, i: (b, i, 0)),
                  pl.BlockSpec((1, S, QW), lambda b, i: (b, 0, 0)),
                  pl.BlockSpec((1, S, MLA_WIDTH), lambda b, i: (b, 0, 0))],
        out_specs=pl.BlockSpec((1, tq, MLA_WIDTH), lambda b, i: (b, i, 0)),
        out_shape=jax.ShapeDtypeStruct((B, S, MLA_WIDTH), BF16),
        compiler_params=_cparams(("parallel", "arbitrary")), name="mla_attn",
    )(q3, k3, v3)
    return o.reshape(B * S, MLA_WIDTH)


def _mix_out_kernel(*refs, moe):
    (x_ref, hf_ref, hb_ref, gy_ref, dil_ref, mo_ref, gl_ref, gd_ref, gm_ref, wout_ref, gffn_ref) = refs[:11]
    if moe:
        rw_ref, x1_ref, h2_ref, route_ref, cnt_ref, carry = refs[11:]
    else:
        x1_ref, h2_ref = refs[11:]
    lru = (hf_ref[...] + hb_ref[...]) * gy_ref[...].astype(F32)
    mix = jnp.concatenate([
        _rms(lru, gl_ref[...], LRU_WIDTH), _rms(dil_ref[...].astype(F32), gd_ref[...], DIL_WIDTH),
        _rms(mo_ref[...].astype(F32), gm_ref[...], MLA_WIDTH)], axis=-1).astype(BF16)
    x1 = x_ref[...] + _dot(mix, wout_ref[...])
    x1_ref[...] = x1
    h2 = _rms(x1, gffn_ref[...], D_MODEL)
    h2_ref[...] = h2.astype(h2_ref.dtype)
    if moe:
        first = (pl.program_id(0) == 0) & (pl.program_id(1) == 0)

        @pl.when(first)
        def _():
            carry[...] = jnp.zeros_like(carry)

        logits = jnp.dot(h2, rw_ref[...], preferred_element_type=F32, precision=lax.Precision.HIGHEST)
        tm = logits.shape[0]
        lane = lax.broadcasted_iota(jnp.int32, logits.shape, 1)
        neg = -jnp.inf
        lg = jnp.where(lane < N_EXPERTS, logits, neg)
        m1 = jnp.max(lg, axis=-1, keepdims=True)
        i1 = jnp.min(jnp.where(lg == m1, lane, LANES), axis=-1, keepdims=True)
        lg2 = jnp.where(lane == i1, neg, lg)
        m2 = jnp.max(lg2, axis=-1, keepdims=True)
        i2 = jnp.min(jnp.where(lg2 == m2, lane, LANES), axis=-1, keepdims=True)
        e2 = jnp.exp(m2 - m1)
        den = 1.0 + e2
        sel1, sel2 = lane == i1, lane == i2
        onehot = jnp.where(sel1 | sel2, 1.0, 0.0)
        tri = (lax.broadcasted_iota(jnp.int32, (tm, tm), 0) > lax.broadcasted_iota(jnp.int32, (tm, tm), 1))
        before = _dot(jnp.where(tri, 1.0, 0.0).astype(BF16), onehot.astype(BF16)) + carry[0:1, :]
        rank1 = jnp.sum(jnp.where(sel1, before, 0.0), axis=-1, keepdims=True)
        rank2 = jnp.sum(jnp.where(sel2, before, 0.0), axis=-1, keepdims=True)
        carry[...] = carry[...] + jnp.sum(onehot, axis=0, keepdims=True)
        cnt_ref[...] = carry[...]
        cols = (i1.astype(F32), i2.astype(F32), 1.0 / den, e2 / den, rank1, rank2)
        route = jnp.zeros(logits.shape, F32)
        for c, val in enumerate(cols):
            route = jnp.where(lane == c, val, route)
        route_ref[...] = route


def _mix_out(x2d, hf, hb, gy, dil, mo, lw, B, S, tm, moe):
    T = B * S
    ns = S // tm
    tok = lambda b, s: (b * ns + s, 0)
    const = lambda b, s: (0, 0)
    tokspec = lambda c: pl.BlockSpec((tm, c), tok)
    full = lambda a: pl.BlockSpec(a.shape, const)
    args = [x2d, hf, hb, gy, dil, mo, lw['g_l'], lw['g_d'], lw['g_m'], lw['w_out'], lw['g_ffn']]
    in_specs = [tokspec(D_MODEL), tokspec(LRU_WIDTH), tokspec(LRU_WIDTH), tokspec(LRU_WIDTH), tokspec(DIL_WIDTH),
                tokspec(MLA_WIDTH), full(lw['g_l']), full(lw['g_d']), full(lw['g_m']), full(lw['w_out']),
                full(lw['g_ffn'])]
    out_shape = [jax.ShapeDtypeStruct((T, D_MODEL), F32), jax.ShapeDtypeStruct((T, D_MODEL), F32 if moe else BF16)]
    out_specs = [tokspec(D_MODEL), tokspec(D_MODEL)]
    scratch = []
    if moe:
        args.append(lw['router_w'])
        in_specs.append(full(lw['router_w']))
        out_shape += [jax.ShapeDtypeStruct((T, LANES), F32), jax.ShapeDtypeStruct((SUBLANES, LANES), F32)]
        out_specs += [tokspec(LANES), pl.BlockSpec((SUBLANES, LANES), const)]
        scratch = [pltpu.VMEM((SUBLANES, LANES), F32)]
    sem = ("arbitrary", "arbitrary") if moe else ("parallel", "parallel")
    return pl.pallas_call(
        functools.partial(_mix_out_kernel, moe=moe), grid=(B, ns), in_specs=in_specs,
        out_specs=tuple(out_specs), out_shape=tuple(out_shape), scratch_shapes=scratch,
        compiler_params=_cparams(sem), name="mix_out",
    )(*args)


def _ffn_kernel(*refs, final):
    if final:
        h_ref, x1_ref, wg_ref, wu_ref, wd_ref, gfin_ref, o_ref = refs
    else:
        h_ref, x1_ref, wg_ref, wu_ref, wd_ref, o_ref = refs
    f = pl.program_id(1)

    @pl.when(f == 0)
    def _():
        o_ref[...] = x1_ref[...]

    h = h_ref[...]
    g = _dot(h, wg_ref[...])
    u = _dot(h, wu_ref[...])
    a = (g * jax.nn.sigmoid(g) * u).astype(BF16)
    o_ref[...] += _dot(a, wd_ref[...])

    if final:
        @pl.when(f == pl.num_programs(1) - 1)
        def _():
            o_ref[...] = _rms(o_ref[...], gfin_ref[...], D_MODEL)


def _ffn(h2, x1, wg, wu, wd, gfin, tm, tf):
    T = h2.shape[0]
    F = wg.shape[1]
    final = gfin is not None
    args = [h2, x1, wg, wu, wd] + ([gfin] if final else [])
    in_specs = [pl.BlockSpec((tm, D_MODEL), lambda i, f: (i, 0)), pl.BlockSpec((tm, D_MODEL), lambda i, f: (i, 0)),
                pl.BlockSpec((D_MODEL, tf), lambda i, f: (0, f)), pl.BlockSpec((D_MODEL, tf), lambda i, f: (0, f)),
                pl.BlockSpec((tf, D_MODEL), lambda i, f: (f, 0))]
    if final:
        in_specs.append(pl.BlockSpec(gfin.shape, lambda i, f: (0, 0)))
    return pl.pallas_call(
        functools.partial(_ffn_kernel, final=final), grid=(T // tm, F // tf), in_specs=in_specs,
        out_specs=pl.BlockSpec((tm, D_MODEL), lambda i, f: (i, 0)),
        out_shape=jax.ShapeDtypeStruct((T, D_MODEL), F32),
        compiler_params=_cparams(("parallel", "arbitrary")), name="ffn",
    )(*args)


MOE_BLOCK_ROWS = 512
DISPATCH_TOKENS = 1024
COMBINE_TOKENS = 512


def _dispatch_kernel(pos_ref, h_ref, xs_in, xs_out, sem, *, tmd):
    del xs_in

    def body(t, c):
        src = h_ref.at[pl.ds(t, 1), :]
        for k in range(TOP_K):
            pltpu.make_async_copy(src, xs_out.at[pl.ds(pos_ref[0, 0, TOP_K * t + k], 1), :], sem).start()
        return c

    lax.fori_loop(0, tmd, body, 0, unroll=4)
    for _ in range(TOP_K):
        pltpu.make_async_copy(h_ref, xs_out.at[pl.ds(0, tmd), :], sem).wait()


def _dispatch(h2, pos, n_rows):
    T = h2.shape[0]
    tmd = DISPATCH_TOKENS
    pos3 = pos.reshape(T // tmd, 1, TOP_K * tmd)
    xs0 = jnp.zeros((n_rows, D_MODEL), F32)
    return pl.pallas_call(
        functools.partial(_dispatch_kernel, tmd=tmd), grid=(T // tmd,),
        in_specs=[pl.BlockSpec((1, 1, TOP_K * tmd), lambda i: (i, 0, 0), memory_space=pltpu.SMEM),
                  pl.BlockSpec((tmd, D_MODEL), lambda i: (i, 0)), pl.BlockSpec(memory_space=pl.ANY)],
        out_specs=pl.BlockSpec(memory_space=pl.ANY),
        out_shape=jax.ShapeDtypeStruct((n_rows, D_MODEL), F32),
        scratch_shapes=[pltpu.SemaphoreType.DMA(())],
        input_output_aliases={2: 0},
        compiler_params=_cparams(("arbitrary",)), name="moe_dispatch",
    )(pos3, h2, xs0)


def _gmm_kernel(be_ref, xs_ref, wg_ref, wu_ref, wd_ref, y_ref, xb):
    del be_ref
    f = pl.program_id(1)

    @pl.when(f == 0)
    def _():
        xb[...] = xs_ref[...].astype(BF16)

    h = xb[...]
    g = _dot(h, wg_ref[0])
    u = _dot(h, wu_ref[0])
    part = _dot((g * jax.nn.sigmoid(g) * u).astype(BF16), wd_ref[0])

    @pl.when(f == 0)
    def _():
        y_ref[...] = part

    @pl.when(f > 0)
    def _():
        y_ref[...] += part


def _gmm(xs, block_expert, wg, wu, wd, tf):
    P = xs.shape[0]
    bm = MOE_BLOCK_ROWS
    F = wg.shape[2]
    grid_spec = pltpu.PrefetchScalarGridSpec(
        num_scalar_prefetch=1, grid=(P // bm, F // tf),
        in_specs=[pl.BlockSpec((bm, D_MODEL), lambda j, f, be: (j, 0)),
                  pl.BlockSpec((1, D_MODEL, tf), lambda j, f, be: (be[j], 0, f)),
                  pl.BlockSpec((1, D_MODEL, tf), lambda j, f, be: (be[j], 0, f)),
                  pl.BlockSpec((1, tf, D_MODEL), lambda j, f, be: (be[j], f, 0))],
        out_specs=pl.BlockSpec((bm, D_MODEL), lambda j, f, be: (j, 0)),
        scratch_shapes=[pltpu.VMEM((bm, D_MODEL), BF16)])
    return pl.pallas_call(
        _gmm_kernel, grid_spec=grid_spec, out_shape=jax.ShapeDtypeStruct((P, D_MODEL), F32),
        compiler_params=_cparams(("parallel", "arbitrary")), name="moe_gmm",
    )(block_expert, xs, wg, wu, wd)


def _combine_kernel(*refs, tmc, final):
    if final:
        pos_ref, route_ref, x1_ref, y_hbm, gfin_ref, o_ref, ybuf, sem = refs
    else:
        pos_ref, route_ref, x1_ref, y_hbm, o_ref, ybuf, sem = refs

    def body(t, c):
        for k in range(TOP_K):
            pltpu.make_async_copy(y_hbm.at[pl.ds(pos_ref[0, 0, TOP_K * t + k], 1), :],
                                  ybuf.at[k, pl.ds(t, 1), :], sem).start()
        return c

    lax.fori_loop(0, tmc, body, 0, unroll=4)
    for k in range(TOP_K):
        pltpu.make_async_copy(y_hbm.at[pl.ds(0, tmc), :], ybuf.at[k], sem).wait()
    route = route_ref[...]
    out = x1_ref[...] + route[:, 2:3] * ybuf[0] + route[:, 3:4] * ybuf[1]
    if final:
        out = _rms(out, gfin_ref[...], D_MODEL)
    o_ref[...] = out


def _combine(pos, route, x1, y, gfin):
    T = x1.shape[0]
    tmc = COMBINE_TOKENS
    final = gfin is not None
    pos3 = pos.reshape(T // tmc, 1, TOP_K * tmc)
    args = [pos3, route, x1, y] + ([gfin] if final else [])
    in_specs = [pl.BlockSpec((1, 1, TOP_K * tmc), lambda i: (i, 0, 0), memory_space=pltpu.SMEM),
                pl.BlockSpec((tmc, LANES), lambda i: (i, 0)), pl.BlockSpec((tmc, D_MODEL), lambda i: (i, 0)),
                pl.BlockSpec(memory_space=pl.ANY)]
    if final:
        in_specs.append(pl.BlockSpec(gfin.shape, lambda i: (0, 0)))
    return pl.pallas_call(
        functools.partial(_combine_kernel, tmc=tmc, final=final), grid=(T // tmc,), in_specs=in_specs,
        out_specs=pl.BlockSpec((tmc, D_MODEL), lambda i: (i, 0)),
        out_shape=jax.ShapeDtypeStruct((T, D_MODEL), F32),
        scratch_shapes=[pltpu.VMEM((TOP_K, tmc, D_MODEL), F32), pltpu.SemaphoreType.DMA(())],
        compiler_params=_cparams(("arbitrary",)), name="moe_combine",
    )(*args)


def _moe(h2, x1, route, counts, wg, wu, wd, gfin, tf):
    T = h2.shape[0]
    bm = MOE_BLOCK_ROWS
    n_blocks = TOP_K * T // bm + N_EXPERTS
    cnt = counts[0, :N_EXPERTS].astype(jnp.int32)
    padded = (cnt + bm - 1) // bm * bm
    ends = jnp.cumsum(padded)
    starts = ends - padded
    experts = route[:, 0:TOP_K].astype(jnp.int32)
    pos = starts[experts] + route[:, 4:4 + TOP_K].astype(jnp.int32)
    block_start = jnp.arange(n_blocks, dtype=jnp.int32) * bm
    block_expert = jnp.minimum(jnp.sum((block_start[:, None] >= ends[None, :]).astype(jnp.int32), axis=1),
                               N_EXPERTS - 1)
    xs = _dispatch(h2, pos, n_blocks * bm)
    y = _gmm(xs, block_expert, wg, wu, wd, tf)
    return _combine(pos, route, x1, y, gfin)


def _swap_halves(w, heads, dim):
    k = w.shape[0]
    return w.reshape(k, heads, 2, dim // 2)[:, :, ::-1, :].reshape(k, heads * dim)


def _pad_heads(w, heads, dim, at=0):
    k = w.shape[0]
    w = w.reshape(k, heads, dim)
    w = jnp.pad(w, ((0, 0), (0, 0), (at, MLA_HEAD_PAD - at - dim)))
    return w.reshape(k, heads * MLA_HEAD_PAD)


def _block_diag(w):
    nb, bs, _ = w.shape
    eye = jnp.eye(nb, dtype=w.dtype)
    return (eye[:, None, :, None] * w[:, :, None, :]).reshape(nb * bs, nb * bs)


def _rope_tables(S):
    def cs(dim):
        half = dim // 2
        freqs = jnp.power(jnp.float32(ROPE_THETA), -jnp.arange(half, dtype=F32) * 2.0 / dim)
        ang = jnp.arange(S, dtype=F32)[:, None] * freqs[None, :]
        c, s = jnp.cos(ang), jnp.sin(ang)
        return jnp.concatenate([c, c], -1), jnp.concatenate([-s, s], -1)

    cd, sd = cs(DIL_HEAD_DIM)
    cm, sm = cs(MLA_ROPE)
    ones = jnp.ones((S, MLA_NOPE), F32)
    pad = MLA_HEAD_PAD - MLA_NOPE - MLA_ROPE
    return dict(
        cosd=jnp.tile(cd, (1, DIL_HEADS)), sind=jnp.tile(sd, (1, DIL_HEADS)),
        cosm=jnp.concatenate([ones, cm, jnp.ones((S, pad), F32)], -1),
        sinm=jnp.concatenate([0.0 * ones, sm, jnp.zeros((S, pad), F32)], -1))


def _prep_layer(l, p):
    w_in = p['w_in'][l]
    offs = [0, 256, 512, 768, 1024, 1280, 1536, 1664, 1696]
    xr, yr, qd, kd, vd, cq, ckv, kr = [w_in[:, a:b] for a, b in zip(offs[:-1], offs[1:])]
    kr_pad = _pad_heads(kr, 1, MLA_ROPE, at=MLA_NOPE)
    krs_pad = _pad_heads(_swap_halves(kr, 1, MLA_ROPE), 1, MLA_ROPE, at=MLA_NOPE)
    w1 = jnp.concatenate([xr, yr, qd, _swap_halves(qd, DIL_HEADS, DIL_HEAD_DIM), kd,
                          _swap_halves(kd, DIL_HEADS, DIL_HEAD_DIM), vd, cq, ckv, kr_pad, krs_pad], axis=1)

    w_uq = p['mla_w_uq'][l].reshape(MLA_Q_RANK, MLA_HEADS, MLA_NOPE + MLA_ROPE)
    qn, qr = w_uq[:, :, :MLA_NOPE], w_uq[:, :, MLA_NOPE:]
    qr_sw = qr.reshape(MLA_Q_RANK, MLA_HEADS, 2, MLA_ROPE // 2)[:, :, ::-1, :].reshape(qr.shape)
    padq = lambda a, b: jnp.pad(jnp.concatenate([a, b], -1), ((0, 0), (0, 0), (0, MLA_HEAD_PAD - MLA_NOPE - MLA_ROPE))
                                ).reshape(MLA_Q_RANK, MLA_HEADS * MLA_HEAD_PAD)
    w_uq2 = jnp.concatenate([padq(qn, qr), padq(jnp.zeros_like(qn), qr_sw)], axis=1)

    w_ukv = p['mla_w_ukv'][l].reshape(MLA_KV_RANK, MLA_HEADS, MLA_NOPE + MLA_V)
    kn = _pad_heads(w_ukv[:, :, :MLA_NOPE].reshape(MLA_KV_RANK, -1), MLA_HEADS, MLA_NOPE)
    vm = w_ukv[:, :, MLA_NOPE:].reshape(MLA_KV_RANK, MLA_WIDTH)
    w_ukv2 = jnp.concatenate([kn, vm], axis=1)

    rg_w = jnp.stack([jnp.concatenate([_block_diag(p['rg_w_a'][l, d]), _block_diag(p['rg_w_i'][l, d])], axis=1)
                      for d in range(2)])
    rg_b = jnp.stack([jnp.concatenate([p['rg_b_a'][l, d], p['rg_b_i'][l, d]])[None, :] for d in range(2)])
    g_out = p['mix_out_norm_g'][l]
    row = lambda v: v.reshape(1, -1).astype(F32)
    return dict(
        g_mix=row(p['norm_mix_g'][l]), w1=w1.astype(BF16), g_q=row(p['mla_q_norm_g'][l]), w_uq=w_uq2.astype(BF16),
        g_kv=row(p['mla_kv_norm_g'][l]), w_ukv=w_ukv2.astype(BF16),
        conv_w=p['conv_w'][l].astype(F32), conv_b=row(p['conv_b'][l]), rg_w=rg_w.astype(BF16), rg_b=rg_b.astype(F32),
        rg_lam=p['rg_lambda'][l].reshape(2, 1, LRU_WIDTH).astype(F32),
        g_l=row(g_out[:LRU_WIDTH]), g_d=row(g_out[LRU_WIDTH:LRU_WIDTH + DIL_WIDTH]),
        g_m=row(g_out[LRU_WIDTH + DIL_WIDTH:]), w_out=p['w_out'][l].astype(BF16), g_ffn=row(p['norm_ffn_g'][l]))


def _mixers(x2d, lw, tabs, B, S, tm, moe):
    xr, gy, q, k, v, qm, km, vm = _in_proj(x2d, lw, tabs, B, S, tm)
    hf, hb = _lru(xr, lw, B, S)
    dil = _dilated(q, k, v, B, S)
    mo = _mla(qm, km, vm, B, S, 256)
    return _mix_out(x2d, hf, hb, gy, dil, mo, lw, B, S, tm, moe)


def _trunk(x, layers, p, depth):
    B, S, _ = x.shape
    tabs = _rope_tables(S)
    x2d = x.reshape(B * S, D_MODEL)
    tm = 512
    for l in range(depth):
        lw = layers[l]
        moe = l % 2 == 1
        last = l == depth - 1
        gfin = p['final_norm_g'].reshape(1, D_MODEL) if last else None
        j = l // 2
        if moe:
            lw = dict(lw, router_w=jnp.pad(p['router_w'][j], ((0, 0), (0, LANES - N_EXPERTS))))
            x1, h2, route, counts = _mixers(x2d, lw, tabs, B, S, tm, True)
            x2d = _moe(h2, x1, route, counts, lw['moe_wg'], lw['moe_wu'], lw['moe_wd'], gfin, 896)
        else:
            x1, h2 = _mixers(x2d, lw, tabs, B, S, tm, False)
            x2d = _ffn(h2, x1, lw['ffn_wg'], lw['ffn_wu'], lw['ffn_wd'], gfin, tm, 1408)
    return x2d.reshape(B, S, D_MODEL)


def kernel(x_prompt, x_sample, norm_mix_g, w_in, conv_w, conv_b, rg_w_a, rg_b_a, rg_w_i, rg_b_i, rg_lambda, mla_q_norm_g, mla_w_uq, mla_kv_norm_g, mla_w_ukv, mix_out_norm_g, w_out, norm_ffn_g, ffn_w_gate, ffn_w_up, ffn_w_down, router_w, moe_w_gate, moe_w_up, moe_w_down, final_norm_g):
    p = dict(norm_mix_g=norm_mix_g, w_in=w_in, conv_w=conv_w, conv_b=conv_b, rg_w_a=rg_w_a, rg_b_a=rg_b_a,
             rg_w_i=rg_w_i, rg_b_i=rg_b_i, rg_lambda=rg_lambda, mla_q_norm_g=mla_q_norm_g, mla_w_uq=mla_w_uq,
             mla_kv_norm_g=mla_kv_norm_g, mla_w_ukv=mla_w_ukv, mix_out_norm_g=mix_out_norm_g, w_out=w_out,
             norm_ffn_g=norm_ffn_g, router_w=router_w, final_norm_g=final_norm_g)
    depth = w_in.shape[0]
    layers = []
    for l in range(depth):
        lw = _prep_layer(l, p)
        j = l // 2
        if l % 2 == 1:
            lw.update(moe_wg=moe_w_gate[j].astype(BF16), moe_wu=moe_w_up[j].astype(BF16),
                      moe_wd=moe_w_down[j].astype(BF16))
        else:
            lw.update(ffn_wg=ffn_w_gate[j].astype(BF16), ffn_wu=ffn_w_up[j].astype(BF16),
                      ffn_wd=ffn_w_down[j].astype(BF16))
        layers.append(lw)
    return (_trunk(x_prompt, layers, p, depth), _trunk(x_sample, layers, p, depth))
```

```python
import functools
import math

import jax
import jax.numpy as jnp
from jax import lax
from jax.experimental import pallas as pl
from jax.experimental.pallas import tpu as pltpu

F32 = jnp.float32
BF16 = jnp.bfloat16

D_MODEL = 1024
NORM_EPS = 1e-6
ROPE_THETA = 10000.0
LRU_WIDTH = 256
LRU_BLOCKS = 4
CONV_WIDTH = 4
LRU_C = 8.0
DIL_HEADS = 4
DIL_HEAD_DIM = 64
DIL_WIDTH = DIL_HEADS * DIL_HEAD_DIM
DIL_PATTERNS = ((128, 1), (512, 4), (2048, 16))
MLA_HEADS = 8
MLA_NOPE = 64
MLA_ROPE = 32
MLA_V = 64
MLA_Q_RANK = 256
MLA_KV_RANK = 128
MLA_WIDTH = MLA_HEADS * MLA_V
N_EXPERTS = 8
TOP_K = 2

LANES = 128
SUBLANES = 8
MLA_HEAD_PAD = LANES
VMEM_LIMIT = 56 * 1024 * 1024
LOG2E = math.log2(math.e)

_C_XR, _C_YR, _C_Q, _C_K, _C_V, _C_CQ, _C_CKV, _C_KR, _C_END = (0, 256, 512, 768, 1024, 1280, 1536, 1664, 1792)


def _cparams(sem):
    return pltpu.CompilerParams(dimension_semantics=sem, vmem_limit_bytes=VMEM_LIMIT)


def _rms(x, g, n):
    ms = jnp.sum(x * x, axis=-1, keepdims=True) * (1.0 / n)
    return x * lax.rsqrt(ms + NORM_EPS) * g


def _gelu_tanh(x):
    return 0.5 * x * (1.0 + jnp.tanh(0.7978845608028654 * (x + 0.044715 * (x * x * x))))


def _dot(a, b):
    return jnp.dot(a, b, preferred_element_type=F32)


def _dot_nt(a, b):
    return lax.dot_general(a, b, (((1,), (1,)), ((), ())), preferred_element_type=F32)


def _lane_groups(width):
    return [slice(c * LANES, (c + 1) * LANES) for c in range(width // LANES)]


def _swap_rotary_halves(x, half, period, start):
    lane = lax.broadcasted_iota(jnp.int32, (1, LANES), 1) % period
    first = lane < start + half
    outs = []
    for cs in _lane_groups(x.shape[1]):
        xs = x[:, cs]
        outs.append(jnp.where(first, pltpu.roll(xs, LANES - half, 1), pltpu.roll(xs, half, 1)))
    return outs[0] if len(outs) == 1 else jnp.concatenate(outs, axis=-1)


def _in_proj_kernel(x_ref, g_ref, w1_ref, gq_ref, wuq_ref, gkv_ref, wukv_ref,
                    cosd_ref, sind_ref, cosm_ref, sinm_ref,
                    xr_ref, gy_ref, q_ref, k_ref, v_ref, qm_ref, km_ref, vm_ref):
    h = _rms(x_ref[...], g_ref[...], D_MODEL).astype(BF16)
    p = _dot(h, w1_ref[...])
    xr_ref[...] = p[:, _C_XR:_C_YR]
    gy_ref[...] = _gelu_tanh(p[:, _C_YR:_C_Q]).astype(BF16)
    cosd, sind = cosd_ref[...], sind_ref[...]
    half_d = DIL_HEAD_DIM // 2
    q, k = p[:, _C_Q:_C_K], p[:, _C_K:_C_V]
    q = q * cosd + _swap_rotary_halves(q, half_d, DIL_HEAD_DIM, 0) * sind
    q_ref[...] = (q * (DIL_HEAD_DIM ** -0.5 * LOG2E)).astype(BF16)
    k_ref[...] = (k * cosd + _swap_rotary_halves(k, half_d, DIL_HEAD_DIM, 0) * sind).astype(BF16)
    v_ref[...] = p[:, _C_V:_C_CQ].astype(BF16)

    cosm, sinm = cosm_ref[...], sinm_ref[...]
    half_m = MLA_ROPE // 2
    cqn = _rms(p[:, _C_CQ:_C_CKV], gq_ref[...], MLA_Q_RANK).astype(BF16)
    qq = _dot(cqn, wuq_ref[...])
    qq_sw = _swap_rotary_halves(qq, half_m, MLA_HEAD_PAD, MLA_NOPE)
    ckvn = _rms(p[:, _C_CKV:_C_KR], gkv_ref[...], MLA_KV_RANK).astype(BF16)
    kv = _dot(ckvn, wukv_ref[...])
    kr = p[:, _C_KR:_C_END]
    kr = kr * cosm + _swap_rotary_halves(kr, half_m, MLA_HEAD_PAD, MLA_NOPE) * sinm
    scale = (MLA_NOPE + MLA_ROPE) ** -0.5 * LOG2E
    nq = MLA_HEADS * MLA_HEAD_PAD
    for hh in range(MLA_HEADS):
        lo, hi = hh * MLA_HEAD_PAD, (hh + 1) * MLA_HEAD_PAD
        qh = qq[:, lo:hi] * cosm + qq_sw[:, lo:hi] * sinm
        qm_ref[:, lo:hi] = (qh * scale).astype(BF16)
        km_ref[:, lo:hi] = (kv[:, lo:hi] + kr).astype(BF16)
    vm_ref[...] = kv[:, nq:].astype(BF16)


def _in_proj(x2d, lw, tabs, B, S, tm):
    T = B * S
    ns = S // tm
    tok = lambda b, s: (b * ns + s, 0)
    pos = lambda b, s: (s, 0)
    const = lambda b, s: (0, 0)

    def full(a):
        return pl.BlockSpec(a.shape, const)

    def tokspec(c):
        return pl.BlockSpec((tm, c), tok)

    out_shape = (
        jax.ShapeDtypeStruct((T, LRU_WIDTH), F32),
        jax.ShapeDtypeStruct((T, LRU_WIDTH), BF16),
        jax.ShapeDtypeStruct((T, DIL_WIDTH), BF16),
        jax.ShapeDtypeStruct((T, DIL_WIDTH), BF16),
        jax.ShapeDtypeStruct((T, DIL_WIDTH), BF16),
        jax.ShapeDtypeStruct((T, MLA_HEADS * MLA_HEAD_PAD), BF16),
        jax.ShapeDtypeStruct((T, MLA_HEADS * MLA_HEAD_PAD), BF16),
        jax.ShapeDtypeStruct((T, MLA_WIDTH), BF16),
    )
    out_specs = (
        tokspec(LRU_WIDTH), tokspec(LRU_WIDTH), tokspec(DIL_WIDTH), tokspec(DIL_WIDTH), tokspec(DIL_WIDTH),
        tokspec(MLA_HEADS * MLA_HEAD_PAD), tokspec(MLA_HEADS * MLA_HEAD_PAD), tokspec(MLA_WIDTH),
    )
    in_specs = [
        tokspec(D_MODEL), full(lw['g_mix']), full(lw['w1']), full(lw['g_q']), full(lw['w_uq']),
        full(lw['g_kv']), full(lw['w_ukv']),
        pl.BlockSpec((tm, DIL_WIDTH), pos), pl.BlockSpec((tm, DIL_WIDTH), pos),
        pl.BlockSpec((tm, LANES), pos), pl.BlockSpec((tm, LANES), pos),
    ]
    return pl.pallas_call(
        _in_proj_kernel, grid=(B, ns), in_specs=in_specs, out_specs=out_specs, out_shape=out_shape,
        compiler_params=_cparams(("parallel", "parallel")), name="in_proj",
    )(x2d, lw['g_mix'], lw['w1'], lw['g_q'], lw['w_uq'], lw['g_kv'], lw['w_ukv'],
      tabs['cosd'], tabs['sind'], tabs['cosm'], tabs['sinm'])


LRU_CHUNK_ROWS = 1024
LRU_HALO = SUBLANES


def _lru_kernel(xf_ref, xfp_ref, xfn_ref, xb_ref, xbp_ref, xbn_ref, cw_ref, cb_ref, wg_ref, bg_ref,
                lam_ref, hf_ref, hb_ref, xpad, a_f, b_f, a_b, b_b, hs_f, hs_b, hcar, *, B, tc):
    i = pl.program_id(0)
    n = pl.num_programs(0)
    R = tc * B
    left = CONV_WIDTH // 2
    groups = _lane_groups(LRU_WIDTH)

    @pl.when(i == 0)
    def _():
        hcar[...] = jnp.zeros_like(hcar)

    def prep(x_ref, xp_ref, xn_ref, ci, d, a_s, b_s):
        for c, cs in enumerate(groups):
            for b in range(B):
                xpad[c, pl.ds(b, left, stride=B), :] = jnp.where(ci > 0, xp_ref[b, LRU_HALO - left:LRU_HALO, cs], 0.0)
                xpad[c, pl.ds(left * B + b, tc, stride=B), :] = x_ref[b, :, cs]
                xpad[c, pl.ds((left + tc) * B + b, 1), :] = jnp.where(ci < n - 1, xn_ref[b, 0:1, cs], 0.0)
        halves = []
        for c, cs in enumerate(groups):
            xc = cb_ref[:, cs] + xpad[c, 0:R, :] * cw_ref[0:1, cs]
            for kk in range(1, CONV_WIDTH):
                xc = xc + xpad[c, kk * B:kk * B + R, :] * cw_ref[kk:kk + 1, cs]
            halves.append(xc)
        xc = jnp.concatenate(halves, axis=-1)
        g = _dot(xc.astype(BF16), wg_ref[d]) + bg_ref[d]
        r = jax.nn.sigmoid(g[:, :LRU_WIDTH])
        ig = jax.nn.sigmoid(g[:, LRU_WIDTH:])
        lam = lam_ref[d]
        softplus = jnp.maximum(-lam, 0.0) + jnp.log(1.0 + jnp.exp(-jnp.abs(lam)))
        log_a = (-LRU_C) * r * softplus
        a_s[...] = jnp.exp(log_a)
        b_s[...] = jnp.sqrt(1.0 - jnp.exp(2.0 * log_a)) * (ig * xc)

    prep(xf_ref, xfp_ref, xfn_ref, i, 0, a_f, b_f)
    prep(xb_ref, xbp_ref, xbn_ref, n - 1 - i, 1, a_b, b_b)

    def body(s, carry):
        hf, hb = carry
        rf = pl.multiple_of(s * B, B)
        hf = a_f[pl.ds(rf, B), :] * hf + b_f[pl.ds(rf, B), :]
        rb = pl.multiple_of((tc - 1 - s) * B, B)
        hb = a_b[pl.ds(rb, B), :] * hb + b_b[pl.ds(rb, B), :]
        for c, cs in enumerate(groups):
            hs_f[c, pl.ds(rf, B), :] = hf[:, cs]
            hs_b[c, pl.ds(rb, B), :] = hb[:, cs]
        return hf, hb

    hf, hb = lax.fori_loop(0, tc, body, (hcar[0], hcar[1]), unroll=8)
    hcar[0] = hf
    hcar[1] = hb
    for c, cs in enumerate(groups):
        for b in range(B):
            hf_ref[b, :, cs] = hs_f[c, pl.ds(b, tc, stride=B), :]
            hb_ref[b, :, cs] = hs_b[c, pl.ds(b, tc, stride=B), :]


def _lru(xr, lw, B, S):
    tc = LRU_CHUNK_ROWS // B
    n = S // tc
    nh = S // LRU_HALO
    hb_per = tc // LRU_HALO
    x3 = xr.reshape(B, S, LRU_WIDTH)
    W = LRU_WIDTH

    main_f = pl.BlockSpec((B, tc, W), lambda i: (0, i, 0))
    prev_f = pl.BlockSpec((B, LRU_HALO, W), lambda i: (0, jnp.maximum(i * hb_per - 1, 0), 0))
    next_f = pl.BlockSpec((B, LRU_HALO, W), lambda i: (0, jnp.minimum((i + 1) * hb_per, nh - 1), 0))
    main_b = pl.BlockSpec((B, tc, W), lambda i: (0, n - 1 - i, 0))
    prev_b = pl.BlockSpec((B, LRU_HALO, W), lambda i: (0, jnp.maximum((n - 1 - i) * hb_per - 1, 0), 0))
    next_b = pl.BlockSpec((B, LRU_HALO, W), lambda i: (0, jnp.minimum((n - i) * hb_per, nh - 1), 0))

    def full(a):
        nd = a.ndim
        return pl.BlockSpec(a.shape, lambda i: (0,) * nd)

    R = tc * B
    ng = W // LANES
    hf, hb = pl.pallas_call(
        functools.partial(_lru_kernel, B=B, tc=tc),
        grid=(n,),
        in_specs=[main_f, prev_f, next_f, main_b, prev_b, next_b,
                  full(lw['conv_w']), full(lw['conv_b']), full(lw['rg_w']), full(lw['rg_b']), full(lw['rg_lam'])],
        out_specs=(main_f, main_b),
        out_shape=(jax.ShapeDtypeStruct((B, S, W), F32),) * 2,
        scratch_shapes=[pltpu.VMEM((ng, R + (CONV_WIDTH - 1) * B, LANES), F32)] + [pltpu.VMEM((R, W), F32)] * 4
        + [pltpu.VMEM((ng, R, LANES), F32)] * 2 + [pltpu.VMEM((2, B, W), F32)],
        compiler_params=_cparams(("arbitrary",)), name="rglru",
    )(x3, x3, x3, x3, x3, x3, lw['conv_w'], lw['conv_b'], lw['rg_w'], lw['rg_b'], lw['rg_lam'])
    return hf.reshape(B * S, W), hb.reshape(B * S, W)


BAND_HALF = 64
DIL_TILE = 1024
DIL_QBLOCK = 256


def _dil_kernel(q_ref, kp_ref, k_ref, kn_ref, vp_ref, v_ref, vn_ref, o_ref,
                qs, kcat, vcat, acc_m, acc_l, acc_o, *, S):
    TT = DIL_TILE
    tile_start = pl.program_id(1) * TT
    groups = _lane_groups(DIL_WIDTH)
    for c, cs in enumerate(groups):
        qs[c] = q_ref[0, :, cs].astype(F32)
        for j, (kr, vr) in enumerate(((kp_ref, vp_ref), (k_ref, v_ref), (kn_ref, vn_ref))):
            kcat[c, j * TT:(j + 1) * TT, :] = kr[0, :, cs].astype(F32)
            vcat[c, j * TT:(j + 1) * TT, :] = vr[0, :, cs].astype(F32)
    head = lax.broadcasted_iota(jnp.int32, (1, DIL_WIDTH), 1) // DIL_HEAD_DIM

    order = sorted((dil for _, dil in DIL_PATTERNS), reverse=True)
    for dil in order:
        first = dil == order[0]
        nq = min(DIL_QBLOCK, TT // dil)
        nsub = TT // dil // nq
        nk = nq + 2 * BAND_HALF
        row = lax.broadcasted_iota(jnp.int32, (nq, nk), 0)
        col = lax.broadcasted_iota(jnp.int32, (nq, nk), 1)
        band = jnp.abs(col - BAND_HALF - row) <= BAND_HALF

        def rows(start, n, dil=dil):
            return pl.ds(pl.multiple_of(start, SUBLANES), n) if dil == 1 else pl.ds(start, n, stride=dil)

        def ld(ref, idx):
            return jnp.concatenate([ref[c, idx, :] for c in range(len(groups))], axis=-1)

        def st(ref, idx, val):
            for c, cs in enumerate(groups):
                ref[c, idx, :] = val[:, cs]

        def block(it, carry, dil=dil, nq=nq, nsub=nsub, nk=nk, band=band, col=col, rows=rows, first=first):
            r, j = (0, it) if dil == 1 else (it, 0)
            off = j * nq * dil + r
            qi = rows(off, nq)
            ki = rows(off + TT - BAND_HALF * dil, nk)
            kpos = tile_start + off - BAND_HALF * dil + col * dil
            valid = band & (kpos >= 0) & (kpos < S)
            q = ld(qs, qi).astype(BF16)
            k = ld(kcat, ki).astype(BF16)
            v = ld(vcat, ki).astype(BF16)
            m_b = jnp.zeros((nq, DIL_WIDTH), F32)
            l_b = jnp.zeros((nq, DIL_WIDTH), F32)
            o_b = jnp.zeros((nq, DIL_WIDTH), F32)
            per = DIL_HEADS if nq * DIL_HEADS <= DIL_QBLOCK else 1
            groups_h = [range(h0, h0 + per) for h0 in range(0, DIL_HEADS, per)]
            valid_g = jnp.concatenate([valid] * per, axis=0)
            scores = [_dot_nt(jnp.concatenate([jnp.where(head == h, q, jnp.zeros_like(q)) for h in hs], axis=0), k)
                      for hs in groups_h]
            stats = []
            for s in scores:
                s = jnp.where(valid_g, s, -1e30)
                m = jnp.max(s, axis=-1, keepdims=True)
                e = jnp.exp2(s - m)
                stats.append((m, jnp.sum(e, axis=-1, keepdims=True), e.astype(BF16)))
            for hs, (m, l, e) in zip(groups_h, stats):
                o = _dot(e, v)
                for n_, h in enumerate(hs):
                    hm = head == h
                    sl = slice(n_ * nq, (n_ + 1) * nq)
                    m_b = jnp.where(hm, m[sl], m_b)
                    l_b = jnp.where(hm, l[sl], l_b)
                    o_b = jnp.where(hm, o[sl], o_b)
            if first:
                st(acc_l, qi, l_b)
                st(acc_o, qi, o_b)
                st(acc_m, qi, m_b)
                return carry
            m_old = ld(acc_m, qi)
            m_new = jnp.maximum(m_old, m_b)
            w_old = jnp.exp2(m_old - m_new)
            w_b = jnp.exp2(m_b - m_new)
            st(acc_l, qi, w_old * ld(acc_l, qi) + w_b * l_b)
            st(acc_o, qi, w_old * ld(acc_o, qi) + w_b * o_b)
            st(acc_m, qi, m_new)
            return carry

        lax.fori_loop(0, dil * nsub, block, 0, unroll=2 if nq < DIL_QBLOCK else 1)

    for c, cs in enumerate(groups):
        o_ref[0, :, cs] = (acc_o[c] / acc_l[c]).astype(BF16)


def _dilated(q, k, v, B, S):
    TT = DIL_TILE
    nt = S // TT
    W = DIL_WIDTH
    ng = W // LANES
    q3, k3, v3 = (a.reshape(B, S, W) for a in (q, k, v))
    own = pl.BlockSpec((1, TT, W), lambda b, i: (b, i, 0))
    prev = pl.BlockSpec((1, TT, W), lambda b, i: (b, jnp.maximum(i - 1, 0), 0))
    nxt = pl.BlockSpec((1, TT, W), lambda b, i: (b, jnp.minimum(i + 1, nt - 1), 0))
    o = pl.pallas_call(
        functools.partial(_dil_kernel, S=S), grid=(B, nt),
        in_specs=[own, prev, own, nxt, prev, own, nxt], out_specs=own,
        out_shape=jax.ShapeDtypeStruct((B, S, W), BF16),
        scratch_shapes=[pltpu.VMEM((ng, TT, LANES), F32), pltpu.VMEM((ng, 3 * TT, LANES), F32),
                        pltpu.VMEM((ng, 3 * TT, LANES), F32)] + [pltpu.VMEM((ng, TT, LANES), F32)] * 3,
        compiler_params=_cparams(("parallel", "parallel")), name="dilated_attn",
    )(q3, k3, k3, k3, v3, v3, v3)
    return o.reshape(B * S, W)


def _mla_kernel(q_ref, k_ref, v_ref, o_ref):
    lane = lax.broadcasted_iota(jnp.int32, (1, LANES), 1)

    def scores(h):
        lo, hi = h * MLA_HEAD_PAD, (h + 1) * MLA_HEAD_PAD
        return _dot_nt(q_ref[0, :, lo:hi], k_ref[0, :, lo:hi])

    s_next = scores(0)
    outs = []
    for h in range(MLA_HEADS):
        s = s_next
        if h + 1 < MLA_HEADS:
            s_next = scores(h + 1)
        m = jnp.max(s, axis=-1, keepdims=True)
        e = jnp.exp2(s - m)
        l = jnp.sum(e, axis=-1, keepdims=True)
        pair = h // 2
        vv = v_ref[0, :, pair * LANES:(pair + 1) * LANES]
        outs.append(_dot(e.astype(BF16), vv) * (1.0 / l))
        if h % 2 == 1:
            o_ref[0, :, pair * LANES:(pair + 1) * LANES] = jnp.where(lane < MLA_V, outs[h - 1], outs[h]).astype(BF16)


def _mla(qm, km, vm, B, S, tq):
    QW = MLA_HEADS * MLA_HEAD_PAD
    q3, k3, v3 = qm.reshape(B, S, QW), km.reshape(B, S, QW), vm.reshape(B, S, MLA_WIDTH)
    o = pl.pallas_call(
        _mla_kernel, grid=(B, S // tq),
        in_specs=[pl.BlockSpec((1, tq, QW), lambda b, i: (b, i, 0)),
                  pl.BlockSpec((1, S, QW), lambda b, i: (b, 0, 0)),
                  pl.BlockSpec((1, S, MLA_WIDTH), lambda b, i: (b, 0, 0))],
        out_specs=pl.BlockSpec((1, tq, MLA_WIDTH), lambda b, i: (b, i, 0)),
        out_shape=jax.ShapeDtypeStruct((B, S, MLA_WIDTH), BF16),
        compiler_params=_cparams(("parallel", "arbitrary")), name="mla_attn",
    )(q3, k3, v3)
    return o.reshape(B * S, MLA_WIDTH)


def _mix_out_kernel(*refs, moe):
    (x_ref, hf_ref, hb_ref, gy_ref, dil_ref, mo_ref, gl_ref, gd_ref, gm_ref, wout_ref, gffn_ref) = refs[:11]
    if moe:
        rw_ref, x1_ref, h2_ref, route_ref, cnt_ref, carry = refs[11:]
    else:
        x1_ref, h2_ref = refs[11:]
    lru = (hf_ref[...] + hb_ref[...]) * gy_ref[...].astype(F32)
    mix = jnp.concatenate([
        _rms(lru, gl_ref[...], LRU_WIDTH), _rms(dil_ref[...].astype(F32), gd_ref[...], DIL_WIDTH),
        _rms(mo_ref[...].astype(F32), gm_ref[...], MLA_WIDTH)], axis=-1).astype(BF16)
    x1 = x_ref[...] + _dot(mix, wout_ref[...])
    x1_ref[...] = x1
    h2 = _rms(x1, gffn_ref[...], D_MODEL)
    h2_ref[...] = h2.astype(h2_ref.dtype)
    if moe:
        first = (pl.program_id(0) == 0) & (pl.program_id(1) == 0)

        @pl.when(first)
        def _():
            carry[...] = jnp.zeros_like(carry)

        h_hi = h2.astype(BF16)
        h_lo = (h2 - h_hi.astype(F32)).astype(BF16)
        logits = _dot(h_hi, rw_ref[0]) + (_dot(h_hi, rw_ref[1]) + _dot(h_lo, rw_ref[0]))
        tm = logits.shape[0]
        lane = lax.broadcasted_iota(jnp.int32, logits.shape, 1)
        neg = -jnp.inf
        lg = jnp.where(lane < N_EXPERTS, logits, neg)
        m1 = jnp.max(lg, axis=-1, keepdims=True)
        i1 = jnp.min(jnp.where(lg == m1, lane, LANES), axis=-1, keepdims=True)
        lg2 = jnp.where(lane == i1, neg, lg)
        m2 = jnp.max(lg2, axis=-1, keepdims=True)
        i2 = jnp.min(jnp.where(lg2 == m2, lane, LANES), axis=-1, keepdims=True)
        e2 = jnp.exp(m2 - m1)
        den = 1.0 + e2
        sel1, sel2 = lane == i1, lane == i2
        onehot = jnp.where(sel1 | sel2, 1.0, 0.0)
        tri = (lax.broadcasted_iota(jnp.int32, (tm, tm), 0) > lax.broadcasted_iota(jnp.int32, (tm, tm), 1))
        before = _dot(jnp.where(tri, 1.0, 0.0).astype(BF16), onehot.astype(BF16)) + carry[0:1, :]
        rank1 = jnp.sum(jnp.where(sel1, before, 0.0), axis=-1, keepdims=True)
        rank2 = jnp.sum(jnp.where(sel2, before, 0.0), axis=-1, keepdims=True)
        carry[...] = carry[...] + jnp.sum(onehot, axis=0, keepdims=True)
        cnt_ref[...] = carry[...]
        cols = (i1.astype(F32), i2.astype(F32), 1.0 / den, e2 / den, rank1, rank2)
        route = jnp.zeros(logits.shape, F32)
        for c, val in enumerate(cols):
            route = jnp.where(lane == c, val, route)
        route_ref[...] = route


def _mix_out(x2d, hf, hb, gy, dil, mo, lw, B, S, tm, moe):
    T = B * S
    ns = S // tm
    tok = lambda b, s: (b * ns + s, 0)
    const = lambda b, s: (0, 0)
    tokspec = lambda c: pl.BlockSpec((tm, c), tok)
    full = lambda a: pl.BlockSpec(a.shape, const)
    args = [x2d, hf, hb, gy, dil, mo, lw['g_l'], lw['g_d'], lw['g_m'], lw['w_out'], lw['g_ffn']]
    in_specs = [tokspec(D_MODEL), tokspec(LRU_WIDTH), tokspec(LRU_WIDTH), tokspec(LRU_WIDTH), tokspec(DIL_WIDTH),
                tokspec(MLA_WIDTH), full(lw['g_l']), full(lw['g_d']), full(lw['g_m']), full(lw['w_out']),
                full(lw['g_ffn'])]
    out_shape = [jax.ShapeDtypeStruct((T, D_MODEL), F32), jax.ShapeDtypeStruct((T, D_MODEL), F32 if moe else BF16)]
    out_specs = [tokspec(D_MODEL), tokspec(D_MODEL)]
    scratch = []
    if moe:
        args.append(lw['router_w'])
        in_specs.append(pl.BlockSpec(lw['router_w'].shape, lambda b, s: (0, 0, 0)))
        out_shape += [jax.ShapeDtypeStruct((T, LANES), F32), jax.ShapeDtypeStruct((SUBLANES, LANES), F32)]
        out_specs += [tokspec(LANES), pl.BlockSpec((SUBLANES, LANES), const)]
        scratch = [pltpu.VMEM((SUBLANES, LANES), F32)]
    sem = ("arbitrary", "arbitrary") if moe else ("parallel", "parallel")
    return pl.pallas_call(
        functools.partial(_mix_out_kernel, moe=moe), grid=(B, ns), in_specs=in_specs,
        out_specs=tuple(out_specs), out_shape=tuple(out_shape), scratch_shapes=scratch,
        compiler_params=_cparams(sem), name="mix_out",
    )(*args)


def _ffn_kernel(*refs, final):
    if final:
        h_ref, x1_ref, wg_ref, wu_ref, wd_ref, gfin_ref, o_ref = refs
    else:
        h_ref, x1_ref, wg_ref, wu_ref, wd_ref, o_ref = refs
    f = pl.program_id(1)

    @pl.when(f == 0)
    def _():
        o_ref[...] = x1_ref[...]

    h = h_ref[...]
    g = _dot(h, wg_ref[...])
    u = _dot(h, wu_ref[...])
    a = (g * jax.nn.sigmoid(g) * u).astype(BF16)
    o_ref[...] += _dot(a, wd_ref[...])

    if final:
        @pl.when(f == pl.num_programs(1) - 1)
        def _():
            o_ref[...] = _rms(o_ref[...], gfin_ref[...], D_MODEL)


def _ffn(h2, x1, wg, wu, wd, gfin, tm, tf):
    T = h2.shape[0]
    F = wg.shape[1]
    final = gfin is not None
    args = [h2, x1, wg, wu, wd] + ([gfin] if final else [])
    in_specs = [pl.BlockSpec((tm, D_MODEL), lambda i, f: (i, 0)), pl.BlockSpec((tm, D_MODEL), lambda i, f: (i, 0)),
                pl.BlockSpec((D_MODEL, tf), lambda i, f: (0, f)), pl.BlockSpec((D_MODEL, tf), lambda i, f: (0, f)),
                pl.BlockSpec((tf, D_MODEL), lambda i, f: (f, 0))]
    if final:
        in_specs.append(pl.BlockSpec(gfin.shape, lambda i, f: (0, 0)))
    return pl.pallas_call(
        functools.partial(_ffn_kernel, final=final), grid=(T // tm, F // tf), in_specs=in_specs,
        out_specs=pl.BlockSpec((tm, D_MODEL), lambda i, f: (i, 0)),
        out_shape=jax.ShapeDtypeStruct((T, D_MODEL), F32),
        compiler_params=_cparams(("parallel", "arbitrary")), name="ffn",
    )(*args)


MOE_BLOCK_ROWS = 512
DISPATCH_TOKENS = 1024
COMBINE_TOKENS = 512


def _dispatch_kernel(pos_ref, h_ref, xs_in, xs_out, sem, *, tmd):
    del xs_in

    def body(t, c):
        src = h_ref.at[pl.ds(t, 1), :]
        for k in range(TOP_K):
            pltpu.make_async_copy(src, xs_out.at[pl.ds(pos_ref[0, 0, TOP_K * t + k], 1), :], sem).start()
        return c

    lax.fori_loop(0, tmd, body, 0, unroll=4)
    for _ in range(TOP_K):
        pltpu.make_async_copy(h_ref, xs_out.at[pl.ds(0, tmd), :], sem).wait()


def _dispatch(h2, pos, n_rows):
    T = h2.shape[0]
    tmd = DISPATCH_TOKENS
    pos3 = pos.reshape(T // tmd, 1, TOP_K * tmd)
    xs0 = jnp.zeros((n_rows, D_MODEL), F32)
    return pl.pallas_call(
        functools.partial(_dispatch_kernel, tmd=tmd), grid=(T // tmd,),
        in_specs=[pl.BlockSpec((1, 1, TOP_K * tmd), lambda i: (i, 0, 0), memory_space=pltpu.SMEM),
                  pl.BlockSpec((tmd, D_MODEL), lambda i: (i, 0)), pl.BlockSpec(memory_space=pl.ANY)],
        out_specs=pl.BlockSpec(memory_space=pl.ANY),
        out_shape=jax.ShapeDtypeStruct((n_rows, D_MODEL), F32),
        scratch_shapes=[pltpu.SemaphoreType.DMA(())],
        input_output_aliases={2: 0},
        compiler_params=_cparams(("arbitrary",)), name="moe_dispatch",
    )(pos3, h2, xs0)


def _gmm_kernel(be_ref, xs_ref, wg_ref, wu_ref, wd_ref, y_ref, xb):
    del be_ref
    f = pl.program_id(1)

    @pl.when(f == 0)
    def _():
        xb[...] = xs_ref[...].astype(BF16)

    h = xb[...]
    g = _dot(h, wg_ref[0])
    u = _dot(h, wu_ref[0])
    part = _dot((g * jax.nn.sigmoid(g) * u).astype(BF16), wd_ref[0])

    @pl.when(f == 0)
    def _():
        y_ref[...] = part

    @pl.when(f > 0)
    def _():
        y_ref[...] += part


def _gmm(xs, block_expert, wg, wu, wd, tf):
    P = xs.shape[0]
    bm = MOE_BLOCK_ROWS
    F = wg.shape[2]
    grid_spec = pltpu.PrefetchScalarGridSpec(
        num_scalar_prefetch=1, grid=(P // bm, F // tf),
        in_specs=[pl.BlockSpec((bm, D_MODEL), lambda j, f, be: (j, 0)),
                  pl.BlockSpec((1, D_MODEL, tf), lambda j, f, be: (be[j], 0, f)),
                  pl.BlockSpec((1, D_MODEL, tf), lambda j, f, be: (be[j], 0, f)),
                  pl.BlockSpec((1, tf, D_MODEL), lambda j, f, be: (be[j], f, 0))],
        out_specs=pl.BlockSpec((bm, D_MODEL), lambda j, f, be: (j, 0)),
        scratch_shapes=[pltpu.VMEM((bm, D_MODEL), BF16)])
    return pl.pallas_call(
        _gmm_kernel, grid_spec=grid_spec, out_shape=jax.ShapeDtypeStruct((P, D_MODEL), F32),
        compiler_params=_cparams(("parallel", "arbitrary")), name="moe_gmm",
    )(block_expert, xs, wg, wu, wd)


def _combine_kernel(*refs, tmc, final):
    if final:
        pos_ref, route_ref, x1_ref, y_hbm, gfin_ref, o_ref, ybuf, sem = refs
    else:
        pos_ref, route_ref, x1_ref, y_hbm, o_ref, ybuf, sem = refs

    def body(t, c):
        for k in range(TOP_K):
            pltpu.make_async_copy(y_hbm.at[pl.ds(pos_ref[0, 0, TOP_K * t + k], 1), :],
                                  ybuf.at[k, pl.ds(t, 1), :], sem).start()
        return c

    lax.fori_loop(0, tmc, body, 0, unroll=4)
    for k in range(TOP_K):
        pltpu.make_async_copy(y_hbm.at[pl.ds(0, tmc), :], ybuf.at[k], sem).wait()
    route = route_ref[...]
    out = x1_ref[...] + route[:, 2:3] * ybuf[0] + route[:, 3:4] * ybuf[1]
    if final:
        out = _rms(out, gfin_ref[...], D_MODEL)
    o_ref[...] = out


def _combine(pos, route, x1, y, gfin):
    T = x1.shape[0]
    tmc = COMBINE_TOKENS
    final = gfin is not None
    pos3 = pos.reshape(T // tmc, 1, TOP_K * tmc)
    args = [pos3, route, x1, y] + ([gfin] if final else [])
    in_specs = [pl.BlockSpec((1, 1, TOP_K * tmc), lambda i: (i, 0, 0), memory_space=pltpu.SMEM),
                pl.BlockSpec((tmc, LANES), lambda i: (i, 0)), pl.BlockSpec((tmc, D_MODEL), lambda i: (i, 0)),
                pl.BlockSpec(memory_space=pl.ANY)]
    if final:
        in_specs.append(pl.BlockSpec(gfin.shape, lambda i: (0, 0)))
    return pl.pallas_call(
        functools.partial(_combine_kernel, tmc=tmc, final=final), grid=(T // tmc,), in_specs=in_specs,
        out_specs=pl.BlockSpec((tmc, D_MODEL), lambda i: (i, 0)),
        out_shape=jax.ShapeDtypeStruct((T, D_MODEL), F32),
        scratch_shapes=[pltpu.VMEM((TOP_K, tmc, D_MODEL), F32), pltpu.SemaphoreType.DMA(())],
        compiler_params=_cparams(("arbitrary",)), name="moe_combine",
    )(*args)


def _moe(h2, x1, route, counts, wg, wu, wd, gfin, tf):
    T = h2.shape[0]
    bm = MOE_BLOCK_ROWS
    n_blocks = TOP_K * T // bm + N_EXPERTS
    cnt = counts[0, :N_EXPERTS].astype(jnp.int32)
    padded = (cnt + bm - 1) // bm * bm
    ends = jnp.cumsum(padded)
    starts = ends - padded
    experts = route[:, 0:TOP_K].astype(jnp.int32)
    pos = starts[experts] + route[:, 4:4 + TOP_K].astype(jnp.int32)
    block_start = jnp.arange(n_blocks, dtype=jnp.int32) * bm
    block_expert = jnp.minimum(jnp.sum((block_start[:, None] >= ends[None, :]).astype(jnp.int32), axis=1),
                               N_EXPERTS - 1)
    xs = _dispatch(h2, pos, n_blocks * bm)
    y = _gmm(xs, block_expert, wg, wu, wd, tf)
    return _combine(pos, route, x1, y, gfin)


def _pad_heads(w, heads, dim, at=0):
    k = w.shape[0]
    w = w.reshape(k, heads, dim)
    w = jnp.pad(w, ((0, 0), (0, 0), (at, MLA_HEAD_PAD - at - dim)))
    return w.reshape(k, heads * MLA_HEAD_PAD)


def _block_diag(w):
    nb, bs, _ = w.shape
    eye = jnp.eye(nb, dtype=w.dtype)
    return (eye[:, None, :, None] * w[:, :, None, :]).reshape(nb * bs, nb * bs)


def _rope_tables(S):
    def cs(dim):
        half = dim // 2
        freqs = jnp.power(jnp.float32(ROPE_THETA), -jnp.arange(half, dtype=F32) * 2.0 / dim)
        ang = jnp.arange(S, dtype=F32)[:, None] * freqs[None, :]
        c, s = jnp.cos(ang), jnp.sin(ang)
        return jnp.concatenate([c, c], -1), jnp.concatenate([-s, s], -1)

    cd, sd = cs(DIL_HEAD_DIM)
    cm, sm = cs(MLA_ROPE)
    ones = jnp.ones((S, MLA_NOPE), F32)
    pad = MLA_HEAD_PAD - MLA_NOPE - MLA_ROPE
    return dict(
        cosd=jnp.tile(cd, (1, DIL_HEADS)), sind=jnp.tile(sd, (1, DIL_HEADS)),
        cosm=jnp.concatenate([ones, cm, jnp.ones((S, pad), F32)], -1),
        sinm=jnp.concatenate([0.0 * ones, sm, jnp.zeros((S, pad), F32)], -1))


def _prep_layer(l, p):
    w_in = p['w_in'][l]
    offs = [0, 256, 512, 768, 1024, 1280, 1536, 1664, 1696]
    xr, yr, qd, kd, vd, cq, ckv, kr = [w_in[:, a:b] for a, b in zip(offs[:-1], offs[1:])]
    kr_pad = _pad_heads(kr, 1, MLA_ROPE, at=MLA_NOPE)
    w1 = jnp.concatenate([xr, yr, qd, kd, vd, cq, ckv, kr_pad], axis=1)
    w_uq2 = _pad_heads(p['mla_w_uq'][l], MLA_HEADS, MLA_NOPE + MLA_ROPE)

    w_ukv = p['mla_w_ukv'][l].reshape(MLA_KV_RANK, MLA_HEADS, MLA_NOPE + MLA_V)
    kn = _pad_heads(w_ukv[:, :, :MLA_NOPE].reshape(MLA_KV_RANK, -1), MLA_HEADS, MLA_NOPE)
    vm = w_ukv[:, :, MLA_NOPE:].reshape(MLA_KV_RANK, MLA_WIDTH)
    w_ukv2 = jnp.concatenate([kn, vm], axis=1)

    rg_w = jnp.stack([jnp.concatenate([_block_diag(p['rg_w_a'][l, d]), _block_diag(p['rg_w_i'][l, d])], axis=1)
                      for d in range(2)])
    rg_b = jnp.stack([jnp.concatenate([p['rg_b_a'][l, d], p['rg_b_i'][l, d]])[None, :] for d in range(2)])
    g_out = p['mix_out_norm_g'][l]
    row = lambda v: v.reshape(1, -1).astype(F32)
    return dict(
        g_mix=row(p['norm_mix_g'][l]), w1=w1.astype(BF16), g_q=row(p['mla_q_norm_g'][l]), w_uq=w_uq2.astype(BF16),
        g_kv=row(p['mla_kv_norm_g'][l]), w_ukv=w_ukv2.astype(BF16),
        conv_w=p['conv_w'][l].astype(F32), conv_b=row(p['conv_b'][l]), rg_w=rg_w.astype(BF16), rg_b=rg_b.astype(F32),
        rg_lam=p['rg_lambda'][l].reshape(2, 1, LRU_WIDTH).astype(F32),
        g_l=row(g_out[:LRU_WIDTH]), g_d=row(g_out[LRU_WIDTH:LRU_WIDTH + DIL_WIDTH]),
        g_m=row(g_out[LRU_WIDTH + DIL_WIDTH:]), w_out=p['w_out'][l].astype(BF16), g_ffn=row(p['norm_ffn_g'][l]))


def _mixers(x2d, lw, tabs, B, S, tm, moe):
    xr, gy, q, k, v, qm, km, vm = _in_proj(x2d, lw, tabs, B, S, tm)
    hf, hb = _lru(xr, lw, B, S)
    dil = _dilated(q, k, v, B, S)
    mo = _mla(qm, km, vm, B, S, 256)
    return _mix_out(x2d, hf, hb, gy, dil, mo, lw, B, S, tm, moe)


def _trunk(x, layers, p, depth):
    B, S, _ = x.shape
    tabs = _rope_tables(S)
    x2d = x.reshape(B * S, D_MODEL)
    tm = 512
    for l in range(depth):
        lw = layers[l]
        moe = l % 2 == 1
        last = l == depth - 1
        gfin = p['final_norm_g'].reshape(1, D_MODEL) if last else None
        j = l // 2
        if moe:
            rw = jnp.pad(p['router_w'][j].astype(F32), ((0, 0), (0, LANES - N_EXPERTS)))
            rw_hi = rw.astype(BF16)
            lw = dict(lw, router_w=jnp.stack([rw_hi, (rw - rw_hi.astype(F32)).astype(BF16)]))
            x1, h2, route, counts = _mixers(x2d, lw, tabs, B, S, tm, True)
            x2d = _moe(h2, x1, route, counts, lw['moe_wg'], lw['moe_wu'], lw['moe_wd'], gfin, 896)
        else:
            x1, h2 = _mixers(x2d, lw, tabs, B, S, tm, False)
            x2d = _ffn(h2, x1, lw['ffn_wg'], lw['ffn_wu'], lw['ffn_wd'], gfin, tm, 1408)
    return x2d.reshape(B, S, D_MODEL)


def kernel(x_prompt, x_sample, norm_mix_g, w_in, conv_w, conv_b, rg_w_a, rg_b_a, rg_w_i, rg_b_i, rg_lambda, mla_q_norm_g, mla_w_uq, mla_kv_norm_g, mla_w_ukv, mix_out_norm_g, w_out, norm_ffn_g, ffn_w_gate, ffn_w_up, ffn_w_down, router_w, moe_w_gate, moe_w_up, moe_w_down, final_norm_g):
    p = dict(norm_mix_g=norm_mix_g, w_in=w_in, conv_w=conv_w, conv_b=conv_b, rg_w_a=rg_w_a, rg_b_a=rg_b_a,
             rg_w_i=rg_w_i, rg_b_i=rg_b_i, rg_lambda=rg_lambda, mla_q_norm_g=mla_q_norm_g, mla_w_uq=mla_w_uq,
             mla_kv_norm_g=mla_kv_norm_g, mla_w_ukv=mla_w_ukv, mix_out_norm_g=mix_out_norm_g, w_out=w_out,
             norm_ffn_g=norm_ffn_g, router_w=router_w, final_norm_g=final_norm_g)
    depth = w_in.shape[0]
    layers = []
    for l in range(depth):
        lw = _prep_layer(l, p)
        j = l // 2
        if l % 2 == 1:
            lw.update(moe_wg=moe_w_gate[j].astype(BF16), moe_wu=moe_w_up[j].astype(BF16),
                      moe_wd=moe_w_down[j].astype(BF16))
        else:
            lw.update(ffn_wg=ffn_w_gate[j].astype(BF16), ffn_wu=ffn_w_up[j].astype(BF16),
                      ffn_wd=ffn_w_down[j].astype(BF16))
        layers.append(lw)
    return (_trunk(x_prompt, layers, p, depth), _trunk(x_sample, layers, p, depth))
```

```python
import functools
import math

import jax
import jax.numpy as jnp
from jax import lax
from jax.experimental import pallas as pl
from jax.experimental.pallas import tpu as pltpu

F32 = jnp.float32
BF16 = jnp.bfloat16

D_MODEL = 1024
NORM_EPS = 1e-6
ROPE_THETA = 10000.0
LRU_WIDTH = 256
LRU_BLOCKS = 4
CONV_WIDTH = 4
LRU_C = 8.0
DIL_HEADS = 4
DIL_HEAD_DIM = 64
DIL_WIDTH = DIL_HEADS * DIL_HEAD_DIM
DIL_PATTERNS = ((128, 1), (512, 4), (2048, 16))
MLA_HEADS = 8
MLA_NOPE = 64
MLA_ROPE = 32
MLA_V = 64
MLA_Q_RANK = 256
MLA_KV_RANK = 128
MLA_WIDTH = MLA_HEADS * MLA_V
N_EXPERTS = 8
TOP_K = 2

LANES = 128
SUBLANES = 8
MLA_HEAD_PAD = LANES
VMEM_LIMIT = 56 * 1024 * 1024
LOG2E = math.log2(math.e)

_C_XR, _C_YR, _C_Q, _C_K, _C_V, _C_CQ, _C_CKV, _C_KR, _C_END = (0, 256, 512, 768, 1024, 1280, 1536, 1664, 1792)


def _cparams(sem):
    return pltpu.CompilerParams(dimension_semantics=sem, vmem_limit_bytes=VMEM_LIMIT)


def _rms(x, g, n):
    ms = jnp.sum(x * x, axis=-1, keepdims=True) * (1.0 / n)
    return x * lax.rsqrt(ms + NORM_EPS) * g


def _gelu_tanh(x):
    return 0.5 * x * (1.0 + jnp.tanh(0.7978845608028654 * (x + 0.044715 * (x * x * x))))


def _dot(a, b):
    return jnp.dot(a, b, preferred_element_type=F32)


def _dot_nt(a, b):
    return lax.dot_general(a, b, (((1,), (1,)), ((), ())), preferred_element_type=F32)


def _lane_groups(width):
    return [slice(c * LANES, (c + 1) * LANES) for c in range(width // LANES)]


def _swap_rotary_halves(x, half, period, start):
    lane = lax.broadcasted_iota(jnp.int32, (1, LANES), 1) % period
    first = lane < start + half
    outs = []
    for cs in _lane_groups(x.shape[1]):
        xs = x[:, cs]
        outs.append(jnp.where(first, pltpu.roll(xs, LANES - half, 1), pltpu.roll(xs, half, 1)))
    return outs[0] if len(outs) == 1 else jnp.concatenate(outs, axis=-1)


def _in_proj_kernel(x_ref, g_ref, w1_ref, gq_ref, wuq_ref, gkv_ref, wukv_ref,
                    cosd_ref, sind_ref, cosm_ref, sinm_ref,
                    xr_ref, gy_ref, q_ref, k_ref, v_ref, qm_ref, km_ref, vm_ref):
    h = _rms(x_ref[...], g_ref[...], D_MODEL).astype(BF16)
    p = _dot(h, w1_ref[...])
    xr_ref[...] = p[:, _C_XR:_C_YR]
    gy_ref[...] = _gelu_tanh(p[:, _C_YR:_C_Q]).astype(BF16)
    cosd, sind = cosd_ref[...], sind_ref[...]
    half_d = DIL_HEAD_DIM // 2
    q, k = p[:, _C_Q:_C_K], p[:, _C_K:_C_V]
    q = q * cosd + _swap_rotary_halves(q, half_d, DIL_HEAD_DIM, 0) * sind
    q_ref[...] = (q * (DIL_HEAD_DIM ** -0.5 * LOG2E)).astype(BF16)
    k_ref[...] = (k * cosd + _swap_rotary_halves(k, half_d, DIL_HEAD_DIM, 0) * sind).astype(BF16)
    v_ref[...] = p[:, _C_V:_C_CQ].astype(BF16)

    cosm, sinm = cosm_ref[...], sinm_ref[...]
    half_m = MLA_ROPE // 2
    cqn = _rms(p[:, _C_CQ:_C_CKV], gq_ref[...], MLA_Q_RANK).astype(BF16)
    qq = _dot(cqn, wuq_ref[...])
    qq_sw = _swap_rotary_halves(qq, half_m, MLA_HEAD_PAD, MLA_NOPE)
    ckvn = _rms(p[:, _C_CKV:_C_KR], gkv_ref[...], MLA_KV_RANK).astype(BF16)
    kv = _dot(ckvn, wukv_ref[...])
    kr = p[:, _C_KR:_C_END]
    kr = kr * cosm + _swap_rotary_halves(kr, half_m, MLA_HEAD_PAD, MLA_NOPE) * sinm
    scale = (MLA_NOPE + MLA_ROPE) ** -0.5 * LOG2E
    nq = MLA_HEADS * MLA_HEAD_PAD
    for hh in range(MLA_HEADS):
        lo, hi = hh * MLA_HEAD_PAD, (hh + 1) * MLA_HEAD_PAD
        qh = qq[:, lo:hi] * cosm + qq_sw[:, lo:hi] * sinm
        qm_ref[:, lo:hi] = (qh * scale).astype(BF16)
        km_ref[:, lo:hi] = (kv[:, lo:hi] + kr).astype(BF16)
    vm_ref[...] = kv[:, nq:].astype(BF16)


def _in_proj(x2d, lw, tabs, B, S, tm):
    T = B * S
    ns = S // tm
    tok = lambda b, s: (b * ns + s, 0)
    pos = lambda b, s: (s, 0)
    const = lambda b, s: (0, 0)

    def full(a):
        return pl.BlockSpec(a.shape, const)

    def tokspec(c):
        return pl.BlockSpec((tm, c), tok)

    out_shape = (
        jax.ShapeDtypeStruct((T, LRU_WIDTH), F32),
        jax.ShapeDtypeStruct((T, LRU_WIDTH), BF16),
        jax.ShapeDtypeStruct((T, DIL_WIDTH), BF16),
        jax.ShapeDtypeStruct((T, DIL_WIDTH), BF16),
        jax.ShapeDtypeStruct((T, DIL_WIDTH), BF16),
        jax.ShapeDtypeStruct((T, MLA_HEADS * MLA_HEAD_PAD), BF16),
        jax.ShapeDtypeStruct((T, MLA_HEADS * MLA_HEAD_PAD), BF16),
        jax.ShapeDtypeStruct((T, MLA_WIDTH), BF16),
    )
    out_specs = (
        tokspec(LRU_WIDTH), tokspec(LRU_WIDTH), tokspec(DIL_WIDTH), tokspec(DIL_WIDTH), tokspec(DIL_WIDTH),
        tokspec(MLA_HEADS * MLA_HEAD_PAD), tokspec(MLA_HEADS * MLA_HEAD_PAD), tokspec(MLA_WIDTH),
    )
    in_specs = [
        tokspec(D_MODEL), full(lw['g_mix']), full(lw['w1']), full(lw['g_q']), full(lw['w_uq']),
        full(lw['g_kv']), full(lw['w_ukv']),
        pl.BlockSpec((tm, DIL_WIDTH), pos), pl.BlockSpec((tm, DIL_WIDTH), pos),
        pl.BlockSpec((tm, LANES), pos), pl.BlockSpec((tm, LANES), pos),
    ]
    return pl.pallas_call(
        _in_proj_kernel, grid=(B, ns), in_specs=in_specs, out_specs=out_specs, out_shape=out_shape,
        compiler_params=_cparams(("parallel", "parallel")), name="in_proj",
    )(x2d, lw['g_mix'], lw['w1'], lw['g_q'], lw['w_uq'], lw['g_kv'], lw['w_ukv'],
      tabs['cosd'], tabs['sind'], tabs['cosm'], tabs['sinm'])


LRU_CHUNK_ROWS = 1024
LRU_HALO = SUBLANES


def _lru_kernel(xf_ref, xfp_ref, xfn_ref, xb_ref, xbp_ref, xbn_ref, cw_ref, cb_ref, wg_ref, bg_ref,
                lam_ref, hf_ref, hb_ref, xpad, a_f, b_f, a_b, b_b, hs_f, hs_b, hcar, *, B, tc):
    i = pl.program_id(0)
    n = pl.num_programs(0)
    R = tc * B
    left = CONV_WIDTH // 2
    groups = _lane_groups(LRU_WIDTH)

    @pl.when(i == 0)
    def _():
        hcar[...] = jnp.zeros_like(hcar)

    def prep(x_ref, xp_ref, xn_ref, ci, d, a_s, b_s):
        for c, cs in enumerate(groups):
            for b in range(B):
                xpad[c, pl.ds(b, left, stride=B), :] = jnp.where(ci > 0, xp_ref[b, LRU_HALO - left:LRU_HALO, cs], 0.0)
                xpad[c, pl.ds(left * B + b, tc, stride=B), :] = x_ref[b, :, cs]
                xpad[c, pl.ds((left + tc) * B + b, 1), :] = jnp.where(ci < n - 1, xn_ref[b, 0:1, cs], 0.0)
        halves = []
        for c, cs in enumerate(groups):
            xc = cb_ref[:, cs] + xpad[c, 0:R, :] * cw_ref[0:1, cs]
            for kk in range(1, CONV_WIDTH):
                xc = xc + xpad[c, kk * B:kk * B + R, :] * cw_ref[kk:kk + 1, cs]
            halves.append(xc)
        xc = jnp.concatenate(halves, axis=-1)
        g = _dot(xc.astype(BF16), wg_ref[d]) + bg_ref[d]
        r = jax.nn.sigmoid(g[:, :LRU_WIDTH])
        ig = jax.nn.sigmoid(g[:, LRU_WIDTH:])
        lam = lam_ref[d]
        softplus = jnp.maximum(-lam, 0.0) + jnp.log(1.0 + jnp.exp(-jnp.abs(lam)))
        log_a = (-LRU_C) * r * softplus
        a_s[...] = jnp.exp(log_a)
        b_s[...] = jnp.sqrt(1.0 - jnp.exp(2.0 * log_a)) * (ig * xc)

    prep(xf_ref, xfp_ref, xfn_ref, i, 0, a_f, b_f)
    prep(xb_ref, xbp_ref, xbn_ref, n - 1 - i, 1, a_b, b_b)

    def body(s, carry):
        hf, hb = carry
        rf = pl.multiple_of(s * B, B)
        hf = a_f[pl.ds(rf, B), :] * hf + b_f[pl.ds(rf, B), :]
        rb = pl.multiple_of((tc - 1 - s) * B, B)
        hb = a_b[pl.ds(rb, B), :] * hb + b_b[pl.ds(rb, B), :]
        for c, cs in enumerate(groups):
            hs_f[c, pl.ds(rf, B), :] = hf[:, cs]
            hs_b[c, pl.ds(rb, B), :] = hb[:, cs]
        return hf, hb

    hf, hb = lax.fori_loop(0, tc, body, (hcar[0], hcar[1]), unroll=8)
    hcar[0] = hf
    hcar[1] = hb
    for c, cs in enumerate(groups):
        for b in range(B):
            hf_ref[b, :, cs] = hs_f[c, pl.ds(b, tc, stride=B), :]
            hb_ref[b, :, cs] = hs_b[c, pl.ds(b, tc, stride=B), :]


def _lru(xr, lw, B, S):
    tc = LRU_CHUNK_ROWS // B
    n = S // tc
    nh = S // LRU_HALO
    hb_per = tc // LRU_HALO
    x3 = xr.reshape(B, S, LRU_WIDTH)
    W = LRU_WIDTH

    main_f = pl.BlockSpec((B, tc, W), lambda i: (0, i, 0))
    prev_f = pl.BlockSpec((B, LRU_HALO, W), lambda i: (0, jnp.maximum(i * hb_per - 1, 0), 0))
    next_f = pl.BlockSpec((B, LRU_HALO, W), lambda i: (0, jnp.minimum((i + 1) * hb_per, nh - 1), 0))
    main_b = pl.BlockSpec((B, tc, W), lambda i: (0, n - 1 - i, 0))
    prev_b = pl.BlockSpec((B, LRU_HALO, W), lambda i: (0, jnp.maximum((n - 1 - i) * hb_per - 1, 0), 0))
    next_b = pl.BlockSpec((B, LRU_HALO, W), lambda i: (0, jnp.minimum((n - i) * hb_per, nh - 1), 0))

    def full(a):
        nd = a.ndim
        return pl.BlockSpec(a.shape, lambda i: (0,) * nd)

    R = tc * B
    ng = W // LANES
    hf, hb = pl.pallas_call(
        functools.partial(_lru_kernel, B=B, tc=tc),
        grid=(n,),
        in_specs=[main_f, prev_f, next_f, main_b, prev_b, next_b,
                  full(lw['conv_w']), full(lw['conv_b']), full(lw['rg_w']), full(lw['rg_b']), full(lw['rg_lam'])],
        out_specs=(main_f, main_b),
        out_shape=(jax.ShapeDtypeStruct((B, S, W), F32),) * 2,
        scratch_shapes=[pltpu.VMEM((ng, R + (CONV_WIDTH - 1) * B, LANES), F32)] + [pltpu.VMEM((R, W), F32)] * 4
        + [pltpu.VMEM((ng, R, LANES), F32)] * 2 + [pltpu.VMEM((2, B, W), F32)],
        compiler_params=_cparams(("arbitrary",)), name="rglru",
    )(x3, x3, x3, x3, x3, x3, lw['conv_w'], lw['conv_b'], lw['rg_w'], lw['rg_b'], lw['rg_lam'])
    return hf.reshape(B * S, W), hb.reshape(B * S, W)


BAND_HALF = 64
DIL_TILE = 1024
DIL_QBLOCK = 256


def _dil_kernel(q_ref, kp_ref, k_ref, kn_ref, vp_ref, v_ref, vn_ref, o_ref,
                qs, kcat, vcat, acc_m, acc_l, acc_o, *, S):
    TT = DIL_TILE
    tile_start = pl.program_id(1) * TT
    groups = _lane_groups(DIL_WIDTH)
    for c, cs in enumerate(groups):
        qs[c] = q_ref[0, :, cs].astype(F32)
        for j, (kr, vr) in enumerate(((kp_ref, vp_ref), (k_ref, v_ref), (kn_ref, vn_ref))):
            kcat[c, j * TT:(j + 1) * TT, :] = kr[0, :, cs].astype(F32)
            vcat[c, j * TT:(j + 1) * TT, :] = vr[0, :, cs].astype(F32)
    head = lax.broadcasted_iota(jnp.int32, (1, DIL_WIDTH), 1) // DIL_HEAD_DIM

    order = sorted((dil for _, dil in DIL_PATTERNS), reverse=True)
    for dil in order:
        first = dil == order[0]
        nq = min(DIL_QBLOCK, TT // dil)
        nsub = TT // dil // nq
        nk = nq + 2 * BAND_HALF
        row = lax.broadcasted_iota(jnp.int32, (nq, nk), 0)
        col = lax.broadcasted_iota(jnp.int32, (nq, nk), 1)
        band = jnp.abs(col - BAND_HALF - row) <= BAND_HALF

        def rows(start, n, dil=dil):
            return pl.ds(pl.multiple_of(start, SUBLANES), n) if dil == 1 else pl.ds(start, n, stride=dil)

        def ld(ref, idx):
            return jnp.concatenate([ref[c, idx, :] for c in range(len(groups))], axis=-1)

        def st(ref, idx, val):
            for c, cs in enumerate(groups):
                ref[c, idx, :] = val[:, cs]

        def block(it, carry, dil=dil, nq=nq, nsub=nsub, nk=nk, band=band, col=col, rows=rows, first=first):
            r, j = (0, it) if dil == 1 else (it, 0)
            off = j * nq * dil + r
            qi = rows(off, nq)
            ki = rows(off + TT - BAND_HALF * dil, nk)
            kpos = tile_start + off - BAND_HALF * dil + col * dil
            valid = band & (kpos >= 0) & (kpos < S)
            q = ld(qs, qi).astype(BF16)
            k = ld(kcat, ki).astype(BF16)
            v = ld(vcat, ki).astype(BF16)
            m_b = jnp.zeros((nq, DIL_WIDTH), F32)
            l_b = jnp.zeros((nq, DIL_WIDTH), F32)
            o_b = jnp.zeros((nq, DIL_WIDTH), F32)
            per = DIL_HEADS if nq * DIL_HEADS <= DIL_QBLOCK else 1
            groups_h = [range(h0, h0 + per) for h0 in range(0, DIL_HEADS, per)]
            valid_g = jnp.concatenate([valid] * per, axis=0)
            scores = [_dot_nt(jnp.concatenate([jnp.where(head == h, q, jnp.zeros_like(q)) for h in hs], axis=0), k)
                      for hs in groups_h]
            stats = []
            for s in scores:
                s = jnp.where(valid_g, s, -1e30)
                m = jnp.max(s, axis=-1, keepdims=True)
                e = jnp.exp2(s - m)
                stats.append((m, jnp.sum(e, axis=-1, keepdims=True), e.astype(BF16)))
            for hs, (m, l, e) in zip(groups_h, stats):
                o = _dot(e, v)
                for n_, h in enumerate(hs):
                    hm = head == h
                    sl = slice(n_ * nq, (n_ + 1) * nq)
                    m_b = jnp.where(hm, m[sl], m_b)
                    l_b = jnp.where(hm, l[sl], l_b)
                    o_b = jnp.where(hm, o[sl], o_b)
            if first:
                st(acc_l, qi, l_b)
                st(acc_o, qi, o_b)
                st(acc_m, qi, m_b)
                return carry
            m_old = ld(acc_m, qi)
            m_new = jnp.maximum(m_old, m_b)
            w_old = jnp.exp2(m_old - m_new)
            w_b = jnp.exp2(m_b - m_new)
            st(acc_l, qi, w_old * ld(acc_l, qi) + w_b * l_b)
            st(acc_o, qi, w_old * ld(acc_o, qi) + w_b * o_b)
            st(acc_m, qi, m_new)
            return carry

        lax.fori_loop(0, dil * nsub, block, 0, unroll=4 if nq < DIL_QBLOCK else 1)

    for c, cs in enumerate(groups):
        o_ref[0, :, cs] = (acc_o[c] / acc_l[c]).astype(BF16)


def _dilated(q, k, v, B, S):
    TT = DIL_TILE
    nt = S // TT
    W = DIL_WIDTH
    ng = W // LANES
    q3, k3, v3 = (a.reshape(B, S, W) for a in (q, k, v))
    own = pl.BlockSpec((1, TT, W), lambda b, i: (b, i, 0))
    prev = pl.BlockSpec((1, TT, W), lambda b, i: (b, jnp.maximum(i - 1, 0), 0))
    nxt = pl.BlockSpec((1, TT, W), lambda b, i: (b, jnp.minimum(i + 1, nt - 1), 0))
    o = pl.pallas_call(
        functools.partial(_dil_kernel, S=S), grid=(B, nt),
        in_specs=[own, prev, own, nxt, prev, own, nxt], out_specs=own,
        out_shape=jax.ShapeDtypeStruct((B, S, W), BF16),
        scratch_shapes=[pltpu.VMEM((ng, TT, LANES), F32), pltpu.VMEM((ng, 3 * TT, LANES), F32),
                        pltpu.VMEM((ng, 3 * TT, LANES), F32)] + [pltpu.VMEM((ng, TT, LANES), F32)] * 3,
        compiler_params=_cparams(("parallel", "parallel")), name="dilated_attn",
    )(q3, k3, k3, k3, v3, v3, v3)
    return o.reshape(B * S, W)


ROW_REDUCE_CHAINS = 16


def _reduce_rows(x, op):
    rows, cols = x.shape
    part = op(x.reshape(ROW_REDUCE_CHAINS, rows // (ROW_REDUCE_CHAINS * SUBLANES), SUBLANES, cols), axis=1)
    return op(op(part, axis=0), axis=0, keepdims=True)


def _mla_kernel(q_ref, k_ref, v_ref, o_ref, vt):
    @pl.when(pl.program_id(1) == 0)
    def _():
        vt[...] = v_ref[0].T

    def scores(h):
        lo, hi = h * MLA_HEAD_PAD, (h + 1) * MLA_HEAD_PAD
        return _dot_nt(k_ref[0, :, lo:hi], q_ref[0, :, lo:hi])

    s_next = scores(0)
    outs = []
    for h in range(MLA_HEADS):
        s = s_next
        if h + 1 < MLA_HEADS:
            s_next = scores(h + 1)
        m = _reduce_rows(s, jnp.max)
        e = jnp.exp2(s - m)
        l = _reduce_rows(e, jnp.sum)
        outs.append(_dot(vt[h * MLA_V:(h + 1) * MLA_V, :], e.astype(BF16)) * (1.0 / l))
        if h % 2 == 1:
            pair = h // 2
            o_ref[0, :, pair * LANES:(pair + 1) * LANES] = jnp.concatenate(outs[h - 1:h + 1], axis=0).T.astype(BF16)


def _mla(qm, km, vm, B, S, tq):
    QW = MLA_HEADS * MLA_HEAD_PAD
    q3, k3, v3 = qm.reshape(B, S, QW), km.reshape(B, S, QW), vm.reshape(B, S, MLA_WIDTH)
    o = pl.pallas_call(
        _mla_kernel, grid=(B, S // tq),
        in_specs=[pl.BlockSpec((1, tq, QW), lambda b, i: (b, i, 0)),
                  pl.BlockSpec((1, S, QW), lambda b, i: (b, 0, 0)),
                  pl.BlockSpec((1, S, MLA_WIDTH), lambda b, i: (b, 0, 0))],
        out_specs=pl.BlockSpec((1, tq, MLA_WIDTH), lambda b, i: (b, i, 0)),
        out_shape=jax.ShapeDtypeStruct((B, S, MLA_WIDTH), BF16),
        scratch_shapes=[pltpu.VMEM((MLA_WIDTH, S), BF16)],
        compiler_params=_cparams(("parallel", "arbitrary")), name="mla_attn",
    )(q3, k3, v3)
    return o.reshape(B * S, MLA_WIDTH)


def _mix_out_kernel(*refs, moe):
    (x_ref, hf_ref, hb_ref, gy_ref, dil_ref, mo_ref, gl_ref, gd_ref, gm_ref, wout_ref, gffn_ref) = refs[:11]
    if moe:
        rw_ref, x1_ref, h2_ref, route_ref, cnt_ref, carry = refs[11:]
    else:
        x1_ref, h2_ref = refs[11:]
    lru = (hf_ref[...] + hb_ref[...]) * gy_ref[...].astype(F32)
    mix = jnp.concatenate([
        _rms(lru, gl_ref[...], LRU_WIDTH), _rms(dil_ref[...].astype(F32), gd_ref[...], DIL_WIDTH),
        _rms(mo_ref[...].astype(F32), gm_ref[...], MLA_WIDTH)], axis=-1).astype(BF16)
    x1 = x_ref[...] + _dot(mix, wout_ref[...])
    x1_ref[...] = x1
    h2 = _rms(x1, gffn_ref[...], D_MODEL)
    h2_ref[...] = h2.astype(h2_ref.dtype)
    if moe:
        first = (pl.program_id(0) == 0) & (pl.program_id(1) == 0)

        @pl.when(first)
        def _():
            carry[...] = jnp.zeros_like(carry)

        h_hi = h2.astype(BF16)
        h_lo = (h2 - h_hi.astype(F32)).astype(BF16)
        logits = _dot(h_hi, rw_ref[0]) + (_dot(h_hi, rw_ref[1]) + _dot(h_lo, rw_ref[0]))
        tm = logits.shape[0]
        lane = lax.broadcasted_iota(jnp.int32, logits.shape, 1)
        neg = -jnp.inf
        lg = jnp.where(lane < N_EXPERTS, logits, neg)
        m1 = jnp.max(lg, axis=-1, keepdims=True)
        i1 = jnp.min(jnp.where(lg == m1, lane, LANES), axis=-1, keepdims=True)
        lg2 = jnp.where(lane == i1, neg, lg)
        m2 = jnp.max(lg2, axis=-1, keepdims=True)
        i2 = jnp.min(jnp.where(lg2 == m2, lane, LANES), axis=-1, keepdims=True)
        e2 = jnp.exp(m2 - m1)
        den = 1.0 + e2
        sel1, sel2 = lane == i1, lane == i2
        onehot = jnp.where(sel1 | sel2, 1.0, 0.0)
        tri = (lax.broadcasted_iota(jnp.int32, (tm, tm), 0) > lax.broadcasted_iota(jnp.int32, (tm, tm), 1))
        before = _dot(jnp.where(tri, 1.0, 0.0).astype(BF16), onehot.astype(BF16)) + carry[0:1, :]
        rank1 = jnp.sum(jnp.where(sel1, before, 0.0), axis=-1, keepdims=True)
        rank2 = jnp.sum(jnp.where(sel2, before, 0.0), axis=-1, keepdims=True)
        carry[...] = carry[...] + jnp.sum(onehot, axis=0, keepdims=True)
        cnt_ref[...] = carry[...]
        cols = (i1.astype(F32), i2.astype(F32), 1.0 / den, e2 / den, rank1, rank2)
        route = jnp.zeros(logits.shape, F32)
        for c, val in enumerate(cols):
            route = jnp.where(lane == c, val, route)
        route_ref[...] = route


def _mix_out(x2d, hf, hb, gy, dil, mo, lw, B, S, tm, moe):
    T = B * S
    ns = S // tm
    tok = lambda b, s: (b * ns + s, 0)
    const = lambda b, s: (0, 0)
    tokspec = lambda c: pl.BlockSpec((tm, c), tok)
    full = lambda a: pl.BlockSpec(a.shape, const)
    args = [x2d, hf, hb, gy, dil, mo, lw['g_l'], lw['g_d'], lw['g_m'], lw['w_out'], lw['g_ffn']]
    in_specs = [tokspec(D_MODEL), tokspec(LRU_WIDTH), tokspec(LRU_WIDTH), tokspec(LRU_WIDTH), tokspec(DIL_WIDTH),
                tokspec(MLA_WIDTH), full(lw['g_l']), full(lw['g_d']), full(lw['g_m']), full(lw['w_out']),
                full(lw['g_ffn'])]
    out_shape = [jax.ShapeDtypeStruct((T, D_MODEL), F32), jax.ShapeDtypeStruct((T, D_MODEL), F32 if moe else BF16)]
    out_specs = [tokspec(D_MODEL), tokspec(D_MODEL)]
    scratch = []
    if moe:
        args.append(lw['router_w'])
        in_specs.append(pl.BlockSpec(lw['router_w'].shape, lambda b, s: (0, 0, 0)))
        out_shape += [jax.ShapeDtypeStruct((T, LANES), F32), jax.ShapeDtypeStruct((SUBLANES, LANES), F32)]
        out_specs += [tokspec(LANES), pl.BlockSpec((SUBLANES, LANES), const)]
        scratch = [pltpu.VMEM((SUBLANES, LANES), F32)]
    sem = ("arbitrary", "arbitrary") if moe else ("parallel", "parallel")
    return pl.pallas_call(
        functools.partial(_mix_out_kernel, moe=moe), grid=(B, ns), in_specs=in_specs,
        out_specs=tuple(out_specs), out_shape=tuple(out_shape), scratch_shapes=scratch,
        compiler_params=_cparams(sem), name="mix_out",
    )(*args)


def _ffn_kernel(*refs, final):
    if final:
        h_ref, x1_ref, wg_ref, wu_ref, wd_ref, gfin_ref, o_ref = refs
    else:
        h_ref, x1_ref, wg_ref, wu_ref, wd_ref, o_ref = refs
    f = pl.program_id(1)

    @pl.when(f == 0)
    def _():
        o_ref[...] = x1_ref[...]

    h = h_ref[...]
    g = _dot(h, wg_ref[...])
    u = _dot(h, wu_ref[...])
    a = (g * jax.nn.sigmoid(g) * u).astype(BF16)
    o_ref[...] += _dot(a, wd_ref[...])

    if final:
        @pl.when(f == pl.num_programs(1) - 1)
        def _():
            o_ref[...] = _rms(o_ref[...], gfin_ref[...], D_MODEL)


def _ffn(h2, x1, wg, wu, wd, gfin, tm, tf):
    T = h2.shape[0]
    F = wg.shape[1]
    final = gfin is not None
    args = [h2, x1, wg, wu, wd] + ([gfin] if final else [])
    wmode = dict(pipeline_mode=pl.Buffered(1)) if tf == F else {}
    in_specs = [pl.BlockSpec((tm, D_MODEL), lambda i, f: (i, 0)), pl.BlockSpec((tm, D_MODEL), lambda i, f: (i, 0)),
                pl.BlockSpec((D_MODEL, tf), lambda i, f: (0, f), **wmode),
                pl.BlockSpec((D_MODEL, tf), lambda i, f: (0, f), **wmode),
                pl.BlockSpec((tf, D_MODEL), lambda i, f: (f, 0), **wmode)]
    if final:
        in_specs.append(pl.BlockSpec(gfin.shape, lambda i, f: (0, 0)))
    return pl.pallas_call(
        functools.partial(_ffn_kernel, final=final), grid=(T // tm, F // tf), in_specs=in_specs,
        out_specs=pl.BlockSpec((tm, D_MODEL), lambda i, f: (i, 0)),
        out_shape=jax.ShapeDtypeStruct((T, D_MODEL), F32),
        compiler_params=_cparams(("parallel", "arbitrary")), name="ffn",
    )(*args)


MOE_BLOCK_ROWS = 512
DISPATCH_TOKENS = 1024
COMBINE_TOKENS = 512


def _dispatch_kernel(pos_ref, h_ref, xs_in, xs_out, sem, *, tmd):
    del xs_in

    def body(t, c):
        src = h_ref.at[pl.ds(t, 1), :]
        for k in range(TOP_K):
            pltpu.make_async_copy(src, xs_out.at[pl.ds(pos_ref[0, 0, TOP_K * t + k], 1), :], sem).start()
        return c

    lax.fori_loop(0, tmd, body, 0, unroll=4)
    for _ in range(TOP_K):
        pltpu.make_async_copy(h_ref, xs_out.at[pl.ds(0, tmd), :], sem).wait()


def _dispatch(h2, pos, n_rows):
    T = h2.shape[0]
    tmd = DISPATCH_TOKENS
    pos3 = pos.reshape(T // tmd, 1, TOP_K * tmd)
    xs0 = jnp.zeros((n_rows, D_MODEL), F32)
    return pl.pallas_call(
        functools.partial(_dispatch_kernel, tmd=tmd), grid=(T // tmd,),
        in_specs=[pl.BlockSpec((1, 1, TOP_K * tmd), lambda i: (i, 0, 0), memory_space=pltpu.SMEM),
                  pl.BlockSpec((tmd, D_MODEL), lambda i: (i, 0)), pl.BlockSpec(memory_space=pl.ANY)],
        out_specs=pl.BlockSpec(memory_space=pl.ANY),
        out_shape=jax.ShapeDtypeStruct((n_rows, D_MODEL), F32),
        scratch_shapes=[pltpu.SemaphoreType.DMA(())],
        input_output_aliases={2: 0},
        compiler_params=_cparams(("arbitrary",)), name="moe_dispatch",
    )(pos3, h2, xs0)


def _gmm_kernel(be_ref, xs_ref, wg_ref, wu_ref, wd_ref, y_ref, xb):
    del be_ref
    f = pl.program_id(1)

    @pl.when(f == 0)
    def _():
        xb[...] = xs_ref[...].astype(BF16)

    h = xb[...]
    g = _dot(h, wg_ref[0])
    u = _dot(h, wu_ref[0])
    part = _dot((g * jax.nn.sigmoid(g) * u).astype(BF16), wd_ref[0])

    @pl.when(f == 0)
    def _():
        y_ref[...] = part

    @pl.when(f > 0)
    def _():
        y_ref[...] += part


def _gmm(xs, block_expert, wg, wu, wd, tf):
    P = xs.shape[0]
    bm = MOE_BLOCK_ROWS
    F = wg.shape[2]
    grid_spec = pltpu.PrefetchScalarGridSpec(
        num_scalar_prefetch=1, grid=(P // bm, F // tf),
        in_specs=[pl.BlockSpec((bm, D_MODEL), lambda j, f, be: (j, 0)),
                  pl.BlockSpec((1, D_MODEL, tf), lambda j, f, be: (be[j], 0, f)),
                  pl.BlockSpec((1, D_MODEL, tf), lambda j, f, be: (be[j], 0, f)),
                  pl.BlockSpec((1, tf, D_MODEL), lambda j, f, be: (be[j], f, 0))],
        out_specs=pl.BlockSpec((bm, D_MODEL), lambda j, f, be: (j, 0)),
        scratch_shapes=[pltpu.VMEM((bm, D_MODEL), BF16)])
    return pl.pallas_call(
        _gmm_kernel, grid_spec=grid_spec, out_shape=jax.ShapeDtypeStruct((P, D_MODEL), F32),
        compiler_params=_cparams(("parallel", "arbitrary")), name="moe_gmm",
    )(block_expert, xs, wg, wu, wd)


def _combine_kernel(*refs, tmc, final):
    if final:
        pos_ref, route_ref, x1_ref, y_hbm, gfin_ref, o_ref, ybuf, sem = refs
    else:
        pos_ref, route_ref, x1_ref, y_hbm, o_ref, ybuf, sem = refs

    def body(t, c):
        for k in range(TOP_K):
            pltpu.make_async_copy(y_hbm.at[pl.ds(pos_ref[0, 0, TOP_K * t + k], 1), :],
                                  ybuf.at[k, pl.ds(t, 1), :], sem).start()
        return c

    lax.fori_loop(0, tmc, body, 0, unroll=4)
    for k in range(TOP_K):
        pltpu.make_async_copy(y_hbm.at[pl.ds(0, tmc), :], ybuf.at[k], sem).wait()
    route = route_ref[...]
    out = x1_ref[...] + route[:, 2:3] * ybuf[0] + route[:, 3:4] * ybuf[1]
    if final:
        out = _rms(out, gfin_ref[...], D_MODEL)
    o_ref[...] = out


def _combine(pos, route, x1, y, gfin):
    T = x1.shape[0]
    tmc = COMBINE_TOKENS
    final = gfin is not None
    pos3 = pos.reshape(T // tmc, 1, TOP_K * tmc)
    args = [pos3, route, x1, y] + ([gfin] if final else [])
    in_specs = [pl.BlockSpec((1, 1, TOP_K * tmc), lambda i: (i, 0, 0), memory_space=pltpu.SMEM),
                pl.BlockSpec((tmc, LANES), lambda i: (i, 0)), pl.BlockSpec((tmc, D_MODEL), lambda i: (i, 0)),
                pl.BlockSpec(memory_space=pl.ANY)]
    if final:
        in_specs.append(pl.BlockSpec(gfin.shape, lambda i: (0, 0)))
    return pl.pallas_call(
        functools.partial(_combine_kernel, tmc=tmc, final=final), grid=(T // tmc,), in_specs=in_specs,
        out_specs=pl.BlockSpec((tmc, D_MODEL), lambda i: (i, 0)),
        out_shape=jax.ShapeDtypeStruct((T, D_MODEL), F32),
        scratch_shapes=[pltpu.VMEM((TOP_K, tmc, D_MODEL), F32), pltpu.SemaphoreType.DMA(())],
        compiler_params=_cparams(("arbitrary",)), name="moe_combine",
    )(*args)


def _moe(h2, x1, route, counts, wg, wu, wd, gfin, tf):
    T = h2.shape[0]
    bm = MOE_BLOCK_ROWS
    n_blocks = TOP_K * T // bm + N_EXPERTS
    cnt = counts[0, :N_EXPERTS].astype(jnp.int32)
    padded = (cnt + bm - 1) // bm * bm
    ends = jnp.cumsum(padded)
    starts = ends - padded
    experts = route[:, 0:TOP_K].astype(jnp.int32)
    pos = starts[experts] + route[:, 4:4 + TOP_K].astype(jnp.int32)
    block_start = jnp.arange(n_blocks, dtype=jnp.int32) * bm
    block_expert = jnp.minimum(jnp.sum((block_start[:, None] >= ends[None, :]).astype(jnp.int32), axis=1),
                               N_EXPERTS - 1)
    xs = _dispatch(h2, pos, n_blocks * bm)
    y = _gmm(xs, block_expert, wg, wu, wd, tf)
    return _combine(pos, route, x1, y, gfin)


def _pad_heads(w, heads, dim, at=0):
    k = w.shape[0]
    w = w.reshape(k, heads, dim)
    w = jnp.pad(w, ((0, 0), (0, 0), (at, MLA_HEAD_PAD - at - dim)))
    return w.reshape(k, heads * MLA_HEAD_PAD)


def _block_diag(w):
    nb, bs, _ = w.shape
    eye = jnp.eye(nb, dtype=w.dtype)
    return (eye[:, None, :, None] * w[:, :, None, :]).reshape(nb * bs, nb * bs)


def _rope_tables(S):
    def cs(dim):
        half = dim // 2
        freqs = jnp.power(jnp.float32(ROPE_THETA), -jnp.arange(half, dtype=F32) * 2.0 / dim)
        ang = jnp.arange(S, dtype=F32)[:, None] * freqs[None, :]
        c, s = jnp.cos(ang), jnp.sin(ang)
        return jnp.concatenate([c, c], -1), jnp.concatenate([-s, s], -1)

    cd, sd = cs(DIL_HEAD_DIM)
    cm, sm = cs(MLA_ROPE)
    ones = jnp.ones((S, MLA_NOPE), F32)
    pad = MLA_HEAD_PAD - MLA_NOPE - MLA_ROPE
    return dict(
        cosd=jnp.tile(cd, (1, DIL_HEADS)), sind=jnp.tile(sd, (1, DIL_HEADS)),
        cosm=jnp.concatenate([ones, cm, jnp.ones((S, pad), F32)], -1),
        sinm=jnp.concatenate([0.0 * ones, sm, jnp.zeros((S, pad), F32)], -1))


def _prep_layer(l, p):
    w_in = p['w_in'][l]
    offs = [0, 256, 512, 768, 1024, 1280, 1536, 1664, 1696]
    xr, yr, qd, kd, vd, cq, ckv, kr = [w_in[:, a:b] for a, b in zip(offs[:-1], offs[1:])]
    kr_pad = _pad_heads(kr, 1, MLA_ROPE, at=MLA_NOPE)
    w1 = jnp.concatenate([xr, yr, qd, kd, vd, cq, ckv, kr_pad], axis=1)
    w_uq2 = _pad_heads(p['mla_w_uq'][l], MLA_HEADS, MLA_NOPE + MLA_ROPE)

    w_ukv = p['mla_w_ukv'][l].reshape(MLA_KV_RANK, MLA_HEADS, MLA_NOPE + MLA_V)
    kn = _pad_heads(w_ukv[:, :, :MLA_NOPE].reshape(MLA_KV_RANK, -1), MLA_HEADS, MLA_NOPE)
    vm = w_ukv[:, :, MLA_NOPE:].reshape(MLA_KV_RANK, MLA_WIDTH)
    w_ukv2 = jnp.concatenate([kn, vm], axis=1)

    rg_w = jnp.stack([jnp.concatenate([_block_diag(p['rg_w_a'][l, d]), _block_diag(p['rg_w_i'][l, d])], axis=1)
                      for d in range(2)])
    rg_b = jnp.stack([jnp.concatenate([p['rg_b_a'][l, d], p['rg_b_i'][l, d]])[None, :] for d in range(2)])
    g_out = p['mix_out_norm_g'][l]
    row = lambda v: v.reshape(1, -1).astype(F32)
    return dict(
        g_mix=row(p['norm_mix_g'][l]), w1=w1.astype(BF16), g_q=row(p['mla_q_norm_g'][l]), w_uq=w_uq2.astype(BF16),
        g_kv=row(p['mla_kv_norm_g'][l]), w_ukv=w_ukv2.astype(BF16),
        conv_w=p['conv_w'][l].astype(F32), conv_b=row(p['conv_b'][l]), rg_w=rg_w.astype(BF16), rg_b=rg_b.astype(F32),
        rg_lam=p['rg_lambda'][l].reshape(2, 1, LRU_WIDTH).astype(F32),
        g_l=row(g_out[:LRU_WIDTH]), g_d=row(g_out[LRU_WIDTH:LRU_WIDTH + DIL_WIDTH]),
        g_m=row(g_out[LRU_WIDTH + DIL_WIDTH:]), w_out=p['w_out'][l].astype(BF16), g_ffn=row(p['norm_ffn_g'][l]))


def _mixers(x2d, lw, tabs, B, S, tm, moe):
    xr, gy, q, k, v, qm, km, vm = _in_proj(x2d, lw, tabs, B, S, tm)
    hf, hb = _lru(xr, lw, B, S)
    dil = _dilated(q, k, v, B, S)
    mo = _mla(qm, km, vm, B, S, 256)
    return _mix_out(x2d, hf, hb, gy, dil, mo, lw, B, S, tm, moe)


def _trunk(x, layers, p, depth):
    B, S, _ = x.shape
    tabs = _rope_tables(S)
    x2d = x.reshape(B * S, D_MODEL)
    tm = 512
    for l in range(depth):
        lw = layers[l]
        moe = l % 2 == 1
        last = l == depth - 1
        gfin = p['final_norm_g'].reshape(1, D_MODEL) if last else None
        j = l // 2
        if moe:
            rw = jnp.pad(p['router_w'][j].astype(F32), ((0, 0), (0, LANES - N_EXPERTS)))
            rw_hi = rw.astype(BF16)
            lw = dict(lw, router_w=jnp.stack([rw_hi, (rw - rw_hi.astype(F32)).astype(BF16)]))
            x1, h2, route, counts = _mixers(x2d, lw, tabs, B, S, tm, True)
            x2d = _moe(h2, x1, route, counts, lw['moe_wg'], lw['moe_wu'], lw['moe_wd'], gfin, 1792)
        else:
            x1, h2 = _mixers(x2d, lw, tabs, B, S, tm, False)
            x2d = _ffn(h2, x1, lw['ffn_wg'], lw['ffn_wu'], lw['ffn_wd'], gfin, tm, lw['ffn_wg'].shape[1])
    return x2d.reshape(B, S, D_MODEL)


def kernel(x_prompt, x_sample, norm_mix_g, w_in, conv_w, conv_b, rg_w_a, rg_b_a, rg_w_i, rg_b_i, rg_lambda, mla_q_norm_g, mla_w_uq, mla_kv_norm_g, mla_w_ukv, mix_out_norm_g, w_out, norm_ffn_g, ffn_w_gate, ffn_w_up, ffn_w_down, router_w, moe_w_gate, moe_w_up, moe_w_down, final_norm_g):
    p = dict(norm_mix_g=norm_mix_g, w_in=w_in, conv_w=conv_w, conv_b=conv_b, rg_w_a=rg_w_a, rg_b_a=rg_b_a,
             rg_w_i=rg_w_i, rg_b_i=rg_b_i, rg_lambda=rg_lambda, mla_q_norm_g=mla_q_norm_g, mla_w_uq=mla_w_uq,
             mla_kv_norm_g=mla_kv_norm_g, mla_w_ukv=mla_w_ukv, mix_out_norm_g=mix_out_norm_g, w_out=w_out,
             norm_ffn_g=norm_ffn_g, router_w=router_w, final_norm_g=final_norm_g)
    depth = w_in.shape[0]
    layers = []
    for l in range(depth):
        lw = _prep_layer(l, p)
        j = l // 2
        if l % 2 == 1:
            lw.update(moe_wg=moe_w_gate[j].astype(BF16), moe_wu=moe_w_up[j].astype(BF16),
                      moe_wd=moe_w_down[j].astype(BF16))
        else:
            lw.update(ffn_wg=ffn_w_gate[j].astype(BF16), ffn_wu=ffn_w_up[j].astype(BF16),
                      ffn_wd=ffn_w_down[j].astype(BF16))
        layers.append(lw)
    return (_trunk(x_prompt, layers, p, depth), _trunk(x_sample, layers, p, depth))
```

```python
import functools
import math

import jax
import jax.numpy as jnp
from jax import lax
from jax.experimental import pallas as pl
from jax.experimental.pallas import tpu as pltpu

F32 = jnp.float32
BF16 = jnp.bfloat16

D_MODEL = 1024
NORM_EPS = 1e-6
ROPE_THETA = 10000.0
LRU_WIDTH = 256
LRU_BLOCKS = 4
CONV_WIDTH = 4
LRU_C = 8.0
DIL_HEADS = 4
DIL_HEAD_DIM = 64
DIL_WIDTH = DIL_HEADS * DIL_HEAD_DIM
DIL_PATTERNS = ((128, 1), (512, 4), (2048, 16))
MLA_HEADS = 8
MLA_NOPE = 64
MLA_ROPE = 32
MLA_V = 64
MLA_Q_RANK = 256
MLA_KV_RANK = 128
MLA_WIDTH = MLA_HEADS * MLA_V
N_EXPERTS = 8
TOP_K = 2

LANES = 128
SUBLANES = 8
MLA_HEAD_PAD = LANES
VMEM_LIMIT = 56 * 1024 * 1024
LOG2E = math.log2(math.e)

_C_XR, _C_YR, _C_Q, _C_K, _C_V, _C_CQ, _C_CKV, _C_KR, _C_END = (0, 256, 512, 768, 1024, 1280, 1536, 1664, 1792)


def _cparams(sem):
    return pltpu.CompilerParams(dimension_semantics=sem, vmem_limit_bytes=VMEM_LIMIT)


def _rms(x, g, n):
    ms = jnp.sum(x * x, axis=-1, keepdims=True) * (1.0 / n)
    return x * lax.rsqrt(ms + NORM_EPS) * g


def _gelu_tanh(x):
    return 0.5 * x * (1.0 + jnp.tanh(0.7978845608028654 * (x + 0.044715 * (x * x * x))))


def _dot(a, b):
    return jnp.dot(a, b, preferred_element_type=F32)


def _dot_nt(a, b):
    return lax.dot_general(a, b, (((1,), (1,)), ((), ())), preferred_element_type=F32)


def _lane_groups(width):
    return [slice(c * LANES, (c + 1) * LANES) for c in range(width // LANES)]


def _swap_rotary_halves(x, half, period, start):
    lane = lax.broadcasted_iota(jnp.int32, (1, LANES), 1) % period
    first = lane < start + half
    outs = []
    for cs in _lane_groups(x.shape[1]):
        xs = x[:, cs]
        outs.append(jnp.where(first, pltpu.roll(xs, LANES - half, 1), pltpu.roll(xs, half, 1)))
    return outs[0] if len(outs) == 1 else jnp.concatenate(outs, axis=-1)


def _in_proj_kernel(x_ref, g_ref, w1_ref, gq_ref, wuq_ref, gkv_ref, wukv_ref,
                    cosd_ref, sind_ref, cosm_ref, sinm_ref,
                    xr_ref, gy_ref, q_ref, k_ref, v_ref, qm_ref, km_ref, vm_ref):
    h = _rms(x_ref[...], g_ref[...], D_MODEL).astype(BF16)
    p = _dot(h, w1_ref[...])
    xr_ref[...] = p[:, _C_XR:_C_YR]
    gy_ref[...] = _gelu_tanh(p[:, _C_YR:_C_Q]).astype(BF16)
    cosd, sind = cosd_ref[...], sind_ref[...]
    half_d = DIL_HEAD_DIM // 2
    q, k = p[:, _C_Q:_C_K], p[:, _C_K:_C_V]
    q = q * cosd + _swap_rotary_halves(q, half_d, DIL_HEAD_DIM, 0) * sind
    q_ref[...] = (q * (DIL_HEAD_DIM ** -0.5 * LOG2E)).astype(BF16)
    k_ref[...] = (k * cosd + _swap_rotary_halves(k, half_d, DIL_HEAD_DIM, 0) * sind).astype(BF16)
    v_ref[...] = p[:, _C_V:_C_CQ].astype(BF16)

    cosm, sinm = cosm_ref[...], sinm_ref[...]
    half_m = MLA_ROPE // 2
    cqn = _rms(p[:, _C_CQ:_C_CKV], gq_ref[...], MLA_Q_RANK).astype(BF16)
    qq = _dot(cqn, wuq_ref[...])
    qq_sw = _swap_rotary_halves(qq, half_m, MLA_HEAD_PAD, MLA_NOPE)
    ckvn = _rms(p[:, _C_CKV:_C_KR], gkv_ref[...], MLA_KV_RANK).astype(BF16)
    kv = _dot(ckvn, wukv_ref[...])
    kr = p[:, _C_KR:_C_END]
    kr = kr * cosm + _swap_rotary_halves(kr, half_m, MLA_HEAD_PAD, MLA_NOPE) * sinm
    scale = (MLA_NOPE + MLA_ROPE) ** -0.5 * LOG2E
    nq = MLA_HEADS * MLA_HEAD_PAD
    for hh in range(MLA_HEADS):
        lo, hi = hh * MLA_HEAD_PAD, (hh + 1) * MLA_HEAD_PAD
        qh = qq[:, lo:hi] * cosm + qq_sw[:, lo:hi] * sinm
        qm_ref[:, lo:hi] = (qh * scale).astype(BF16)
        km_ref[:, lo:hi] = (kv[:, lo:hi] + kr).astype(BF16)
    vm_ref[...] = kv[:, nq:].astype(BF16)


def _in_proj(x2d, lw, tabs, B, S, tm):
    T = B * S
    ns = S // tm
    tok = lambda b, s: (b * ns + s, 0)
    pos = lambda b, s: (s, 0)
    const = lambda b, s: (0, 0)

    def full(a):
        return pl.BlockSpec(a.shape, const)

    def tokspec(c):
        return pl.BlockSpec((tm, c), tok)

    out_shape = (
        jax.ShapeDtypeStruct((T, LRU_WIDTH), F32),
        jax.ShapeDtypeStruct((T, LRU_WIDTH), BF16),
        jax.ShapeDtypeStruct((T, DIL_WIDTH), BF16),
        jax.ShapeDtypeStruct((T, DIL_WIDTH), BF16),
        jax.ShapeDtypeStruct((T, DIL_WIDTH), BF16),
        jax.ShapeDtypeStruct((T, MLA_HEADS * MLA_HEAD_PAD), BF16),
        jax.ShapeDtypeStruct((T, MLA_HEADS * MLA_HEAD_PAD), BF16),
        jax.ShapeDtypeStruct((T, MLA_WIDTH), BF16),
    )
    out_specs = (
        tokspec(LRU_WIDTH), tokspec(LRU_WIDTH), tokspec(DIL_WIDTH), tokspec(DIL_WIDTH), tokspec(DIL_WIDTH),
        tokspec(MLA_HEADS * MLA_HEAD_PAD), tokspec(MLA_HEADS * MLA_HEAD_PAD), tokspec(MLA_WIDTH),
    )
    in_specs = [
        tokspec(D_MODEL), full(lw['g_mix']), full(lw['w1']), full(lw['g_q']), full(lw['w_uq']),
        full(lw['g_kv']), full(lw['w_ukv']),
        pl.BlockSpec((tm, DIL_WIDTH), pos), pl.BlockSpec((tm, DIL_WIDTH), pos),
        pl.BlockSpec((tm, LANES), pos), pl.BlockSpec((tm, LANES), pos),
    ]
    return pl.pallas_call(
        _in_proj_kernel, grid=(B, ns), in_specs=in_specs, out_specs=out_specs, out_shape=out_shape,
        compiler_params=_cparams(("parallel", "parallel")), name="in_proj",
    )(x2d, lw['g_mix'], lw['w1'], lw['g_q'], lw['w_uq'], lw['g_kv'], lw['w_ukv'],
      tabs['cosd'], tabs['sind'], tabs['cosm'], tabs['sinm'])


LRU_CHUNK_ROWS = 1024
LRU_HALO = SUBLANES


def _lru_kernel(xf_ref, xfp_ref, xfn_ref, xb_ref, xbp_ref, xbn_ref, cw_ref, cb_ref, wg_ref, bg_ref,
                lam_ref, hf_ref, hb_ref, xpad, a_f, b_f, a_b, b_b, hs_f, hs_b, hcar, *, B, tc):
    i = pl.program_id(0)
    n = pl.num_programs(0)
    R = tc * B
    left = CONV_WIDTH // 2
    groups = _lane_groups(LRU_WIDTH)

    @pl.when(i == 0)
    def _():
        hcar[...] = jnp.zeros_like(hcar)

    def prep(x_ref, xp_ref, xn_ref, ci, d, a_s, b_s):
        for c, cs in enumerate(groups):
            for b in range(B):
                xpad[c, pl.ds(b, left, stride=B), :] = jnp.where(ci > 0, xp_ref[b, LRU_HALO - left:LRU_HALO, cs], 0.0)
                xpad[c, pl.ds(left * B + b, tc, stride=B), :] = x_ref[b, :, cs]
                xpad[c, pl.ds((left + tc) * B + b, 1), :] = jnp.where(ci < n - 1, xn_ref[b, 0:1, cs], 0.0)
        halves = []
        for c, cs in enumerate(groups):
            xc = cb_ref[:, cs] + xpad[c, 0:R, :] * cw_ref[0:1, cs]
            for kk in range(1, CONV_WIDTH):
                xc = xc + xpad[c, kk * B:kk * B + R, :] * cw_ref[kk:kk + 1, cs]
            halves.append(xc)
        xc = jnp.concatenate(halves, axis=-1)
        g = _dot(xc.astype(BF16), wg_ref[d]) + bg_ref[d]
        r = jax.nn.sigmoid(g[:, :LRU_WIDTH])
        ig = jax.nn.sigmoid(g[:, LRU_WIDTH:])
        lam = lam_ref[d]
        softplus = jnp.maximum(-lam, 0.0) + jnp.log(1.0 + jnp.exp(-jnp.abs(lam)))
        log_a = (-LRU_C) * r * softplus
        a_s[...] = jnp.exp(log_a)
        b_s[...] = jnp.sqrt(1.0 - jnp.exp(2.0 * log_a)) * (ig * xc)

    prep(xf_ref, xfp_ref, xfn_ref, i, 0, a_f, b_f)
    prep(xb_ref, xbp_ref, xbn_ref, n - 1 - i, 1, a_b, b_b)

    def body(s, carry):
        hf, hb = carry
        rf = pl.multiple_of(s * B, B)
        hf = a_f[pl.ds(rf, B), :] * hf + b_f[pl.ds(rf, B), :]
        rb = pl.multiple_of((tc - 1 - s) * B, B)
        hb = a_b[pl.ds(rb, B), :] * hb + b_b[pl.ds(rb, B), :]
        for c, cs in enumerate(groups):
            hs_f[c, pl.ds(rf, B), :] = hf[:, cs]
            hs_b[c, pl.ds(rb, B), :] = hb[:, cs]
        return hf, hb

    hf, hb = lax.fori_loop(0, tc, body, (hcar[0], hcar[1]), unroll=8)
    hcar[0] = hf
    hcar[1] = hb
    for c, cs in enumerate(groups):
        for b in range(B):
            hf_ref[b, :, cs] = hs_f[c, pl.ds(b, tc, stride=B), :]
            hb_ref[b, :, cs] = hs_b[c, pl.ds(b, tc, stride=B), :]


def _lru(xr, lw, B, S):
    tc = LRU_CHUNK_ROWS // B
    n = S // tc
    nh = S // LRU_HALO
    hb_per = tc // LRU_HALO
    x3 = xr.reshape(B, S, LRU_WIDTH)
    W = LRU_WIDTH

    main_f = pl.BlockSpec((B, tc, W), lambda i: (0, i, 0))
    prev_f = pl.BlockSpec((B, LRU_HALO, W), lambda i: (0, jnp.maximum(i * hb_per - 1, 0), 0))
    next_f = pl.BlockSpec((B, LRU_HALO, W), lambda i: (0, jnp.minimum((i + 1) * hb_per, nh - 1), 0))
    main_b = pl.BlockSpec((B, tc, W), lambda i: (0, n - 1 - i, 0))
    prev_b = pl.BlockSpec((B, LRU_HALO, W), lambda i: (0, jnp.maximum((n - 1 - i) * hb_per - 1, 0), 0))
    next_b = pl.BlockSpec((B, LRU_HALO, W), lambda i: (0, jnp.minimum((n - i) * hb_per, nh - 1), 0))

    def full(a):
        nd = a.ndim
        return pl.BlockSpec(a.shape, lambda i: (0,) * nd)

    R = tc * B
    ng = W // LANES
    hf, hb = pl.pallas_call(
        functools.partial(_lru_kernel, B=B, tc=tc),
        grid=(n,),
        in_specs=[main_f, prev_f, next_f, main_b, prev_b, next_b,
                  full(lw['conv_w']), full(lw['conv_b']), full(lw['rg_w']), full(lw['rg_b']), full(lw['rg_lam'])],
        out_specs=(main_f, main_b),
        out_shape=(jax.ShapeDtypeStruct((B, S, W), F32),) * 2,
        scratch_shapes=[pltpu.VMEM((ng, R + (CONV_WIDTH - 1) * B, LANES), F32)] + [pltpu.VMEM((R, W), F32)] * 4
        + [pltpu.VMEM((ng, R, LANES), F32)] * 2 + [pltpu.VMEM((2, B, W), F32)],
        compiler_params=_cparams(("arbitrary",)), name="rglru",
    )(x3, x3, x3, x3, x3, x3, lw['conv_w'], lw['conv_b'], lw['rg_w'], lw['rg_b'], lw['rg_lam'])
    return hf.reshape(B * S, W), hb.reshape(B * S, W)


BAND_HALF = 64
DIL_TILE = 1024
DIL_QBLOCK = 256


def _dil_kernel(q_ref, kp_ref, k_ref, kn_ref, vp_ref, v_ref, vn_ref, o_ref,
                qs, kcat, vcat, acc_m, acc_l, acc_o, *, S):
    TT = DIL_TILE
    tile_start = pl.program_id(1) * TT
    groups = _lane_groups(DIL_WIDTH)
    for c, cs in enumerate(groups):
        qs[c] = q_ref[0, :, cs].astype(F32)
        for j, (kr, vr) in enumerate(((kp_ref, vp_ref), (k_ref, v_ref), (kn_ref, vn_ref))):
            kcat[c, j * TT:(j + 1) * TT, :] = kr[0, :, cs].astype(F32)
            vcat[c, j * TT:(j + 1) * TT, :] = vr[0, :, cs].astype(F32)
    head = lax.broadcasted_iota(jnp.int32, (1, DIL_WIDTH), 1) // DIL_HEAD_DIM

    order = sorted((dil for _, dil in DIL_PATTERNS), reverse=True)
    for dil in order:
        first = dil == order[0]
        nq = min(DIL_QBLOCK, TT // dil)
        nsub = TT // dil // nq
        nk = nq + 2 * BAND_HALF
        row = lax.broadcasted_iota(jnp.int32, (nq, nk), 0)
        col = lax.broadcasted_iota(jnp.int32, (nq, nk), 1)
        band = jnp.abs(col - BAND_HALF - row) <= BAND_HALF

        def rows(start, n, dil=dil):
            return pl.ds(pl.multiple_of(start, SUBLANES), n) if dil == 1 else pl.ds(start, n, stride=dil)

        def ld(ref, idx):
            return jnp.concatenate([ref[c, idx, :] for c in range(len(groups))], axis=-1)

        def st(ref, idx, val):
            for c, cs in enumerate(groups):
                ref[c, idx, :] = val[:, cs]

        def block(it, carry, dil=dil, nq=nq, nsub=nsub, nk=nk, band=band, col=col, rows=rows, first=first):
            r, j = (0, it) if dil == 1 else (it, 0)
            off = j * nq * dil + r
            qi = rows(off, nq)
            ki = rows(off + TT - BAND_HALF * dil, nk)
            kpos = tile_start + off - BAND_HALF * dil + col * dil
            valid = band & (kpos >= 0) & (kpos < S)
            q = ld(qs, qi).astype(BF16)
            k = ld(kcat, ki).astype(BF16)
            v = ld(vcat, ki).astype(BF16)
            m_b = jnp.zeros((nq, DIL_WIDTH), F32)
            l_b = jnp.zeros((nq, DIL_WIDTH), F32)
            o_b = jnp.zeros((nq, DIL_WIDTH), F32)
            per = DIL_HEADS if nq * DIL_HEADS <= DIL_QBLOCK else 1
            groups_h = [range(h0, h0 + per) for h0 in range(0, DIL_HEADS, per)]
            valid_g = jnp.concatenate([valid] * per, axis=0)
            scores = [_dot_nt(jnp.concatenate([jnp.where(head == h, q, jnp.zeros_like(q)) for h in hs], axis=0), k)
                      for hs in groups_h]
            stats = []
            for s in scores:
                s = jnp.where(valid_g, s, -1e30)
                m = jnp.max(s, axis=-1, keepdims=True)
                e = jnp.exp2(s - m)
                stats.append((m, jnp.sum(e, axis=-1, keepdims=True), e.astype(BF16)))
            for hs, (m, l, e) in zip(groups_h, stats):
                o = _dot(e, v)
                for n_, h in enumerate(hs):
                    hm = head == h
                    sl = slice(n_ * nq, (n_ + 1) * nq)
                    m_b = jnp.where(hm, m[sl], m_b)
                    l_b = jnp.where(hm, l[sl], l_b)
                    o_b = jnp.where(hm, o[sl], o_b)
            if first:
                st(acc_l, qi, l_b)
                st(acc_o, qi, o_b)
                st(acc_m, qi, m_b)
                return carry
            m_old = ld(acc_m, qi)
            m_new = jnp.maximum(m_old, m_b)
            w_old = jnp.exp2(m_old - m_new)
            w_b = jnp.exp2(m_b - m_new)
            st(acc_l, qi, w_old * ld(acc_l, qi) + w_b * l_b)
            st(acc_o, qi, w_old * ld(acc_o, qi) + w_b * o_b)
            st(acc_m, qi, m_new)
            return carry

        lax.fori_loop(0, dil * nsub, block, 0, unroll=4 if nq < DIL_QBLOCK else 1)

    for c, cs in enumerate(groups):
        o_ref[0, :, cs] = (acc_o[c] / acc_l[c]).astype(BF16)


def _dilated(q, k, v, B, S):
    TT = DIL_TILE
    nt = S // TT
    W = DIL_WIDTH
    ng = W // LANES
    q3, k3, v3 = (a.reshape(B, S, W) for a in (q, k, v))
    own = pl.BlockSpec((1, TT, W), lambda b, i: (b, i, 0))
    prev = pl.BlockSpec((1, TT, W), lambda b, i: (b, jnp.maximum(i - 1, 0), 0))
    nxt = pl.BlockSpec((1, TT, W), lambda b, i: (b, jnp.minimum(i + 1, nt - 1), 0))
    o = pl.pallas_call(
        functools.partial(_dil_kernel, S=S), grid=(B, nt),
        in_specs=[own, prev, own, nxt, prev, own, nxt], out_specs=own,
        out_shape=jax.ShapeDtypeStruct((B, S, W), BF16),
        scratch_shapes=[pltpu.VMEM((ng, TT, LANES), F32), pltpu.VMEM((ng, 3 * TT, LANES), F32),
                        pltpu.VMEM((ng, 3 * TT, LANES), F32)] + [pltpu.VMEM((ng, TT, LANES), F32)] * 3,
        compiler_params=_cparams(("parallel", "parallel")), name="dilated_attn",
    )(q3, k3, k3, k3, v3, v3, v3)
    return o.reshape(B * S, W)


MLA_KEY_CHUNK = 256


def _mla_kernel(q_ref, k_ref, v_ref, o_ref, vt):
    S = k_ref.shape[1]
    tq = q_ref.shape[1]
    ck = MLA_KEY_CHUNK
    chunks = [slice(c * ck, (c + 1) * ck) for c in range(S // ck)]

    @pl.when(pl.program_id(1) == 0)
    def _():
        vt[...] = v_ref[0].T

    def fold(x, op):
        return op(x.reshape(ck // SUBLANES, SUBLANES, tq), axis=0)

    def score_pass(h):
        lo, hi = h * MLA_HEAD_PAD, (h + 1) * MLA_HEAD_PAD
        q_h = q_ref[0, :, lo:hi]
        m8 = None
        parts = []
        for cs in chunks:
            s_c = _dot_nt(k_ref[0, cs, lo:hi], q_h)
            parts.append(s_c)
            m8 = fold(s_c, jnp.max) if m8 is None else jnp.maximum(m8, fold(s_c, jnp.max))
        return parts, jnp.max(m8, axis=0, keepdims=True)

    def value_pass(h, sm):
        parts, m = sm
        acc = jnp.zeros((MLA_V, tq), F32)
        l8 = jnp.zeros((SUBLANES, tq), F32)
        for cs, s_c in zip(chunks, parts):
            e = jnp.exp2(s_c - m)
            l8 = l8 + fold(e, jnp.sum)
            acc = acc + _dot(vt[h * MLA_V:(h + 1) * MLA_V, cs], e.astype(BF16))
        return acc * (1.0 / jnp.sum(l8, axis=0, keepdims=True))

    m_next = score_pass(0)
    outs = []
    for h in range(MLA_HEADS):
        m = m_next
        if h + 1 < MLA_HEADS:
            m_next = score_pass(h + 1)
        outs.append(value_pass(h, m))
        if h % 2 == 1:
            pair = h // 2
            o_ref[0, :, pair * LANES:(pair + 1) * LANES] = jnp.concatenate(outs[h - 1:h + 1], axis=0).T.astype(BF16)


def _mla(qm, km, vm, B, S, tq):
    QW = MLA_HEADS * MLA_HEAD_PAD
    q3, k3, v3 = qm.reshape(B, S, QW), km.reshape(B, S, QW), vm.reshape(B, S, MLA_WIDTH)
    o = pl.pallas_call(
        _mla_kernel, grid=(B, S // tq),
        in_specs=[pl.BlockSpec((1, tq, QW), lambda b, i: (b, i, 0)),
                  pl.BlockSpec((1, S, QW), lambda b, i: (b, 0, 0)),
                  pl.BlockSpec((1, S, MLA_WIDTH), lambda b, i: (b, 0, 0))],
        out_specs=pl.BlockSpec((1, tq, MLA_WIDTH), lambda b, i: (b, i, 0)),
        out_shape=jax.ShapeDtypeStruct((B, S, MLA_WIDTH), BF16),
        scratch_shapes=[pltpu.VMEM((MLA_WIDTH, S), BF16)],
        compiler_params=_cparams(("parallel", "arbitrary")), name="mla_attn",
    )(q3, k3, v3)
    return o.reshape(B * S, MLA_WIDTH)


def _mix_out_kernel(*refs, moe):
    (x_ref, hf_ref, hb_ref, gy_ref, dil_ref, mo_ref, gl_ref, gd_ref, gm_ref, wout_ref, gffn_ref) = refs[:11]
    if moe:
        rw_ref, x1_ref, h2_ref, route_ref, cnt_ref, carry = refs[11:]
    else:
        x1_ref, h2_ref = refs[11:]
    lru = (hf_ref[...] + hb_ref[...]) * gy_ref[...].astype(F32)
    mix = jnp.concatenate([
        _rms(lru, gl_ref[...], LRU_WIDTH), _rms(dil_ref[...].astype(F32), gd_ref[...], DIL_WIDTH),
        _rms(mo_ref[...].astype(F32), gm_ref[...], MLA_WIDTH)], axis=-1).astype(BF16)
    x1 = x_ref[...] + _dot(mix, wout_ref[...])
    x1_ref[...] = x1
    h2 = _rms(x1, gffn_ref[...], D_MODEL)
    h2_ref[...] = h2.astype(h2_ref.dtype)
    if moe:
        first = (pl.program_id(0) == 0) & (pl.program_id(1) == 0)

        @pl.when(first)
        def _():
            carry[...] = jnp.zeros_like(carry)

        h_hi = h2.astype(BF16)
        h_lo = (h2 - h_hi.astype(F32)).astype(BF16)
        logits = _dot(h_hi, rw_ref[0]) + (_dot(h_hi, rw_ref[1]) + _dot(h_lo, rw_ref[0]))
        tm = logits.shape[0]
        lane = lax.broadcasted_iota(jnp.int32, logits.shape, 1)
        neg = -jnp.inf
        lg = jnp.where(lane < N_EXPERTS, logits, neg)
        m1 = jnp.max(lg, axis=-1, keepdims=True)
        i1 = jnp.min(jnp.where(lg == m1, lane, LANES), axis=-1, keepdims=True)
        lg2 = jnp.where(lane == i1, neg, lg)
        m2 = jnp.max(lg2, axis=-1, keepdims=True)
        i2 = jnp.min(jnp.where(lg2 == m2, lane, LANES), axis=-1, keepdims=True)
        e2 = jnp.exp(m2 - m1)
        den = 1.0 + e2
        sel1, sel2 = lane == i1, lane == i2
        onehot = jnp.where(sel1 | sel2, 1.0, 0.0)
        tri = (lax.broadcasted_iota(jnp.int32, (tm, tm), 0) > lax.broadcasted_iota(jnp.int32, (tm, tm), 1))
        before = _dot(jnp.where(tri, 1.0, 0.0).astype(BF16), onehot.astype(BF16)) + carry[0:1, :]
        rank1 = jnp.sum(jnp.where(sel1, before, 0.0), axis=-1, keepdims=True)
        rank2 = jnp.sum(jnp.where(sel2, before, 0.0), axis=-1, keepdims=True)
        carry[...] = carry[...] + jnp.sum(onehot, axis=0, keepdims=True)
        cnt_ref[...] = carry[...]
        cols = (i1.astype(F32), i2.astype(F32), 1.0 / den, e2 / den, rank1, rank2)
        route = jnp.zeros(logits.shape, F32)
        for c, val in enumerate(cols):
            route = jnp.where(lane == c, val, route)
        route_ref[...] = route


def _mix_out(x2d, hf, hb, gy, dil, mo, lw, B, S, tm, moe):
    T = B * S
    ns = S // tm
    tok = lambda b, s: (b * ns + s, 0)
    const = lambda b, s: (0, 0)
    tokspec = lambda c: pl.BlockSpec((tm, c), tok)
    full = lambda a: pl.BlockSpec(a.shape, const)
    args = [x2d, hf, hb, gy, dil, mo, lw['g_l'], lw['g_d'], lw['g_m'], lw['w_out'], lw['g_ffn']]
    in_specs = [tokspec(D_MODEL), tokspec(LRU_WIDTH), tokspec(LRU_WIDTH), tokspec(LRU_WIDTH), tokspec(DIL_WIDTH),
                tokspec(MLA_WIDTH), full(lw['g_l']), full(lw['g_d']), full(lw['g_m']), full(lw['w_out']),
                full(lw['g_ffn'])]
    out_shape = [jax.ShapeDtypeStruct((T, D_MODEL), F32), jax.ShapeDtypeStruct((T, D_MODEL), F32 if moe else BF16)]
    out_specs = [tokspec(D_MODEL), tokspec(D_MODEL)]
    scratch = []
    if moe:
        args.append(lw['router_w'])
        in_specs.append(pl.BlockSpec(lw['router_w'].shape, lambda b, s: (0, 0, 0)))
        out_shape += [jax.ShapeDtypeStruct((T, LANES), F32), jax.ShapeDtypeStruct((SUBLANES, LANES), F32)]
        out_specs += [tokspec(LANES), pl.BlockSpec((SUBLANES, LANES), const)]
        scratch = [pltpu.VMEM((SUBLANES, LANES), F32)]
    sem = ("arbitrary", "arbitrary") if moe else ("parallel", "parallel")
    return pl.pallas_call(
        functools.partial(_mix_out_kernel, moe=moe), grid=(B, ns), in_specs=in_specs,
        out_specs=tuple(out_specs), out_shape=tuple(out_shape), scratch_shapes=scratch,
        compiler_params=_cparams(sem), name="mix_out",
    )(*args)


def _ffn_kernel(*refs, final):
    if final:
        h_ref, x1_ref, wg_ref, wu_ref, wd_ref, gfin_ref, o_ref = refs
    else:
        h_ref, x1_ref, wg_ref, wu_ref, wd_ref, o_ref = refs
    f = pl.program_id(1)

    @pl.when(f == 0)
    def _():
        o_ref[...] = x1_ref[...]

    h = h_ref[...]
    g = _dot(h, wg_ref[...])
    u = _dot(h, wu_ref[...])
    a = (g * jax.nn.sigmoid(g) * u).astype(BF16)
    o_ref[...] += _dot(a, wd_ref[...])

    if final:
        @pl.when(f == pl.num_programs(1) - 1)
        def _():
            o_ref[...] = _rms(o_ref[...], gfin_ref[...], D_MODEL)


def _ffn(h2, x1, wg, wu, wd, gfin, tm, tf):
    T = h2.shape[0]
    F = wg.shape[1]
    final = gfin is not None
    args = [h2, x1, wg, wu, wd] + ([gfin] if final else [])
    wmode = dict(pipeline_mode=pl.Buffered(1)) if tf == F else {}
    in_specs = [pl.BlockSpec((tm, D_MODEL), lambda i, f: (i, 0)), pl.BlockSpec((tm, D_MODEL), lambda i, f: (i, 0)),
                pl.BlockSpec((D_MODEL, tf), lambda i, f: (0, f), **wmode),
                pl.BlockSpec((D_MODEL, tf), lambda i, f: (0, f), **wmode),
                pl.BlockSpec((tf, D_MODEL), lambda i, f: (f, 0), **wmode)]
    if final:
        in_specs.append(pl.BlockSpec(gfin.shape, lambda i, f: (0, 0)))
    return pl.pallas_call(
        functools.partial(_ffn_kernel, final=final), grid=(T // tm, F // tf), in_specs=in_specs,
        out_specs=pl.BlockSpec((tm, D_MODEL), lambda i, f: (i, 0)),
        out_shape=jax.ShapeDtypeStruct((T, D_MODEL), F32),
        compiler_params=_cparams(("parallel", "arbitrary")), name="ffn",
    )(*args)


MOE_BLOCK_ROWS = 512
DISPATCH_TOKENS = 1024
COMBINE_TOKENS = 512


def _dispatch_kernel(pos_ref, h_ref, xs_in, xs_out, sem, *, tmd):
    del xs_in

    def body(t, c):
        src = h_ref.at[pl.ds(t, 1), :]
        for k in range(TOP_K):
            pltpu.make_async_copy(src, xs_out.at[pl.ds(pos_ref[0, 0, TOP_K * t + k], 1), :], sem).start()
        return c

    lax.fori_loop(0, tmd, body, 0, unroll=4)
    for _ in range(TOP_K):
        pltpu.make_async_copy(h_ref, xs_out.at[pl.ds(0, tmd), :], sem).wait()


def _dispatch(h2, pos, n_rows):
    T = h2.shape[0]
    tmd = DISPATCH_TOKENS
    pos3 = pos.reshape(T // tmd, 1, TOP_K * tmd)
    xs0 = jnp.zeros((n_rows, D_MODEL), F32)
    return pl.pallas_call(
        functools.partial(_dispatch_kernel, tmd=tmd), grid=(T // tmd,),
        in_specs=[pl.BlockSpec((1, 1, TOP_K * tmd), lambda i: (i, 0, 0), memory_space=pltpu.SMEM),
                  pl.BlockSpec((tmd, D_MODEL), lambda i: (i, 0)), pl.BlockSpec(memory_space=pl.ANY)],
        out_specs=pl.BlockSpec(memory_space=pl.ANY),
        out_shape=jax.ShapeDtypeStruct((n_rows, D_MODEL), F32),
        scratch_shapes=[pltpu.SemaphoreType.DMA(())],
        input_output_aliases={2: 0},
        compiler_params=_cparams(("arbitrary",)), name="moe_dispatch",
    )(pos3, h2, xs0)


def _gmm_kernel(be_ref, xs_ref, wg_ref, wu_ref, wd_ref, y_ref, xb):
    del be_ref
    f = pl.program_id(1)

    @pl.when(f == 0)
    def _():
        xb[...] = xs_ref[...].astype(BF16)

    h = xb[...]
    g = _dot(h, wg_ref[0])
    u = _dot(h, wu_ref[0])
    part = _dot((g * jax.nn.sigmoid(g) * u).astype(BF16), wd_ref[0])

    @pl.when(f == 0)
    def _():
        y_ref[...] = part

    @pl.when(f > 0)
    def _():
        y_ref[...] += part


def _gmm(xs, block_expert, wg, wu, wd, tf):
    P = xs.shape[0]
    bm = MOE_BLOCK_ROWS
    F = wg.shape[2]
    grid_spec = pltpu.PrefetchScalarGridSpec(
        num_scalar_prefetch=1, grid=(P // bm, F // tf),
        in_specs=[pl.BlockSpec((bm, D_MODEL), lambda j, f, be: (j, 0)),
                  pl.BlockSpec((1, D_MODEL, tf), lambda j, f, be: (be[j], 0, f)),
                  pl.BlockSpec((1, D_MODEL, tf), lambda j, f, be: (be[j], 0, f)),
                  pl.BlockSpec((1, tf, D_MODEL), lambda j, f, be: (be[j], f, 0))],
        out_specs=pl.BlockSpec((bm, D_MODEL), lambda j, f, be: (j, 0)),
        scratch_shapes=[pltpu.VMEM((bm, D_MODEL), BF16)])
    return pl.pallas_call(
        _gmm_kernel, grid_spec=grid_spec, out_shape=jax.ShapeDtypeStruct((P, D_MODEL), F32),
        compiler_params=_cparams(("parallel", "arbitrary")), name="moe_gmm",
    )(block_expert, xs, wg, wu, wd)


def _combine_kernel(*refs, tmc, final):
    if final:
        pos_ref, route_ref, x1_ref, y_hbm, gfin_ref, o_ref, ybuf, sem = refs
    else:
        pos_ref, route_ref, x1_ref, y_hbm, o_ref, ybuf, sem = refs

    def body(t, c):
        for k in range(TOP_K):
            pltpu.make_async_copy(y_hbm.at[pl.ds(pos_ref[0, 0, TOP_K * t + k], 1), :],
                                  ybuf.at[k, pl.ds(t, 1), :], sem).start()
        return c

    lax.fori_loop(0, tmc, body, 0, unroll=4)
    for k in range(TOP_K):
        pltpu.make_async_copy(y_hbm.at[pl.ds(0, tmc), :], ybuf.at[k], sem).wait()
    route = route_ref[...]
    out = x1_ref[...] + route[:, 2:3] * ybuf[0] + route[:, 3:4] * ybuf[1]
    if final:
        out = _rms(out, gfin_ref[...], D_MODEL)
    o_ref[...] = out


def _combine(pos, route, x1, y, gfin):
    T = x1.shape[0]
    tmc = COMBINE_TOKENS
    final = gfin is not None
    pos3 = pos.reshape(T // tmc, 1, TOP_K * tmc)
    args = [pos3, route, x1, y] + ([gfin] if final else [])
    in_specs = [pl.BlockSpec((1, 1, TOP_K * tmc), lambda i: (i, 0, 0), memory_space=pltpu.SMEM),
                pl.BlockSpec((tmc, LANES), lambda i: (i, 0)), pl.BlockSpec((tmc, D_MODEL), lambda i: (i, 0)),
                pl.BlockSpec(memory_space=pl.ANY)]
    if final:
        in_specs.append(pl.BlockSpec(gfin.shape, lambda i: (0, 0)))
    return pl.pallas_call(
        functools.partial(_combine_kernel, tmc=tmc, final=final), grid=(T // tmc,), in_specs=in_specs,
        out_specs=pl.BlockSpec((tmc, D_MODEL), lambda i: (i, 0)),
        out_shape=jax.ShapeDtypeStruct((T, D_MODEL), F32),
        scratch_shapes=[pltpu.VMEM((TOP_K, tmc, D_MODEL), F32), pltpu.SemaphoreType.DMA(())],
        compiler_params=_cparams(("arbitrary",)), name="moe_combine",
    )(*args)


def _moe(h2, x1, route, counts, wg, wu, wd, gfin, tf):
    T = h2.shape[0]
    bm = MOE_BLOCK_ROWS
    n_blocks = TOP_K * T // bm + N_EXPERTS
    cnt = counts[0, :N_EXPERTS].astype(jnp.int32)
    padded = (cnt + bm - 1) // bm * bm
    ends = jnp.cumsum(padded)
    starts = ends - padded
    experts = route[:, 0:TOP_K].astype(jnp.int32)
    pos = starts[experts] + route[:, 4:4 + TOP_K].astype(jnp.int32)
    block_start = jnp.arange(n_blocks, dtype=jnp.int32) * bm
    block_expert = jnp.minimum(jnp.sum((block_start[:, None] >= ends[None, :]).astype(jnp.int32), axis=1),
                               N_EXPERTS - 1)
    xs = _dispatch(h2, pos, n_blocks * bm)
    y = _gmm(xs, block_expert, wg, wu, wd, tf)
    return _combine(pos, route, x1, y, gfin)


def _pad_heads(w, heads, dim, at=0):
    k = w.shape[0]
    w = w.reshape(k, heads, dim)
    w = jnp.pad(w, ((0, 0), (0, 0), (at, MLA_HEAD_PAD - at - dim)))
    return w.reshape(k, heads * MLA_HEAD_PAD)


def _block_diag(w):
    nb, bs, _ = w.shape
    eye = jnp.eye(nb, dtype=w.dtype)
    return (eye[:, None, :, None] * w[:, :, None, :]).reshape(nb * bs, nb * bs)


def _rope_tables(S):
    def cs(dim):
        half = dim // 2
        freqs = jnp.power(jnp.float32(ROPE_THETA), -jnp.arange(half, dtype=F32) * 2.0 / dim)
        ang = jnp.arange(S, dtype=F32)[:, None] * freqs[None, :]
        c, s = jnp.cos(ang), jnp.sin(ang)
        return jnp.concatenate([c, c], -1), jnp.concatenate([-s, s], -1)

    cd, sd = cs(DIL_HEAD_DIM)
    cm, sm = cs(MLA_ROPE)
    ones = jnp.ones((S, MLA_NOPE), F32)
    pad = MLA_HEAD_PAD - MLA_NOPE - MLA_ROPE
    return dict(
        cosd=jnp.tile(cd, (1, DIL_HEADS)), sind=jnp.tile(sd, (1, DIL_HEADS)),
        cosm=jnp.concatenate([ones, cm, jnp.ones((S, pad), F32)], -1),
        sinm=jnp.concatenate([0.0 * ones, sm, jnp.zeros((S, pad), F32)], -1))


def _prep_layer(l, p):
    w_in = p['w_in'][l]
    offs = [0, 256, 512, 768, 1024, 1280, 1536, 1664, 1696]
    xr, yr, qd, kd, vd, cq, ckv, kr = [w_in[:, a:b] for a, b in zip(offs[:-1], offs[1:])]
    kr_pad = _pad_heads(kr, 1, MLA_ROPE, at=MLA_NOPE)
    w1 = jnp.concatenate([xr, yr, qd, kd, vd, cq, ckv, kr_pad], axis=1)
    w_uq2 = _pad_heads(p['mla_w_uq'][l], MLA_HEADS, MLA_NOPE + MLA_ROPE)

    w_ukv = p['mla_w_ukv'][l].reshape(MLA_KV_RANK, MLA_HEADS, MLA_NOPE + MLA_V)
    kn = _pad_heads(w_ukv[:, :, :MLA_NOPE].reshape(MLA_KV_RANK, -1), MLA_HEADS, MLA_NOPE)
    vm = w_ukv[:, :, MLA_NOPE:].reshape(MLA_KV_RANK, MLA_WIDTH)
    w_ukv2 = jnp.concatenate([kn, vm], axis=1)

    rg_w = jnp.stack([jnp.concatenate([_block_diag(p['rg_w_a'][l, d]), _block_diag(p['rg_w_i'][l, d])], axis=1)
                      for d in range(2)])
    rg_b = jnp.stack([jnp.concatenate([p['rg_b_a'][l, d], p['rg_b_i'][l, d]])[None, :] for d in range(2)])
    g_out = p['mix_out_norm_g'][l]
    row = lambda v: v.reshape(1, -1).astype(F32)
    return dict(
        g_mix=row(p['norm_mix_g'][l]), w1=w1.astype(BF16), g_q=row(p['mla_q_norm_g'][l]), w_uq=w_uq2.astype(BF16),
        g_kv=row(p['mla_kv_norm_g'][l]), w_ukv=w_ukv2.astype(BF16),
        conv_w=p['conv_w'][l].astype(F32), conv_b=row(p['conv_b'][l]), rg_w=rg_w.astype(BF16), rg_b=rg_b.astype(F32),
        rg_lam=p['rg_lambda'][l].reshape(2, 1, LRU_WIDTH).astype(F32),
        g_l=row(g_out[:LRU_WIDTH]), g_d=row(g_out[LRU_WIDTH:LRU_WIDTH + DIL_WIDTH]),
        g_m=row(g_out[LRU_WIDTH + DIL_WIDTH:]), w_out=p['w_out'][l].astype(BF16), g_ffn=row(p['norm_ffn_g'][l]))


def _mixers(x2d, lw, tabs, B, S, tm, moe):
    xr, gy, q, k, v, qm, km, vm = _in_proj(x2d, lw, tabs, B, S, tm)
    hf, hb = _lru(xr, lw, B, S)
    dil = _dilated(q, k, v, B, S)
    mo = _mla(qm, km, vm, B, S, 256)
    return _mix_out(x2d, hf, hb, gy, dil, mo, lw, B, S, tm, moe)


def _trunk(x, layers, p, depth):
    B, S, _ = x.shape
    tabs = _rope_tables(S)
    x2d = x.reshape(B * S, D_MODEL)
    tm = 512
    for l in range(depth):
        lw = layers[l]
        moe = l % 2 == 1
        last = l == depth - 1
        gfin = p['final_norm_g'].reshape(1, D_MODEL) if last else None
        j = l // 2
        if moe:
            rw = jnp.pad(p['router_w'][j].astype(F32), ((0, 0), (0, LANES - N_EXPERTS)))
            rw_hi = rw.astype(BF16)
            lw = dict(lw, router_w=jnp.stack([rw_hi, (rw - rw_hi.astype(F32)).astype(BF16)]))
            x1, h2, route, counts = _mixers(x2d, lw, tabs, B, S, tm, True)
            x2d = _moe(h2, x1, route, counts, lw['moe_wg'], lw['moe_wu'], lw['moe_wd'], gfin, 1792)
        else:
            x1, h2 = _mixers(x2d, lw, tabs, B, S, tm, False)
            x2d = _ffn(h2, x1, lw['ffn_wg'], lw['ffn_wu'], lw['ffn_wd'], gfin, tm, lw['ffn_wg'].shape[1])
    return x2d.reshape(B, S, D_MODEL)


def kernel(x_prompt, x_sample, norm_mix_g, w_in, conv_w, conv_b, rg_w_a, rg_b_a, rg_w_i, rg_b_i, rg_lambda, mla_q_norm_g, mla_w_uq, mla_kv_norm_g, mla_w_ukv, mix_out_norm_g, w_out, norm_ffn_g, ffn_w_gate, ffn_w_up, ffn_w_down, router_w, moe_w_gate, moe_w_up, moe_w_down, final_norm_g):
    p = dict(norm_mix_g=norm_mix_g, w_in=w_in, conv_w=conv_w, conv_b=conv_b, rg_w_a=rg_w_a, rg_b_a=rg_b_a,
             rg_w_i=rg_w_i, rg_b_i=rg_b_i, rg_lambda=rg_lambda, mla_q_norm_g=mla_q_norm_g, mla_w_uq=mla_w_uq,
             mla_kv_norm_g=mla_kv_norm_g, mla_w_ukv=mla_w_ukv, mix_out_norm_g=mix_out_norm_g, w_out=w_out,
             norm_ffn_g=norm_ffn_g, router_w=router_w, final_norm_g=final_norm_g)
    depth = w_in.shape[0]
    layers = []
    for l in range(depth):
        lw = _prep_layer(l, p)
        j = l // 2
        if l % 2 == 1:
            lw.update(moe_wg=moe_w_gate[j].astype(BF16), moe_wu=moe_w_up[j].astype(BF16),
                      moe_wd=moe_w_down[j].astype(BF16))
        else:
            lw.update(ffn_wg=ffn_w_gate[j].astype(BF16), ffn_wu=ffn_w_up[j].astype(BF16),
                      ffn_wd=ffn_w_down[j].astype(BF16))
        layers.append(lw)
    return (_trunk(x_prompt, layers, p, depth), _trunk(x_sample, layers, p, depth))
```

```python
import functools
import math

import jax
import jax.numpy as jnp
from jax import lax
from jax.experimental import pallas as pl
from jax.experimental.pallas import tpu as pltpu

F32 = jnp.float32
BF16 = jnp.bfloat16

D_MODEL = 1024
NORM_EPS = 1e-6
ROPE_THETA = 10000.0
LRU_WIDTH = 256
LRU_BLOCKS = 4
CONV_WIDTH = 4
LRU_C = 8.0
DIL_HEADS = 4
DIL_HEAD_DIM = 64
DIL_WIDTH = DIL_HEADS * DIL_HEAD_DIM
DIL_PATTERNS = ((128, 1), (512, 4), (2048, 16))
MLA_HEADS = 8
MLA_NOPE = 64
MLA_ROPE = 32
MLA_V = 64
MLA_Q_RANK = 256
MLA_KV_RANK = 128
MLA_WIDTH = MLA_HEADS * MLA_V
N_EXPERTS = 8
TOP_K = 2

LANES = 128
SUBLANES = 8
MLA_HEAD_PAD = LANES
VMEM_LIMIT = 56 * 1024 * 1024
LOG2E = math.log2(math.e)

_C_XR, _C_YR, _C_Q, _C_K, _C_V, _C_CQ, _C_CKV, _C_KR, _C_END = (0, 256, 512, 768, 1024, 1280, 1536, 1664, 1792)


def _cparams(sem):
    return pltpu.CompilerParams(dimension_semantics=sem, vmem_limit_bytes=VMEM_LIMIT)


def _rms(x, g, n):
    ms = jnp.sum(x * x, axis=-1, keepdims=True) * (1.0 / n)
    return x * lax.rsqrt(ms + NORM_EPS) * g


def _gelu_tanh(x):
    return 0.5 * x * (1.0 + jnp.tanh(0.7978845608028654 * (x + 0.044715 * (x * x * x))))


def _dot(a, b):
    return jnp.dot(a, b, preferred_element_type=F32)


def _dot_nt(a, b):
    return lax.dot_general(a, b, (((1,), (1,)), ((), ())), preferred_element_type=F32)


def _lane_groups(width):
    return [slice(c * LANES, (c + 1) * LANES) for c in range(width // LANES)]


def _swap_rotary_halves(x, half, period, start):
    lane = lax.broadcasted_iota(jnp.int32, (1, LANES), 1) % period
    first = lane < start + half
    outs = []
    for cs in _lane_groups(x.shape[1]):
        xs = x[:, cs]
        outs.append(jnp.where(first, pltpu.roll(xs, LANES - half, 1), pltpu.roll(xs, half, 1)))
    return outs[0] if len(outs) == 1 else jnp.concatenate(outs, axis=-1)


def _in_proj_kernel(x_ref, g_ref, w1_ref, gq_ref, wuq_ref, gkv_ref, wukv_ref,
                    cosd_ref, sind_ref, cosm_ref, sinm_ref,
                    xr_ref, gy_ref, q_ref, k_ref, v_ref, qm_ref, km_ref, vm_ref):
    h = _rms(x_ref[...], g_ref[...], D_MODEL).astype(BF16)
    p = _dot(h, w1_ref[...])
    xr_ref[...] = p[:, _C_XR:_C_YR]
    gy_ref[...] = _gelu_tanh(p[:, _C_YR:_C_Q]).astype(BF16)
    cosd, sind = cosd_ref[...], sind_ref[...]
    half_d = DIL_HEAD_DIM // 2
    q, k = p[:, _C_Q:_C_K], p[:, _C_K:_C_V]
    q = q * cosd + _swap_rotary_halves(q, half_d, DIL_HEAD_DIM, 0) * sind
    q_ref[...] = (q * (DIL_HEAD_DIM ** -0.5 * LOG2E)).astype(BF16)
    k_ref[...] = (k * cosd + _swap_rotary_halves(k, half_d, DIL_HEAD_DIM, 0) * sind).astype(BF16)
    v_ref[...] = p[:, _C_V:_C_CQ].astype(BF16)

    cosm, sinm = cosm_ref[...], sinm_ref[...]
    half_m = MLA_ROPE // 2
    cqn = _rms(p[:, _C_CQ:_C_CKV], gq_ref[...], MLA_Q_RANK).astype(BF16)
    qq = _dot(cqn, wuq_ref[...])
    qq_sw = _swap_rotary_halves(qq, half_m, MLA_HEAD_PAD, MLA_NOPE)
    ckvn = _rms(p[:, _C_CKV:_C_KR], gkv_ref[...], MLA_KV_RANK).astype(BF16)
    kv = _dot(ckvn, wukv_ref[...])
    kr = p[:, _C_KR:_C_END]
    kr = kr * cosm + _swap_rotary_halves(kr, half_m, MLA_HEAD_PAD, MLA_NOPE) * sinm
    scale = (MLA_NOPE + MLA_ROPE) ** -0.5 * LOG2E
    nq = MLA_HEADS * MLA_HEAD_PAD
    for hh in range(MLA_HEADS):
        lo, hi = hh * MLA_HEAD_PAD, (hh + 1) * MLA_HEAD_PAD
        qh = qq[:, lo:hi] * cosm + qq_sw[:, lo:hi] * sinm
        qm_ref[:, lo:hi] = (qh * scale).astype(BF16)
        km_ref[:, lo:hi] = (kv[:, lo:hi] + kr).astype(BF16)
    vm_ref[...] = kv[:, nq:].astype(BF16)


def _in_proj(x2d, lw, tabs, B, S, tm):
    T = B * S
    ns = S // tm
    tok = lambda b, s: (b * ns + s, 0)
    pos = lambda b, s: (s, 0)
    const = lambda b, s: (0, 0)

    def full(a):
        return pl.BlockSpec(a.shape, const)

    def tokspec(c):
        return pl.BlockSpec((tm, c), tok)

    out_shape = (
        jax.ShapeDtypeStruct((T, LRU_WIDTH), F32),
        jax.ShapeDtypeStruct((T, LRU_WIDTH), BF16),
        jax.ShapeDtypeStruct((T, DIL_WIDTH), BF16),
        jax.ShapeDtypeStruct((T, DIL_WIDTH), BF16),
        jax.ShapeDtypeStruct((T, DIL_WIDTH), BF16),
        jax.ShapeDtypeStruct((T, MLA_HEADS * MLA_HEAD_PAD), BF16),
        jax.ShapeDtypeStruct((T, MLA_HEADS * MLA_HEAD_PAD), BF16),
        jax.ShapeDtypeStruct((T, MLA_WIDTH), BF16),
    )
    out_specs = (
        tokspec(LRU_WIDTH), tokspec(LRU_WIDTH), tokspec(DIL_WIDTH), tokspec(DIL_WIDTH), tokspec(DIL_WIDTH),
        tokspec(MLA_HEADS * MLA_HEAD_PAD), tokspec(MLA_HEADS * MLA_HEAD_PAD), tokspec(MLA_WIDTH),
    )
    in_specs = [
        tokspec(D_MODEL), full(lw['g_mix']), full(lw['w1']), full(lw['g_q']), full(lw['w_uq']),
        full(lw['g_kv']), full(lw['w_ukv']),
        pl.BlockSpec((tm, DIL_WIDTH), pos), pl.BlockSpec((tm, DIL_WIDTH), pos),
        pl.BlockSpec((tm, LANES), pos), pl.BlockSpec((tm, LANES), pos),
    ]
    return pl.pallas_call(
        _in_proj_kernel, grid=(B, ns), in_specs=in_specs, out_specs=out_specs, out_shape=out_shape,
        compiler_params=_cparams(("parallel", "parallel")), name="in_proj",
    )(x2d, lw['g_mix'], lw['w1'], lw['g_q'], lw['w_uq'], lw['g_kv'], lw['w_ukv'],
      tabs['cosd'], tabs['sind'], tabs['cosm'], tabs['sinm'])


LRU_CHUNK_ROWS = 1024
LRU_HALO = SUBLANES


def _lru_kernel(xf_ref, xfp_ref, xfn_ref, xb_ref, xbp_ref, xbn_ref, cw_ref, cb_ref, wg_ref, bg_ref,
                lam_ref, hf_ref, hb_ref, xpad, a_f, b_f, a_b, b_b, hs_f, hs_b, hcar, *, B, tc):
    i = pl.program_id(0)
    n = pl.num_programs(0)
    R = tc * B
    left = CONV_WIDTH // 2
    groups = _lane_groups(LRU_WIDTH)

    @pl.when(i == 0)
    def _():
        hcar[...] = jnp.zeros_like(hcar)

    def prep(x_ref, xp_ref, xn_ref, ci, d, a_s, b_s):
        for c, cs in enumerate(groups):
            for b in range(B):
                xpad[c, pl.ds(b, left, stride=B), :] = jnp.where(ci > 0, xp_ref[b, LRU_HALO - left:LRU_HALO, cs], 0.0)
                xpad[c, pl.ds(left * B + b, tc, stride=B), :] = x_ref[b, :, cs]
                xpad[c, pl.ds((left + tc) * B + b, 1), :] = jnp.where(ci < n - 1, xn_ref[b, 0:1, cs], 0.0)
        halves = []
        for c, cs in enumerate(groups):
            xc = cb_ref[:, cs] + xpad[c, 0:R, :] * cw_ref[0:1, cs]
            for kk in range(1, CONV_WIDTH):
                xc = xc + xpad[c, kk * B:kk * B + R, :] * cw_ref[kk:kk + 1, cs]
            halves.append(xc)
        xc = jnp.concatenate(halves, axis=-1)
        g = _dot(xc.astype(BF16), wg_ref[d]) + bg_ref[d]
        r = jax.nn.sigmoid(g[:, :LRU_WIDTH])
        ig = jax.nn.sigmoid(g[:, LRU_WIDTH:])
        lam = lam_ref[d]
        softplus = jnp.maximum(-lam, 0.0) + jnp.log(1.0 + jnp.exp(-jnp.abs(lam)))
        log_a = (-LRU_C) * r * softplus
        a_s[...] = jnp.exp(log_a)
        b_s[...] = jnp.sqrt(1.0 - jnp.exp(2.0 * log_a)) * (ig * xc)

    prep(xf_ref, xfp_ref, xfn_ref, i, 0, a_f, b_f)
    prep(xb_ref, xbp_ref, xbn_ref, n - 1 - i, 1, a_b, b_b)

    def body(s, carry):
        hf, hb = carry
        rf = pl.multiple_of(s * B, B)
        hf = a_f[pl.ds(rf, B), :] * hf + b_f[pl.ds(rf, B), :]
        rb = pl.multiple_of((tc - 1 - s) * B, B)
        hb = a_b[pl.ds(rb, B), :] * hb + b_b[pl.ds(rb, B), :]
        for c, cs in enumerate(groups):
            hs_f[c, pl.ds(rf, B), :] = hf[:, cs]
            hs_b[c, pl.ds(rb, B), :] = hb[:, cs]
        return hf, hb

    hf, hb = lax.fori_loop(0, tc, body, (hcar[0], hcar[1]), unroll=8)
    hcar[0] = hf
    hcar[1] = hb
    for c, cs in enumerate(groups):
        for b in range(B):
            hf_ref[b, :, cs] = hs_f[c, pl.ds(b, tc, stride=B), :]
            hb_ref[b, :, cs] = hs_b[c, pl.ds(b, tc, stride=B), :]


def _lru(xr, lw, B, S):
    tc = LRU_CHUNK_ROWS // B
    n = S // tc
    nh = S // LRU_HALO
    hb_per = tc // LRU_HALO
    x3 = xr.reshape(B, S, LRU_WIDTH)
    W = LRU_WIDTH

    main_f = pl.BlockSpec((B, tc, W), lambda i: (0, i, 0))
    prev_f = pl.BlockSpec((B, LRU_HALO, W), lambda i: (0, jnp.maximum(i * hb_per - 1, 0), 0))
    next_f = pl.BlockSpec((B, LRU_HALO, W), lambda i: (0, jnp.minimum((i + 1) * hb_per, nh - 1), 0))
    main_b = pl.BlockSpec((B, tc, W), lambda i: (0, n - 1 - i, 0))
    prev_b = pl.BlockSpec((B, LRU_HALO, W), lambda i: (0, jnp.maximum((n - 1 - i) * hb_per - 1, 0), 0))
    next_b = pl.BlockSpec((B, LRU_HALO, W), lambda i: (0, jnp.minimum((n - i) * hb_per, nh - 1), 0))

    def full(a):
        nd = a.ndim
        return pl.BlockSpec(a.shape, lambda i: (0,) * nd)

    R = tc * B
    ng = W // LANES
    hf, hb = pl.pallas_call(
        functools.partial(_lru_kernel, B=B, tc=tc),
        grid=(n,),
        in_specs=[main_f, prev_f, next_f, main_b, prev_b, next_b,
                  full(lw['conv_w']), full(lw['conv_b']), full(lw['rg_w']), full(lw['rg_b']), full(lw['rg_lam'])],
        out_specs=(main_f, main_b),
        out_shape=(jax.ShapeDtypeStruct((B, S, W), F32),) * 2,
        scratch_shapes=[pltpu.VMEM((ng, R + (CONV_WIDTH - 1) * B, LANES), F32)] + [pltpu.VMEM((R, W), F32)] * 4
        + [pltpu.VMEM((ng, R, LANES), F32)] * 2 + [pltpu.VMEM((2, B, W), F32)],
        compiler_params=_cparams(("arbitrary",)), name="rglru",
    )(x3, x3, x3, x3, x3, x3, lw['conv_w'], lw['conv_b'], lw['rg_w'], lw['rg_b'], lw['rg_lam'])
    return hf.reshape(B * S, W), hb.reshape(B * S, W)


BAND_HALF = 64
DIL_TILE = 1024
DIL_QBLOCK = 128
DIL_STACK_ROWS = 256


def _dil_kernel(q_ref, kp_ref, k_ref, kn_ref, vp_ref, v_ref, vn_ref, o_ref,
                qs, kcat, vcat, acc_m, acc_l, acc_o, *, S):
    TT = DIL_TILE
    tile_start = pl.program_id(1) * TT
    groups = _lane_groups(DIL_WIDTH)
    for c, cs in enumerate(groups):
        qs[c] = q_ref[0, :, cs].astype(F32)
        for j, (kr, vr) in enumerate(((kp_ref, vp_ref), (k_ref, v_ref), (kn_ref, vn_ref))):
            kcat[c, j * TT:(j + 1) * TT, :] = kr[0, :, cs].astype(F32)
            vcat[c, j * TT:(j + 1) * TT, :] = vr[0, :, cs].astype(F32)
    head = lax.broadcasted_iota(jnp.int32, (1, DIL_WIDTH), 1) // DIL_HEAD_DIM

    order = sorted((dil for _, dil in DIL_PATTERNS), reverse=True)
    for dil in order:
        first = dil == order[0]
        nq = min(DIL_QBLOCK, TT // dil)
        nsub = TT // dil // nq
        nk = nq + 2 * BAND_HALF
        row = lax.broadcasted_iota(jnp.int32, (nq, nk), 0)
        col = lax.broadcasted_iota(jnp.int32, (nq, nk), 1)
        band = jnp.abs(col - BAND_HALF - row) <= BAND_HALF

        def rows(start, n, dil=dil):
            return pl.ds(pl.multiple_of(start, SUBLANES), n) if dil == 1 else pl.ds(start, n, stride=dil)

        def ld(ref, idx):
            return jnp.concatenate([ref[c, idx, :] for c in range(len(groups))], axis=-1)

        def st(ref, idx, val):
            for c, cs in enumerate(groups):
                ref[c, idx, :] = val[:, cs]

        def block(it, carry, dil=dil, nq=nq, nsub=nsub, nk=nk, band=band, col=col, rows=rows, first=first):
            r, j = (0, it) if dil == 1 else (it % dil, it // dil)
            off = j * nq * dil + r
            qi = rows(off, nq)
            ki = rows(off + TT - BAND_HALF * dil, nk)
            kpos = tile_start + off - BAND_HALF * dil + col * dil
            valid = band & (kpos >= 0) & (kpos < S)
            q = ld(qs, qi).astype(BF16)
            k = ld(kcat, ki).astype(BF16)
            v = ld(vcat, ki).astype(BF16)
            m_b = jnp.zeros((nq, DIL_WIDTH), F32)
            l_b = jnp.zeros((nq, DIL_WIDTH), F32)
            o_b = jnp.zeros((nq, DIL_WIDTH), F32)
            per = min(DIL_HEADS, max(1, DIL_STACK_ROWS // nq))
            groups_h = [range(h0, h0 + per) for h0 in range(0, DIL_HEADS, per)]
            valid_g = jnp.concatenate([valid] * per, axis=0)
            scores = [_dot_nt(jnp.concatenate([jnp.where(head == h, q, jnp.zeros_like(q)) for h in hs], axis=0), k)
                      for hs in groups_h]
            stats = []
            for s in scores:
                s = jnp.where(valid_g, s, -1e30)
                m = jnp.max(s, axis=-1, keepdims=True)
                e = jnp.exp2(s - m)
                stats.append((m, jnp.sum(e, axis=-1, keepdims=True), e.astype(BF16)))
            for hs, (m, l, e) in zip(groups_h, stats):
                o = _dot(e, v)
                for n_, h in enumerate(hs):
                    hm = head == h
                    sl = slice(n_ * nq, (n_ + 1) * nq)
                    m_b = jnp.where(hm, m[sl], m_b)
                    l_b = jnp.where(hm, l[sl], l_b)
                    o_b = jnp.where(hm, o[sl], o_b)
            if first:
                st(acc_l, qi, l_b)
                st(acc_o, qi, o_b)
                st(acc_m, qi, m_b)
                return carry
            m_old = ld(acc_m, qi)
            m_new = jnp.maximum(m_old, m_b)
            w_old = jnp.exp2(m_old - m_new)
            w_b = jnp.exp2(m_b - m_new)
            st(acc_l, qi, w_old * ld(acc_l, qi) + w_b * l_b)
            st(acc_o, qi, w_old * ld(acc_o, qi) + w_b * o_b)
            st(acc_m, qi, m_new)
            return carry

        lax.fori_loop(0, dil * nsub, block, 0, unroll=8 if nq < DIL_QBLOCK else 2)

    for c, cs in enumerate(groups):
        o_ref[0, :, cs] = (acc_o[c] / acc_l[c]).astype(BF16)


def _dilated(q, k, v, B, S):
    TT = DIL_TILE
    nt = S // TT
    W = DIL_WIDTH
    ng = W // LANES
    q3, k3, v3 = (a.reshape(B, S, W) for a in (q, k, v))
    own = pl.BlockSpec((1, TT, W), lambda b, i: (b, i, 0))
    prev = pl.BlockSpec((1, TT, W), lambda b, i: (b, jnp.maximum(i - 1, 0), 0))
    nxt = pl.BlockSpec((1, TT, W), lambda b, i: (b, jnp.minimum(i + 1, nt - 1), 0))
    o = pl.pallas_call(
        functools.partial(_dil_kernel, S=S), grid=(B, nt),
        in_specs=[own, prev, own, nxt, prev, own, nxt], out_specs=own,
        out_shape=jax.ShapeDtypeStruct((B, S, W), BF16),
        scratch_shapes=[pltpu.VMEM((ng, TT, LANES), F32), pltpu.VMEM((ng, 3 * TT, LANES), F32),
                        pltpu.VMEM((ng, 3 * TT, LANES), F32)] + [pltpu.VMEM((ng, TT, LANES), F32)] * 3,
        compiler_params=_cparams(("parallel", "parallel")), name="dilated_attn",
    )(q3, k3, k3, k3, v3, v3, v3)
    return o.reshape(B * S, W)


MLA_KEY_CHUNK = 256


def _mla_kernel(q_ref, k_ref, v_ref, o_ref, vt):
    S = k_ref.shape[1]
    tq = q_ref.shape[1]
    ck = MLA_KEY_CHUNK
    chunks = [slice(c * ck, (c + 1) * ck) for c in range(S // ck)]

    @pl.when(pl.program_id(1) == 0)
    def _():
        vt[...] = v_ref[0].T

    def fold(x, op):
        return op(x.reshape(ck // SUBLANES, SUBLANES, tq), axis=0)

    def score_pass(h):
        lo, hi = h * MLA_HEAD_PAD, (h + 1) * MLA_HEAD_PAD
        q_h = q_ref[0, :, lo:hi]
        m8 = None
        parts = []
        for cs in chunks:
            s_c = _dot_nt(k_ref[0, cs, lo:hi], q_h)
            parts.append(s_c)
            m8 = fold(s_c, jnp.max) if m8 is None else jnp.maximum(m8, fold(s_c, jnp.max))
        return parts, jnp.max(m8, axis=0, keepdims=True)

    def value_pass(h, sm):
        parts, m = sm
        acc = jnp.zeros((MLA_V, tq), F32)
        l8 = jnp.zeros((SUBLANES, tq), F32)
        for cs, s_c in zip(chunks, parts):
            e = jnp.exp2(s_c - m)
            l8 = l8 + fold(e, jnp.sum)
            acc = acc + _dot(vt[h * MLA_V:(h + 1) * MLA_V, cs], e.astype(BF16))
        return acc * (1.0 / jnp.sum(l8, axis=0, keepdims=True))

    m_next = score_pass(0)
    outs = []
    for h in range(MLA_HEADS):
        m = m_next
        if h + 1 < MLA_HEADS:
            m_next = score_pass(h + 1)
        outs.append(value_pass(h, m))
        if h % 2 == 1:
            pair = h // 2
            o_ref[0, :, pair * LANES:(pair + 1) * LANES] = jnp.concatenate(outs[h - 1:h + 1], axis=0).T.astype(BF16)


def _mla(qm, km, vm, B, S, tq):
    QW = MLA_HEADS * MLA_HEAD_PAD
    q3, k3, v3 = qm.reshape(B, S, QW), km.reshape(B, S, QW), vm.reshape(B, S, MLA_WIDTH)
    o = pl.pallas_call(
        _mla_kernel, grid=(B, S // tq),
        in_specs=[pl.BlockSpec((1, tq, QW), lambda b, i: (b, i, 0)),
                  pl.BlockSpec((1, S, QW), lambda b, i: (b, 0, 0)),
                  pl.BlockSpec((1, S, MLA_WIDTH), lambda b, i: (b, 0, 0))],
        out_specs=pl.BlockSpec((1, tq, MLA_WIDTH), lambda b, i: (b, i, 0)),
        out_shape=jax.ShapeDtypeStruct((B, S, MLA_WIDTH), BF16),
        scratch_shapes=[pltpu.VMEM((MLA_WIDTH, S), BF16)],
        compiler_params=_cparams(("parallel", "arbitrary")), name="mla_attn",
    )(q3, k3, v3)
    return o.reshape(B * S, MLA_WIDTH)


def _mix_out_kernel(*refs, moe, final):
    (x_ref, hf_ref, hb_ref, gy_ref, dil_ref, mo_ref, gl_ref, gd_ref, gm_ref, wout_ref, gffn_ref) = refs[:11]
    if moe:
        rw_ref, x1_ref, h2_ref, route_ref, cnt_ref, carry = refs[11:]
    elif final:
        wg_ref, wu_ref, wd_ref, gfin_ref, o_ref = refs[11:]
    else:
        wg_ref, wu_ref, wd_ref, o_ref = refs[11:]
    lru = (hf_ref[...] + hb_ref[...]) * gy_ref[...].astype(F32)
    mix = jnp.concatenate([
        _rms(lru, gl_ref[...], LRU_WIDTH), _rms(dil_ref[...].astype(F32), gd_ref[...], DIL_WIDTH),
        _rms(mo_ref[...].astype(F32), gm_ref[...], MLA_WIDTH)], axis=-1).astype(BF16)
    x1 = x_ref[...] + _dot(mix, wout_ref[...])
    h2 = _rms(x1, gffn_ref[...], D_MODEL)
    if not moe:
        h = h2.astype(BF16)
        g = _dot(h, wg_ref[...])
        u = _dot(h, wu_ref[...])
        out = x1 + _dot((g * jax.nn.sigmoid(g) * u).astype(BF16), wd_ref[...])
        o_ref[...] = _rms(out, gfin_ref[...], D_MODEL) if final else out
        return
    x1_ref[...] = x1
    h2_ref[...] = h2
    if moe:
        first = (pl.program_id(0) == 0) & (pl.program_id(1) == 0)

        @pl.when(first)
        def _():
            carry[...] = jnp.zeros_like(carry)

        h_hi = h2.astype(BF16)
        h_lo = (h2 - h_hi.astype(F32)).astype(BF16)
        logits = _dot(h_hi, rw_ref[0]) + (_dot(h_hi, rw_ref[1]) + _dot(h_lo, rw_ref[0]))
        tm = logits.shape[0]
        lane = lax.broadcasted_iota(jnp.int32, logits.shape, 1)
        neg = -jnp.inf
        lg = jnp.where(lane < N_EXPERTS, logits, neg)
        m1 = jnp.max(lg, axis=-1, keepdims=True)
        i1 = jnp.min(jnp.where(lg == m1, lane, LANES), axis=-1, keepdims=True)
        lg2 = jnp.where(lane == i1, neg, lg)
        m2 = jnp.max(lg2, axis=-1, keepdims=True)
        i2 = jnp.min(jnp.where(lg2 == m2, lane, LANES), axis=-1, keepdims=True)
        e2 = jnp.exp(m2 - m1)
        den = 1.0 + e2
        sel1, sel2 = lane == i1, lane == i2
        onehot = jnp.where(sel1 | sel2, 1.0, 0.0)
        tri = (lax.broadcasted_iota(jnp.int32, (tm, tm), 0) > lax.broadcasted_iota(jnp.int32, (tm, tm), 1))
        before = _dot(jnp.where(tri, 1.0, 0.0).astype(BF16), onehot.astype(BF16)) + carry[0:1, :]
        rank1 = jnp.sum(jnp.where(sel1, before, 0.0), axis=-1, keepdims=True)
        rank2 = jnp.sum(jnp.where(sel2, before, 0.0), axis=-1, keepdims=True)
        carry[...] = carry[...] + jnp.sum(onehot, axis=0, keepdims=True)
        cnt_ref[...] = carry[...]
        cols = (i1.astype(F32), i2.astype(F32), 1.0 / den, e2 / den, rank1, rank2)
        route = jnp.zeros(logits.shape, F32)
        for c, val in enumerate(cols):
            route = jnp.where(lane == c, val, route)
        route_ref[...] = route


def _mix_out(x2d, hf, hb, gy, dil, mo, lw, B, S, tm, moe, gfin=None):
    T = B * S
    ns = S // tm
    tok = lambda b, s: (b * ns + s, 0)
    const = lambda b, s: (0, 0)
    tokspec = lambda c: pl.BlockSpec((tm, c), tok)
    full = lambda a: pl.BlockSpec(a.shape, const)
    once = lambda a: pl.BlockSpec(a.shape, const, pipeline_mode=pl.Buffered(1))
    args = [x2d, hf, hb, gy, dil, mo, lw['g_l'], lw['g_d'], lw['g_m'], lw['w_out'], lw['g_ffn']]
    in_specs = [tokspec(D_MODEL), tokspec(LRU_WIDTH), tokspec(LRU_WIDTH), tokspec(LRU_WIDTH), tokspec(DIL_WIDTH),
                tokspec(MLA_WIDTH), full(lw['g_l']), full(lw['g_d']), full(lw['g_m']), once(lw['w_out']),
                full(lw['g_ffn'])]
    scratch = []
    if moe:
        args.append(lw['router_w'])
        in_specs.append(pl.BlockSpec(lw['router_w'].shape, lambda b, s: (0, 0, 0)))
        out_shape = [jax.ShapeDtypeStruct((T, D_MODEL), F32), jax.ShapeDtypeStruct((T, D_MODEL), F32),
                     jax.ShapeDtypeStruct((T, LANES), F32), jax.ShapeDtypeStruct((SUBLANES, LANES), F32)]
        out_specs = [tokspec(D_MODEL), tokspec(D_MODEL), tokspec(LANES), pl.BlockSpec((SUBLANES, LANES), const)]
        scratch = [pltpu.VMEM((SUBLANES, LANES), F32)]
    else:
        ffn_w = [lw['ffn_wg'], lw['ffn_wu'], lw['ffn_wd']]
        args += ffn_w + ([gfin] if gfin is not None else [])
        in_specs += [once(w) for w in ffn_w] + ([full(gfin)] if gfin is not None else [])
        out_shape = [jax.ShapeDtypeStruct((T, D_MODEL), F32)]
        out_specs = [tokspec(D_MODEL)]
    sem = ("arbitrary", "arbitrary") if moe else ("parallel", "parallel")
    res = pl.pallas_call(
        functools.partial(_mix_out_kernel, moe=moe, final=gfin is not None and not moe), grid=(B, ns),
        in_specs=in_specs, out_specs=tuple(out_specs), out_shape=tuple(out_shape), scratch_shapes=scratch,
        compiler_params=_cparams(sem), name="mix_out_moe" if moe else "mix_out_ffn",
    )(*args)
    return res if moe else res[0]


MOE_BLOCK_ROWS = 512
DISPATCH_TOKENS = 1024
COMBINE_TOKENS = 512


def _dispatch_kernel(pos_ref, h_ref, xs_in, xs_out, sem, *, tmd):
    del xs_in

    def body(t, c):
        src = h_ref.at[pl.ds(t, 1), :]
        for k in range(TOP_K):
            pltpu.make_async_copy(src, xs_out.at[pl.ds(pos_ref[0, 0, TOP_K * t + k], 1), :], sem).start()
        return c

    lax.fori_loop(0, tmd, body, 0, unroll=4)
    for _ in range(TOP_K):
        pltpu.make_async_copy(h_ref, xs_out.at[pl.ds(0, tmd), :], sem).wait()


def _dispatch(h2, pos, n_rows):
    T = h2.shape[0]
    tmd = DISPATCH_TOKENS
    pos3 = pos.reshape(T // tmd, 1, TOP_K * tmd)
    xs0 = jnp.zeros((n_rows, D_MODEL), F32)
    return pl.pallas_call(
        functools.partial(_dispatch_kernel, tmd=tmd), grid=(T // tmd,),
        in_specs=[pl.BlockSpec((1, 1, TOP_K * tmd), lambda i: (i, 0, 0), memory_space=pltpu.SMEM),
                  pl.BlockSpec((tmd, D_MODEL), lambda i: (i, 0)), pl.BlockSpec(memory_space=pl.ANY)],
        out_specs=pl.BlockSpec(memory_space=pl.ANY),
        out_shape=jax.ShapeDtypeStruct((n_rows, D_MODEL), F32),
        scratch_shapes=[pltpu.SemaphoreType.DMA(())],
        input_output_aliases={2: 0},
        compiler_params=_cparams(("arbitrary",)), name="moe_dispatch",
    )(pos3, h2, xs0)


def _gmm_kernel(be_ref, xs_ref, wg_ref, wu_ref, wd_ref, y_ref, xb):
    del be_ref
    f = pl.program_id(1)

    @pl.when(f == 0)
    def _():
        xb[...] = xs_ref[...].astype(BF16)

    h = xb[...]
    g = _dot(h, wg_ref[0])
    u = _dot(h, wu_ref[0])
    part = _dot((g * jax.nn.sigmoid(g) * u).astype(BF16), wd_ref[0])

    @pl.when(f == 0)
    def _():
        y_ref[...] = part

    @pl.when(f > 0)
    def _():
        y_ref[...] += part


def _gmm(xs, block_expert, wg, wu, wd, tf):
    P = xs.shape[0]
    bm = MOE_BLOCK_ROWS
    F = wg.shape[2]
    grid_spec = pltpu.PrefetchScalarGridSpec(
        num_scalar_prefetch=1, grid=(P // bm, F // tf),
        in_specs=[pl.BlockSpec((bm, D_MODEL), lambda j, f, be: (j, 0)),
                  pl.BlockSpec((1, D_MODEL, tf), lambda j, f, be: (be[j], 0, f)),
                  pl.BlockSpec((1, D_MODEL, tf), lambda j, f, be: (be[j], 0, f)),
                  pl.BlockSpec((1, tf, D_MODEL), lambda j, f, be: (be[j], f, 0))],
        out_specs=pl.BlockSpec((bm, D_MODEL), lambda j, f, be: (j, 0)),
        scratch_shapes=[pltpu.VMEM((bm, D_MODEL), BF16)])
    return pl.pallas_call(
        _gmm_kernel, grid_spec=grid_spec, out_shape=jax.ShapeDtypeStruct((P, D_MODEL), F32),
        compiler_params=_cparams(("parallel", "arbitrary")), name="moe_gmm",
    )(block_expert, xs, wg, wu, wd)


def _combine_kernel(*refs, tmc, final):
    if final:
        pos_ref, route_ref, x1_ref, y_hbm, gfin_ref, o_ref, ybuf, sem = refs
    else:
        pos_ref, route_ref, x1_ref, y_hbm, o_ref, ybuf, sem = refs

    def body(t, c):
        for k in range(TOP_K):
            pltpu.make_async_copy(y_hbm.at[pl.ds(pos_ref[0, 0, TOP_K * t + k], 1), :],
                                  ybuf.at[k, pl.ds(t, 1), :], sem).start()
        return c

    lax.fori_loop(0, tmc, body, 0, unroll=4)
    for k in range(TOP_K):
        pltpu.make_async_copy(y_hbm.at[pl.ds(0, tmc), :], ybuf.at[k], sem).wait()
    route = route_ref[...]
    out = x1_ref[...] + route[:, 2:3] * ybuf[0] + route[:, 3:4] * ybuf[1]
    if final:
        out = _rms(out, gfin_ref[...], D_MODEL)
    o_ref[...] = out


def _combine(pos, route, x1, y, gfin):
    T = x1.shape[0]
    tmc = COMBINE_TOKENS
    final = gfin is not None
    pos3 = pos.reshape(T // tmc, 1, TOP_K * tmc)
    args = [pos3, route, x1, y] + ([gfin] if final else [])
    in_specs = [pl.BlockSpec((1, 1, TOP_K * tmc), lambda i: (i, 0, 0), memory_space=pltpu.SMEM),
                pl.BlockSpec((tmc, LANES), lambda i: (i, 0)), pl.BlockSpec((tmc, D_MODEL), lambda i: (i, 0)),
                pl.BlockSpec(memory_space=pl.ANY)]
    if final:
        in_specs.append(pl.BlockSpec(gfin.shape, lambda i: (0, 0)))
    return pl.pallas_call(
        functools.partial(_combine_kernel, tmc=tmc, final=final), grid=(T // tmc,), in_specs=in_specs,
        out_specs=pl.BlockSpec((tmc, D_MODEL), lambda i: (i, 0)),
        out_shape=jax.ShapeDtypeStruct((T, D_MODEL), F32),
        scratch_shapes=[pltpu.VMEM((TOP_K, tmc, D_MODEL), F32), pltpu.SemaphoreType.DMA(())],
        compiler_params=_cparams(("arbitrary",)), name="moe_combine",
    )(*args)


def _moe(h2, x1, route, counts, wg, wu, wd, gfin, tf):
    T = h2.shape[0]
    bm = MOE_BLOCK_ROWS
    n_blocks = TOP_K * T // bm + N_EXPERTS
    cnt = counts[0, :N_EXPERTS].astype(jnp.int32)
    padded = (cnt + bm - 1) // bm * bm
    ends = jnp.cumsum(padded)
    starts = ends - padded
    experts = route[:, 0:TOP_K].astype(jnp.int32)
    pos = starts[experts] + route[:, 4:4 + TOP_K].astype(jnp.int32)
    block_start = jnp.arange(n_blocks, dtype=jnp.int32) * bm
    block_expert = jnp.minimum(jnp.sum((block_start[:, None] >= ends[None, :]).astype(jnp.int32), axis=1),
                               N_EXPERTS - 1)
    xs = _dispatch(h2, pos, n_blocks * bm)
    y = _gmm(xs, block_expert, wg, wu, wd, tf)
    return _combine(pos, route, x1, y, gfin)


def _pad_heads(w, heads, dim, at=0):
    k = w.shape[0]
    w = w.reshape(k, heads, dim)
    w = jnp.pad(w, ((0, 0), (0, 0), (at, MLA_HEAD_PAD - at - dim)))
    return w.reshape(k, heads * MLA_HEAD_PAD)


def _block_diag(w):
    nb, bs, _ = w.shape
    eye = jnp.eye(nb, dtype=w.dtype)
    return (eye[:, None, :, None] * w[:, :, None, :]).reshape(nb * bs, nb * bs)


def _rope_tables(S):
    def cs(dim):
        half = dim // 2
        freqs = jnp.power(jnp.float32(ROPE_THETA), -jnp.arange(half, dtype=F32) * 2.0 / dim)
        ang = jnp.arange(S, dtype=F32)[:, None] * freqs[None, :]
        c, s = jnp.cos(ang), jnp.sin(ang)
        return jnp.concatenate([c, c], -1), jnp.concatenate([-s, s], -1)

    cd, sd = cs(DIL_HEAD_DIM)
    cm, sm = cs(MLA_ROPE)
    ones = jnp.ones((S, MLA_NOPE), F32)
    pad = MLA_HEAD_PAD - MLA_NOPE - MLA_ROPE
    return dict(
        cosd=jnp.tile(cd, (1, DIL_HEADS)), sind=jnp.tile(sd, (1, DIL_HEADS)),
        cosm=jnp.concatenate([ones, cm, jnp.ones((S, pad), F32)], -1),
        sinm=jnp.concatenate([0.0 * ones, sm, jnp.zeros((S, pad), F32)], -1))


def _prep_layer(l, p):
    w_in = p['w_in'][l]
    offs = [0, 256, 512, 768, 1024, 1280, 1536, 1664, 1696]
    xr, yr, qd, kd, vd, cq, ckv, kr = [w_in[:, a:b] for a, b in zip(offs[:-1], offs[1:])]
    kr_pad = _pad_heads(kr, 1, MLA_ROPE, at=MLA_NOPE)
    w1 = jnp.concatenate([xr, yr, qd, kd, vd, cq, ckv, kr_pad], axis=1)
    w_uq2 = _pad_heads(p['mla_w_uq'][l], MLA_HEADS, MLA_NOPE + MLA_ROPE)

    w_ukv = p['mla_w_ukv'][l].reshape(MLA_KV_RANK, MLA_HEADS, MLA_NOPE + MLA_V)
    kn = _pad_heads(w_ukv[:, :, :MLA_NOPE].reshape(MLA_KV_RANK, -1), MLA_HEADS, MLA_NOPE)
    vm = w_ukv[:, :, MLA_NOPE:].reshape(MLA_KV_RANK, MLA_WIDTH)
    w_ukv2 = jnp.concatenate([kn, vm], axis=1)

    rg_w = jnp.stack([jnp.concatenate([_block_diag(p['rg_w_a'][l, d]), _block_diag(p['rg_w_i'][l, d])], axis=1)
                      for d in range(2)])
    rg_b = jnp.stack([jnp.concatenate([p['rg_b_a'][l, d], p['rg_b_i'][l, d]])[None, :] for d in range(2)])
    g_out = p['mix_out_norm_g'][l]
    row = lambda v: v.reshape(1, -1).astype(F32)
    return dict(
        g_mix=row(p['norm_mix_g'][l]), w1=w1.astype(BF16), g_q=row(p['mla_q_norm_g'][l]), w_uq=w_uq2.astype(BF16),
        g_kv=row(p['mla_kv_norm_g'][l]), w_ukv=w_ukv2.astype(BF16),
        conv_w=p['conv_w'][l].astype(F32), conv_b=row(p['conv_b'][l]), rg_w=rg_w.astype(BF16), rg_b=rg_b.astype(F32),
        rg_lam=p['rg_lambda'][l].reshape(2, 1, LRU_WIDTH).astype(F32),
        g_l=row(g_out[:LRU_WIDTH]), g_d=row(g_out[LRU_WIDTH:LRU_WIDTH + DIL_WIDTH]),
        g_m=row(g_out[LRU_WIDTH + DIL_WIDTH:]), w_out=p['w_out'][l].astype(BF16), g_ffn=row(p['norm_ffn_g'][l]))


def _mixers(x2d, lw, tabs, B, S, tm, moe, gfin=None):
    xr, gy, q, k, v, qm, km, vm = _in_proj(x2d, lw, tabs, B, S, tm)
    hf, hb = _lru(xr, lw, B, S)
    dil = _dilated(q, k, v, B, S)
    mo = _mla(qm, km, vm, B, S, 256)
    return _mix_out(x2d, hf, hb, gy, dil, mo, lw, B, S, tm, moe, gfin)


def _trunk(x, layers, p, depth):
    B, S, _ = x.shape
    tabs = _rope_tables(S)
    x2d = x.reshape(B * S, D_MODEL)
    tm = 512
    for l in range(depth):
        lw = layers[l]
        moe = l % 2 == 1
        last = l == depth - 1
        gfin = p['final_norm_g'].reshape(1, D_MODEL) if last else None
        j = l // 2
        if moe:
            rw = jnp.pad(p['router_w'][j].astype(F32), ((0, 0), (0, LANES - N_EXPERTS)))
            rw_hi = rw.astype(BF16)
            lw = dict(lw, router_w=jnp.stack([rw_hi, (rw - rw_hi.astype(F32)).astype(BF16)]))
            x1, h2, route, counts = _mixers(x2d, lw, tabs, B, S, tm, True)
            x2d = _moe(h2, x1, route, counts, lw['moe_wg'], lw['moe_wu'], lw['moe_wd'], gfin, 1792)
        else:
            x2d = _mixers(x2d, lw, tabs, B, S, tm, False, gfin)
    return x2d.reshape(B, S, D_MODEL)


def kernel(x_prompt, x_sample, norm_mix_g, w_in, conv_w, conv_b, rg_w_a, rg_b_a, rg_w_i, rg_b_i, rg_lambda, mla_q_norm_g, mla_w_uq, mla_kv_norm_g, mla_w_ukv, mix_out_norm_g, w_out, norm_ffn_g, ffn_w_gate, ffn_w_up, ffn_w_down, router_w, moe_w_gate, moe_w_up, moe_w_down, final_norm_g):
    p = dict(norm_mix_g=norm_mix_g, w_in=w_in, conv_w=conv_w, conv_b=conv_b, rg_w_a=rg_w_a, rg_b_a=rg_b_a,
             rg_w_i=rg_w_i, rg_b_i=rg_b_i, rg_lambda=rg_lambda, mla_q_norm_g=mla_q_norm_g, mla_w_uq=mla_w_uq,
             mla_kv_norm_g=mla_kv_norm_g, mla_w_ukv=mla_w_ukv, mix_out_norm_g=mix_out_norm_g, w_out=w_out,
             norm_ffn_g=norm_ffn_g, router_w=router_w, final_norm_g=final_norm_g)
    depth = w_in.shape[0]
    layers = []
    for l in range(depth):
        lw = _prep_layer(l, p)
        j = l // 2
        if l % 2 == 1:
            lw.update(moe_wg=moe_w_gate[j].astype(BF16), moe_wu=moe_w_up[j].astype(BF16),
                      moe_wd=moe_w_down[j].astype(BF16))
        else:
            lw.update(ffn_wg=ffn_w_gate[j].astype(BF16), ffn_wu=ffn_w_up[j].astype(BF16),
                      ffn_wd=ffn_w_down[j].astype(BF16))
        layers.append(lw)
    return (_trunk(x_prompt, layers, p, depth), _trunk(x_sample, layers, p, depth))
```

```python
import functools
import math

import jax
import jax.numpy as jnp
from jax import lax
from jax.experimental import pallas as pl
from jax.experimental.pallas import tpu as pltpu

F32 = jnp.float32
BF16 = jnp.bfloat16

D_MODEL = 1024
NORM_EPS = 1e-6
ROPE_THETA = 10000.0
LRU_WIDTH = 256
LRU_BLOCKS = 4
CONV_WIDTH = 4
LRU_C = 8.0
DIL_HEADS = 4
DIL_HEAD_DIM = 64
DIL_WIDTH = DIL_HEADS * DIL_HEAD_DIM
DIL_PATTERNS = ((128, 1), (512, 4), (2048, 16))
MLA_HEADS = 8
MLA_NOPE = 64
MLA_ROPE = 32
MLA_V = 64
MLA_Q_RANK = 256
MLA_KV_RANK = 128
MLA_WIDTH = MLA_HEADS * MLA_V
N_EXPERTS = 8
TOP_K = 2

LANES = 128
SUBLANES = 8
MLA_HEAD_PAD = LANES
VMEM_LIMIT = 56 * 1024 * 1024
LOG2E = math.log2(math.e)

_C_XR, _C_YR, _C_Q, _C_K, _C_V, _C_CQ, _C_CKV, _C_KR, _C_END = (0, 256, 512, 768, 1024, 1280, 1536, 1664, 1792)


def _cparams(sem):
    return pltpu.CompilerParams(dimension_semantics=sem, vmem_limit_bytes=VMEM_LIMIT)


def _rms(x, g, n):
    ms = jnp.sum(x * x, axis=-1, keepdims=True) * (1.0 / n)
    return x * lax.rsqrt(ms + NORM_EPS) * g


def _gelu_tanh(x):
    return 0.5 * x * (1.0 + jnp.tanh(0.7978845608028654 * (x + 0.044715 * (x * x * x))))


def _dot(a, b):
    return jnp.dot(a, b, preferred_element_type=F32)


def _dot_nt(a, b):
    return lax.dot_general(a, b, (((1,), (1,)), ((), ())), preferred_element_type=F32)


def _lane_groups(width):
    return [slice(c * LANES, (c + 1) * LANES) for c in range(width // LANES)]


def _swap_rotary_halves(x, half, period, start):
    lane = lax.broadcasted_iota(jnp.int32, (1, LANES), 1) % period
    first = lane < start + half
    outs = []
    for cs in _lane_groups(x.shape[1]):
        xs = x[:, cs]
        outs.append(jnp.where(first, pltpu.roll(xs, LANES - half, 1), pltpu.roll(xs, half, 1)))
    return outs[0] if len(outs) == 1 else jnp.concatenate(outs, axis=-1)


def _in_proj_kernel(x_ref, g_ref, w1_ref, gq_ref, wuq_ref, gkv_ref, wukv_ref,
                    cosd_ref, sind_ref, cosm_ref, sinm_ref,
                    xr_ref, gy_ref, q_ref, k_ref, v_ref, qm_ref, km_ref, vm_ref):
    h = _rms(x_ref[...], g_ref[...], D_MODEL).astype(BF16)
    p = _dot(h, w1_ref[...])
    xr_ref[...] = p[:, _C_XR:_C_YR]
    gy_ref[...] = _gelu_tanh(p[:, _C_YR:_C_Q]).astype(BF16)
    cosd, sind = cosd_ref[...], sind_ref[...]
    half_d = DIL_HEAD_DIM // 2
    q, k = p[:, _C_Q:_C_K], p[:, _C_K:_C_V]
    q = q * cosd + _swap_rotary_halves(q, half_d, DIL_HEAD_DIM, 0) * sind
    q_ref[...] = (q * (DIL_HEAD_DIM ** -0.5 * LOG2E)).astype(BF16)
    k_ref[...] = (k * cosd + _swap_rotary_halves(k, half_d, DIL_HEAD_DIM, 0) * sind).astype(BF16)
    v_ref[...] = p[:, _C_V:_C_CQ].astype(BF16)

    cosm, sinm = cosm_ref[...], sinm_ref[...]
    half_m = MLA_ROPE // 2
    cqn = _rms(p[:, _C_CQ:_C_CKV], gq_ref[...], MLA_Q_RANK).astype(BF16)
    nq = MLA_HEADS * MLA_HEAD_PAD
    qq2 = _dot(cqn, wuq_ref[...])
    qq, qq_sw = qq2[:, :nq], qq2[:, nq:]
    ckvn = _rms(p[:, _C_CKV:_C_KR], gkv_ref[...], MLA_KV_RANK).astype(BF16)
    kv = _dot(ckvn, wukv_ref[...])
    kr = p[:, _C_KR:_C_END]
    kr = kr * cosm + _swap_rotary_halves(kr, half_m, MLA_HEAD_PAD, MLA_NOPE) * sinm
    scale = (MLA_NOPE + MLA_ROPE) ** -0.5 * LOG2E
    for hh in range(MLA_HEADS):
        lo, hi = hh * MLA_HEAD_PAD, (hh + 1) * MLA_HEAD_PAD
        qh = qq[:, lo:hi] * cosm + qq_sw[:, lo:hi] * sinm
        qm_ref[:, lo:hi] = (qh * scale).astype(BF16)
        km_ref[:, lo:hi] = (kv[:, lo:hi] + kr).astype(BF16)
    vm_ref[...] = kv[:, nq:].astype(BF16)


def _in_proj(x2d, lw, tabs, B, S, tm):
    T = B * S
    ns = S // tm
    tok = lambda b, s: (b * ns + s, 0)
    pos = lambda b, s: (s, 0)
    const = lambda b, s: (0, 0)

    def full(a):
        return pl.BlockSpec(a.shape, const)

    def tokspec(c):
        return pl.BlockSpec((tm, c), tok)

    out_shape = (
        jax.ShapeDtypeStruct((T, LRU_WIDTH), F32),
        jax.ShapeDtypeStruct((T, LRU_WIDTH), BF16),
        jax.ShapeDtypeStruct((T, DIL_WIDTH), BF16),
        jax.ShapeDtypeStruct((T, DIL_WIDTH), BF16),
        jax.ShapeDtypeStruct((T, DIL_WIDTH), BF16),
        jax.ShapeDtypeStruct((T, MLA_HEADS * MLA_HEAD_PAD), BF16),
        jax.ShapeDtypeStruct((T, MLA_HEADS * MLA_HEAD_PAD), BF16),
        jax.ShapeDtypeStruct((T, MLA_WIDTH), BF16),
    )
    out_specs = (
        tokspec(LRU_WIDTH), tokspec(LRU_WIDTH), tokspec(DIL_WIDTH), tokspec(DIL_WIDTH), tokspec(DIL_WIDTH),
        tokspec(MLA_HEADS * MLA_HEAD_PAD), tokspec(MLA_HEADS * MLA_HEAD_PAD), tokspec(MLA_WIDTH),
    )
    in_specs = [
        tokspec(D_MODEL), full(lw['g_mix']), full(lw['w1']), full(lw['g_q']), full(lw['w_uq']),
        full(lw['g_kv']), full(lw['w_ukv']),
        pl.BlockSpec((tm, DIL_WIDTH), pos), pl.BlockSpec((tm, DIL_WIDTH), pos),
        pl.BlockSpec((tm, LANES), pos), pl.BlockSpec((tm, LANES), pos),
    ]
    return pl.pallas_call(
        _in_proj_kernel, grid=(B, ns), in_specs=in_specs, out_specs=out_specs, out_shape=out_shape,
        compiler_params=_cparams(("parallel", "parallel")), name="in_proj",
    )(x2d, lw['g_mix'], lw['w1'], lw['g_q'], lw['w_uq'], lw['g_kv'], lw['w_ukv'],
      tabs['cosd'], tabs['sind'], tabs['cosm'], tabs['sinm'])


LRU_CHUNK_ROWS = 1024
LRU_HALO = SUBLANES


def _lru_kernel(xf_ref, xfp_ref, xfn_ref, xb_ref, xbp_ref, xbn_ref, cw_ref, cb_ref, wg_ref, bg_ref,
                lam_ref, hf_ref, hb_ref, xpad, a_f, b_f, a_b, b_b, hs_f, hs_b, hcar, *, B, tc):
    i = pl.program_id(0)
    n = pl.num_programs(0)
    R = tc * B
    left = CONV_WIDTH // 2
    groups = _lane_groups(LRU_WIDTH)

    @pl.when(i == 0)
    def _():
        hcar[...] = jnp.zeros_like(hcar)

    def prep(x_ref, xp_ref, xn_ref, ci, d, a_s, b_s):
        for c, cs in enumerate(groups):
            for b in range(B):
                xpad[c, pl.ds(b, left, stride=B), :] = jnp.where(ci > 0, xp_ref[b, LRU_HALO - left:LRU_HALO, cs], 0.0)
                xpad[c, pl.ds(left * B + b, tc, stride=B), :] = x_ref[b, :, cs]
                xpad[c, pl.ds((left + tc) * B + b, 1), :] = jnp.where(ci < n - 1, xn_ref[b, 0:1, cs], 0.0)
        halves = []
        for c, cs in enumerate(groups):
            xc = cb_ref[:, cs] + xpad[c, 0:R, :] * cw_ref[0:1, cs]
            for kk in range(1, CONV_WIDTH):
                xc = xc + xpad[c, kk * B:kk * B + R, :] * cw_ref[kk:kk + 1, cs]
            halves.append(xc)
        xc = jnp.concatenate(halves, axis=-1)
        g = _dot(xc.astype(BF16), wg_ref[d]) + bg_ref[d]
        r = jax.nn.sigmoid(g[:, :LRU_WIDTH])
        ig = jax.nn.sigmoid(g[:, LRU_WIDTH:])
        lam = lam_ref[d]
        softplus = jnp.maximum(-lam, 0.0) + jnp.log(1.0 + jnp.exp(-jnp.abs(lam)))
        log_a = (-LRU_C) * r * softplus
        a_s[...] = jnp.exp(log_a)
        b_s[...] = jnp.sqrt(1.0 - jnp.exp(2.0 * log_a)) * (ig * xc)

    prep(xf_ref, xfp_ref, xfn_ref, i, 0, a_f, b_f)
    prep(xb_ref, xbp_ref, xbn_ref, n - 1 - i, 1, a_b, b_b)

    def body(s, carry):
        hf, hb = carry
        rf = pl.multiple_of(s * B, B)
        hf = a_f[pl.ds(rf, B), :] * hf + b_f[pl.ds(rf, B), :]
        rb = pl.multiple_of((tc - 1 - s) * B, B)
        hb = a_b[pl.ds(rb, B), :] * hb + b_b[pl.ds(rb, B), :]
        for c, cs in enumerate(groups):
            hs_f[c, pl.ds(rf, B), :] = hf[:, cs]
            hs_b[c, pl.ds(rb, B), :] = hb[:, cs]
        return hf, hb

    hf, hb = lax.fori_loop(0, tc, body, (hcar[0], hcar[1]), unroll=8)
    hcar[0] = hf
    hcar[1] = hb
    for c, cs in enumerate(groups):
        for b in range(B):
            hf_ref[b, :, cs] = hs_f[c, pl.ds(b, tc, stride=B), :]
            hb_ref[b, :, cs] = hs_b[c, pl.ds(b, tc, stride=B), :]


def _lru(xr, lw, B, S):
    tc = LRU_CHUNK_ROWS // B
    n = S // tc
    nh = S // LRU_HALO
    hb_per = tc // LRU_HALO
    x3 = xr.reshape(B, S, LRU_WIDTH)
    W = LRU_WIDTH

    main_f = pl.BlockSpec((B, tc, W), lambda i: (0, i, 0))
    prev_f = pl.BlockSpec((B, LRU_HALO, W), lambda i: (0, jnp.maximum(i * hb_per - 1, 0), 0))
    next_f = pl.BlockSpec((B, LRU_HALO, W), lambda i: (0, jnp.minimum((i + 1) * hb_per, nh - 1), 0))
    main_b = pl.BlockSpec((B, tc, W), lambda i: (0, n - 1 - i, 0))
    prev_b = pl.BlockSpec((B, LRU_HALO, W), lambda i: (0, jnp.maximum((n - 1 - i) * hb_per - 1, 0), 0))
    next_b = pl.BlockSpec((B, LRU_HALO, W), lambda i: (0, jnp.minimum((n - i) * hb_per, nh - 1), 0))

    def full(a):
        nd = a.ndim
        return pl.BlockSpec(a.shape, lambda i: (0,) * nd)

    R = tc * B
    ng = W // LANES
    hf, hb = pl.pallas_call(
        functools.partial(_lru_kernel, B=B, tc=tc),
        grid=(n,),
        in_specs=[main_f, prev_f, next_f, main_b, prev_b, next_b,
                  full(lw['conv_w']), full(lw['conv_b']), full(lw['rg_w']), full(lw['rg_b']), full(lw['rg_lam'])],
        out_specs=(main_f, main_b),
        out_shape=(jax.ShapeDtypeStruct((B, S, W), F32),) * 2,
        scratch_shapes=[pltpu.VMEM((ng, R + (CONV_WIDTH - 1) * B, LANES), F32)] + [pltpu.VMEM((R, W), F32)] * 4
        + [pltpu.VMEM((ng, R, LANES), F32)] * 2 + [pltpu.VMEM((2, B, W), F32)],
        compiler_params=_cparams(("arbitrary",)), name="rglru",
    )(x3, x3, x3, x3, x3, x3, lw['conv_w'], lw['conv_b'], lw['rg_w'], lw['rg_b'], lw['rg_lam'])
    return hf.reshape(B * S, W), hb.reshape(B * S, W)


BAND_HALF = 64
DIL_TILE = 1024
DIL_QBLOCK = 128
DIL_STACK_ROWS = 256


def _dil_kernel(q_ref, kp_ref, k_ref, kn_ref, vp_ref, v_ref, vn_ref, o_ref,
                qs, kcat, vcat, acc_m, acc_l, acc_o, *, S):
    TT = DIL_TILE
    tile_start = pl.program_id(1) * TT
    groups = _lane_groups(DIL_WIDTH)
    for c, cs in enumerate(groups):
        qs[c] = q_ref[0, :, cs].astype(F32)
        for j, (kr, vr) in enumerate(((kp_ref, vp_ref), (k_ref, v_ref), (kn_ref, vn_ref))):
            kcat[c, j * TT:(j + 1) * TT, :] = kr[0, :, cs].astype(F32)
            vcat[c, j * TT:(j + 1) * TT, :] = vr[0, :, cs].astype(F32)
    head = lax.broadcasted_iota(jnp.int32, (1, DIL_WIDTH), 1) // DIL_HEAD_DIM

    order = sorted((dil for _, dil in DIL_PATTERNS), reverse=True)
    for dil in order:
        first = dil == order[0]
        nq = min(DIL_QBLOCK, TT // dil)
        nsub = TT // dil // nq
        nk = nq + 2 * BAND_HALF
        row = lax.broadcasted_iota(jnp.int32, (nq, nk), 0)
        col = lax.broadcasted_iota(jnp.int32, (nq, nk), 1)
        band = jnp.abs(col - BAND_HALF - row) <= BAND_HALF

        def rows(start, n, dil=dil):
            return pl.ds(pl.multiple_of(start, SUBLANES), n) if dil == 1 else pl.ds(start, n, stride=dil)

        def ld(ref, idx):
            return jnp.concatenate([ref[c, idx, :] for c in range(len(groups))], axis=-1)

        def st(ref, idx, val):
            for c, cs in enumerate(groups):
                ref[c, idx, :] = val[:, cs]

        def block(it, carry, dil=dil, nq=nq, nsub=nsub, nk=nk, band=band, col=col, rows=rows, first=first):
            r, j = (0, it) if dil == 1 else (it % dil, it // dil)
            off = j * nq * dil + r
            qi = rows(off, nq)
            ki = rows(off + TT - BAND_HALF * dil, nk)
            kpos = tile_start + off - BAND_HALF * dil + col * dil
            valid = band & (kpos >= 0) & (kpos < S)
            q = ld(qs, qi).astype(BF16)
            k = ld(kcat, ki).astype(BF16)
            v = ld(vcat, ki).astype(BF16)
            m_b = jnp.zeros((nq, DIL_WIDTH), F32)
            l_b = jnp.zeros((nq, DIL_WIDTH), F32)
            o_b = jnp.zeros((nq, DIL_WIDTH), F32)
            per = min(DIL_HEADS, max(1, DIL_STACK_ROWS // nq))
            groups_h = [range(h0, h0 + per) for h0 in range(0, DIL_HEADS, per)]
            valid_g = jnp.concatenate([valid] * per, axis=0)
            scores = [_dot_nt(jnp.concatenate([jnp.where(head == h, q, jnp.zeros_like(q)) for h in hs], axis=0), k)
                      for hs in groups_h]
            stats = []
            for s in scores:
                s = jnp.where(valid_g, s, -1e30)
                m = jnp.max(s, axis=-1, keepdims=True)
                e = jnp.exp2(s - m)
                stats.append((m, jnp.sum(e, axis=-1, keepdims=True), e.astype(BF16)))
            for hs, (m, l, e) in zip(groups_h, stats):
                o = _dot(e, v)
                for n_, h in enumerate(hs):
                    hm = head == h
                    sl = slice(n_ * nq, (n_ + 1) * nq)
                    m_b = jnp.where(hm, m[sl], m_b)
                    l_b = jnp.where(hm, l[sl], l_b)
                    o_b = jnp.where(hm, o[sl], o_b)
            if first:
                st(acc_l, qi, l_b)
                st(acc_o, qi, o_b)
                st(acc_m, qi, m_b)
                return carry
            m_old = ld(acc_m, qi)
            m_new = jnp.maximum(m_old, m_b)
            w_old = jnp.exp2(m_old - m_new)
            w_b = jnp.exp2(m_b - m_new)
            st(acc_l, qi, w_old * ld(acc_l, qi) + w_b * l_b)
            st(acc_o, qi, w_old * ld(acc_o, qi) + w_b * o_b)
            st(acc_m, qi, m_new)
            return carry

        lax.fori_loop(0, dil * nsub, block, 0, unroll=8 if nq < DIL_QBLOCK else 2)

    for c, cs in enumerate(groups):
        o_ref[0, :, cs] = (acc_o[c] / acc_l[c]).astype(BF16)


def _dilated(q, k, v, B, S):
    TT = DIL_TILE
    nt = S // TT
    W = DIL_WIDTH
    ng = W // LANES
    q3, k3, v3 = (a.reshape(B, S, W) for a in (q, k, v))
    own = pl.BlockSpec((1, TT, W), lambda b, i: (b, i, 0))
    prev = pl.BlockSpec((1, TT, W), lambda b, i: (b, jnp.maximum(i - 1, 0), 0))
    nxt = pl.BlockSpec((1, TT, W), lambda b, i: (b, jnp.minimum(i + 1, nt - 1), 0))
    o = pl.pallas_call(
        functools.partial(_dil_kernel, S=S), grid=(B, nt),
        in_specs=[own, prev, own, nxt, prev, own, nxt], out_specs=own,
        out_shape=jax.ShapeDtypeStruct((B, S, W), BF16),
        scratch_shapes=[pltpu.VMEM((ng, TT, LANES), F32), pltpu.VMEM((ng, 3 * TT, LANES), F32),
                        pltpu.VMEM((ng, 3 * TT, LANES), F32)] + [pltpu.VMEM((ng, TT, LANES), F32)] * 3,
        compiler_params=_cparams(("parallel", "parallel")), name="dilated_attn",
    )(q3, k3, k3, k3, v3, v3, v3)
    return o.reshape(B * S, W)


MLA_KEY_CHUNK = 256


def _mla_kernel(q_ref, k_ref, v_ref, o_ref, vt):
    S = k_ref.shape[1]
    tq = q_ref.shape[1]
    ck = MLA_KEY_CHUNK
    chunks = [slice(c * ck, (c + 1) * ck) for c in range(S // ck)]

    @pl.when(pl.program_id(1) == 0)
    def _():
        vt[...] = v_ref[0].T

    def fold(x, op):
        return op(x.reshape(ck // SUBLANES, SUBLANES, tq), axis=0)

    def score_pass(h):
        lo, hi = h * MLA_HEAD_PAD, (h + 1) * MLA_HEAD_PAD
        q_h = q_ref[0, :, lo:hi]
        m8 = None
        parts = []
        for cs in chunks:
            s_c = _dot_nt(k_ref[0, cs, lo:hi], q_h)
            parts.append(s_c)
            m8 = fold(s_c, jnp.max) if m8 is None else jnp.maximum(m8, fold(s_c, jnp.max))
        return parts, jnp.max(m8, axis=0, keepdims=True)

    def value_pass(h, sm):
        parts, m = sm
        acc = jnp.zeros((MLA_V, tq), F32)
        l8 = jnp.zeros((SUBLANES, tq), F32)
        for cs, s_c in zip(chunks, parts):
            e = jnp.exp2(s_c - m)
            l8 = l8 + fold(e, jnp.sum)
            acc = acc + _dot(vt[h * MLA_V:(h + 1) * MLA_V, cs], e.astype(BF16))
        return acc * (1.0 / jnp.sum(l8, axis=0, keepdims=True))

    m_next = score_pass(0)
    outs = []
    for h in range(MLA_HEADS):
        m = m_next
        if h + 1 < MLA_HEADS:
            m_next = score_pass(h + 1)
        outs.append(value_pass(h, m))
        if h % 2 == 1:
            pair = h // 2
            o_ref[0, :, pair * LANES:(pair + 1) * LANES] = jnp.concatenate(outs[h - 1:h + 1], axis=0).T.astype(BF16)


def _mla(qm, km, vm, B, S, tq):
    QW = MLA_HEADS * MLA_HEAD_PAD
    q3, k3, v3 = qm.reshape(B, S, QW), km.reshape(B, S, QW), vm.reshape(B, S, MLA_WIDTH)
    o = pl.pallas_call(
        _mla_kernel, grid=(B, S // tq),
        in_specs=[pl.BlockSpec((1, tq, QW), lambda b, i: (b, i, 0)),
                  pl.BlockSpec((1, S, QW), lambda b, i: (b, 0, 0)),
                  pl.BlockSpec((1, S, MLA_WIDTH), lambda b, i: (b, 0, 0))],
        out_specs=pl.BlockSpec((1, tq, MLA_WIDTH), lambda b, i: (b, i, 0)),
        out_shape=jax.ShapeDtypeStruct((B, S, MLA_WIDTH), BF16),
        scratch_shapes=[pltpu.VMEM((MLA_WIDTH, S), BF16)],
        compiler_params=_cparams(("parallel", "arbitrary")), name="mla_attn",
    )(q3, k3, v3)
    return o.reshape(B * S, MLA_WIDTH)


def _mix_out_kernel(*refs, moe, final):
    (x_ref, hf_ref, hb_ref, gy_ref, dil_ref, mo_ref, gl_ref, gd_ref, gm_ref, wout_ref, gffn_ref) = refs[:11]
    if moe:
        rw_ref, x1_ref, h2_ref, route_ref, cnt_ref, carry = refs[11:]
    elif final:
        wg_ref, wu_ref, wd_ref, gfin_ref, o_ref = refs[11:]
    else:
        wg_ref, wu_ref, wd_ref, o_ref = refs[11:]
    lru = (hf_ref[...] + hb_ref[...]) * gy_ref[...].astype(F32)
    mix = jnp.concatenate([
        _rms(lru, gl_ref[...], LRU_WIDTH), _rms(dil_ref[...].astype(F32), gd_ref[...], DIL_WIDTH),
        _rms(mo_ref[...].astype(F32), gm_ref[...], MLA_WIDTH)], axis=-1).astype(BF16)
    x1 = x_ref[...] + _dot(mix, wout_ref[...])
    h2 = _rms(x1, gffn_ref[...], D_MODEL)
    if not moe:
        h = h2.astype(BF16)
        g = _dot(h, wg_ref[...])
        u = _dot(h, wu_ref[...])
        out = x1 + _dot((g * jax.nn.sigmoid(g) * u).astype(BF16), wd_ref[...])
        o_ref[...] = _rms(out, gfin_ref[...], D_MODEL) if final else out
        return
    x1_ref[...] = x1
    h2_ref[...] = h2
    if moe:
        first = (pl.program_id(0) == 0) & (pl.program_id(1) == 0)

        @pl.when(first)
        def _():
            carry[...] = jnp.zeros_like(carry)

        h_hi = h2.astype(BF16)
        h_lo = (h2 - h_hi.astype(F32)).astype(BF16)
        logits = _dot(h_hi, rw_ref[0]) + (_dot(h_hi, rw_ref[1]) + _dot(h_lo, rw_ref[0]))
        tm = logits.shape[0]
        lane = lax.broadcasted_iota(jnp.int32, logits.shape, 1)
        neg = -jnp.inf
        lg = jnp.where(lane < N_EXPERTS, logits, neg)
        m1 = jnp.max(lg, axis=-1, keepdims=True)
        i1 = jnp.min(jnp.where(lg == m1, lane, LANES), axis=-1, keepdims=True)
        lg2 = jnp.where(lane == i1, neg, lg)
        m2 = jnp.max(lg2, axis=-1, keepdims=True)
        i2 = jnp.min(jnp.where(lg2 == m2, lane, LANES), axis=-1, keepdims=True)
        e2 = jnp.exp(m2 - m1)
        den = 1.0 + e2
        sel1, sel2 = lane == i1, lane == i2
        onehot = jnp.where(sel1 | sel2, 1.0, 0.0)
        tri = (lax.broadcasted_iota(jnp.int32, (tm, tm), 0) > lax.broadcasted_iota(jnp.int32, (tm, tm), 1))
        before = _dot(jnp.where(tri, 1.0, 0.0).astype(BF16), onehot.astype(BF16)) + carry[0:1, :]
        rank1 = jnp.sum(jnp.where(sel1, before, 0.0), axis=-1, keepdims=True)
        rank2 = jnp.sum(jnp.where(sel2, before, 0.0), axis=-1, keepdims=True)
        carry[...] = carry[...] + jnp.sum(onehot, axis=0, keepdims=True)
        cnt_ref[...] = carry[...]
        cols = (i1.astype(F32), i2.astype(F32), 1.0 / den, e2 / den, rank1, rank2)
        route = jnp.zeros(logits.shape, F32)
        for c, val in enumerate(cols):
            route = jnp.where(lane == c, val, route)
        route_ref[...] = route


def _mix_out(x2d, hf, hb, gy, dil, mo, lw, B, S, tm, moe, gfin=None):
    T = B * S
    ns = S // tm
    tok = lambda b, s: (b * ns + s, 0)
    const = lambda b, s: (0, 0)
    tokspec = lambda c: pl.BlockSpec((tm, c), tok)
    full = lambda a: pl.BlockSpec(a.shape, const)
    once = lambda a: pl.BlockSpec(a.shape, const, pipeline_mode=pl.Buffered(1))
    args = [x2d, hf, hb, gy, dil, mo, lw['g_l'], lw['g_d'], lw['g_m'], lw['w_out'], lw['g_ffn']]
    in_specs = [tokspec(D_MODEL), tokspec(LRU_WIDTH), tokspec(LRU_WIDTH), tokspec(LRU_WIDTH), tokspec(DIL_WIDTH),
                tokspec(MLA_WIDTH), full(lw['g_l']), full(lw['g_d']), full(lw['g_m']), once(lw['w_out']),
                full(lw['g_ffn'])]
    scratch = []
    if moe:
        args.append(lw['router_w'])
        in_specs.append(pl.BlockSpec(lw['router_w'].shape, lambda b, s: (0, 0, 0)))
        out_shape = [jax.ShapeDtypeStruct((T, D_MODEL), F32), jax.ShapeDtypeStruct((T, D_MODEL), F32),
                     jax.ShapeDtypeStruct((T, LANES), F32), jax.ShapeDtypeStruct((SUBLANES, LANES), F32)]
        out_specs = [tokspec(D_MODEL), tokspec(D_MODEL), tokspec(LANES), pl.BlockSpec((SUBLANES, LANES), const)]
        scratch = [pltpu.VMEM((SUBLANES, LANES), F32)]
    else:
        ffn_w = [lw['ffn_wg'], lw['ffn_wu'], lw['ffn_wd']]
        args += ffn_w + ([gfin] if gfin is not None else [])
        in_specs += [once(w) for w in ffn_w] + ([full(gfin)] if gfin is not None else [])
        out_shape = [jax.ShapeDtypeStruct((T, D_MODEL), F32)]
        out_specs = [tokspec(D_MODEL)]
    sem = ("arbitrary", "arbitrary") if moe else ("parallel", "parallel")
    res = pl.pallas_call(
        functools.partial(_mix_out_kernel, moe=moe, final=gfin is not None and not moe), grid=(B, ns),
        in_specs=in_specs, out_specs=tuple(out_specs), out_shape=tuple(out_shape), scratch_shapes=scratch,
        compiler_params=_cparams(sem), name="mix_out_moe" if moe else "mix_out_ffn",
    )(*args)
    return res if moe else res[0]


MOE_BLOCK_ROWS = 512
DISPATCH_TOKENS = 1024
COMBINE_TOKENS = 512


ZERO_BLOCKS = 2 * N_EXPERTS


def _dispatch_kernel(pos_ref, zinfo_ref, h_ref, xs_out, zbuf, sem, zsem, *, tmd):
    bm = MOE_BLOCK_ROWS

    @pl.when(pl.program_id(0) == 0)
    def _():
        zbuf[...] = jnp.zeros_like(zbuf)

        def zero_copy(i):
            start = pl.multiple_of(zinfo_ref[0, 0, i] * bm, bm)
            return pltpu.make_async_copy(zbuf, xs_out.at[pl.ds(start, bm), :], zsem)

        for i in range(ZERO_BLOCKS):
            @pl.when(zinfo_ref[0, 0, ZERO_BLOCKS + i] != 0)
            def _():
                zero_copy(i).start()
        for i in range(ZERO_BLOCKS):
            @pl.when(zinfo_ref[0, 0, ZERO_BLOCKS + i] != 0)
            def _():
                zero_copy(i).wait()

    def body(t, c):
        src = h_ref.at[pl.ds(t, 1), :]
        for k in range(TOP_K):
            pltpu.make_async_copy(src, xs_out.at[pl.ds(pos_ref[0, 0, TOP_K * t + k], 1), :], sem).start()
        return c

    lax.fori_loop(0, tmd, body, 0, unroll=4)
    for _ in range(TOP_K):
        pltpu.make_async_copy(h_ref, xs_out.at[pl.ds(0, tmd), :], sem).wait()


def _dispatch(h2, pos, zinfo, n_rows):
    T = h2.shape[0]
    tmd = DISPATCH_TOKENS
    pos3 = pos.reshape(T // tmd, 1, TOP_K * tmd)
    return pl.pallas_call(
        functools.partial(_dispatch_kernel, tmd=tmd), grid=(T // tmd,),
        in_specs=[pl.BlockSpec((1, 1, TOP_K * tmd), lambda i: (i, 0, 0), memory_space=pltpu.SMEM),
                  pl.BlockSpec((1, 1, 2 * ZERO_BLOCKS), lambda i: (0, 0, 0), memory_space=pltpu.SMEM),
                  pl.BlockSpec((tmd, D_MODEL), lambda i: (i, 0))],
        out_specs=pl.BlockSpec(memory_space=pl.ANY),
        out_shape=jax.ShapeDtypeStruct((n_rows, D_MODEL), F32),
        scratch_shapes=[pltpu.VMEM((MOE_BLOCK_ROWS, D_MODEL), F32), pltpu.SemaphoreType.DMA(()),
                        pltpu.SemaphoreType.DMA(())],
        compiler_params=_cparams(("arbitrary",)), name="moe_dispatch",
    )(pos3, zinfo.reshape(1, 1, 2 * ZERO_BLOCKS), h2)


def _gmm_kernel(be_ref, xs_ref, wg_ref, wu_ref, wd_ref, y_ref, xb):
    del be_ref
    f = pl.program_id(1)

    @pl.when(f == 0)
    def _():
        xb[...] = xs_ref[...].astype(BF16)

    h = xb[...]
    g = _dot(h, wg_ref[0])
    u = _dot(h, wu_ref[0])
    part = _dot((g * jax.nn.sigmoid(g) * u).astype(BF16), wd_ref[0])

    @pl.when(f == 0)
    def _():
        y_ref[...] = part

    @pl.when(f > 0)
    def _():
        y_ref[...] += part


def _gmm(xs, block_expert, wg, wu, wd, tf):
    P = xs.shape[0]
    bm = MOE_BLOCK_ROWS
    F = wg.shape[2]
    grid_spec = pltpu.PrefetchScalarGridSpec(
        num_scalar_prefetch=1, grid=(P // bm, F // tf),
        in_specs=[pl.BlockSpec((bm, D_MODEL), lambda j, f, be: (j, 0)),
                  pl.BlockSpec((1, D_MODEL, tf), lambda j, f, be: (be[j], 0, f)),
                  pl.BlockSpec((1, D_MODEL, tf), lambda j, f, be: (be[j], 0, f)),
                  pl.BlockSpec((1, tf, D_MODEL), lambda j, f, be: (be[j], f, 0))],
        out_specs=pl.BlockSpec((bm, D_MODEL), lambda j, f, be: (j, 0)),
        scratch_shapes=[pltpu.VMEM((bm, D_MODEL), BF16)])
    return pl.pallas_call(
        _gmm_kernel, grid_spec=grid_spec, out_shape=jax.ShapeDtypeStruct((P, D_MODEL), F32),
        compiler_params=_cparams(("parallel", "arbitrary")), name="moe_gmm",
    )(block_expert, xs, wg, wu, wd)


def _combine_kernel(*refs, tmc, final):
    if final:
        pos_ref, route_ref, x1_ref, y_hbm, gfin_ref, o_ref, ybuf, sem = refs
    else:
        pos_ref, route_ref, x1_ref, y_hbm, o_ref, ybuf, sem = refs

    def body(t, c):
        for k in range(TOP_K):
            pltpu.make_async_copy(y_hbm.at[pl.ds(pos_ref[0, 0, TOP_K * t + k], 1), :],
                                  ybuf.at[k, pl.ds(t, 1), :], sem).start()
        return c

    lax.fori_loop(0, tmc, body, 0, unroll=4)
    for k in range(TOP_K):
        pltpu.make_async_copy(y_hbm.at[pl.ds(0, tmc), :], ybuf.at[k], sem).wait()
    route = route_ref[...]
    out = x1_ref[...] + route[:, 2:3] * ybuf[0] + route[:, 3:4] * ybuf[1]
    if final:
        out = _rms(out, gfin_ref[...], D_MODEL)
    o_ref[...] = out


def _combine(pos, route, x1, y, gfin):
    T = x1.shape[0]
    tmc = COMBINE_TOKENS
    final = gfin is not None
    pos3 = pos.reshape(T // tmc, 1, TOP_K * tmc)
    args = [pos3, route, x1, y] + ([gfin] if final else [])
    in_specs = [pl.BlockSpec((1, 1, TOP_K * tmc), lambda i: (i, 0, 0), memory_space=pltpu.SMEM),
                pl.BlockSpec((tmc, LANES), lambda i: (i, 0)), pl.BlockSpec((tmc, D_MODEL), lambda i: (i, 0)),
                pl.BlockSpec(memory_space=pl.ANY)]
    if final:
        in_specs.append(pl.BlockSpec(gfin.shape, lambda i: (0, 0)))
    return pl.pallas_call(
        functools.partial(_combine_kernel, tmc=tmc, final=final), grid=(T // tmc,), in_specs=in_specs,
        out_specs=pl.BlockSpec((tmc, D_MODEL), lambda i: (i, 0)),
        out_shape=jax.ShapeDtypeStruct((T, D_MODEL), F32),
        scratch_shapes=[pltpu.VMEM((TOP_K, tmc, D_MODEL), F32), pltpu.SemaphoreType.DMA(())],
        compiler_params=_cparams(("arbitrary",)), name="moe_combine",
    )(*args)


def _moe(h2, x1, route, counts, wg, wu, wd, gfin, tf):
    T = h2.shape[0]
    bm = MOE_BLOCK_ROWS
    n_blocks = TOP_K * T // bm + N_EXPERTS
    cnt = counts[0, :N_EXPERTS].astype(jnp.int32)
    padded = (cnt + bm - 1) // bm * bm
    ends = jnp.cumsum(padded)
    starts = ends - padded
    experts = route[:, 0:TOP_K].astype(jnp.int32)
    pos = starts[experts] + route[:, 4:4 + TOP_K].astype(jnp.int32)
    block_start = jnp.arange(n_blocks, dtype=jnp.int32) * bm
    block_expert = jnp.minimum(jnp.sum((block_start[:, None] >= ends[None, :]).astype(jnp.int32), axis=1),
                               N_EXPERTS - 1)
    tail = n_blocks - N_EXPERTS + jnp.arange(N_EXPERTS, dtype=jnp.int32)
    zinfo = jnp.concatenate([ends // bm - 1, tail, (padded > 0).astype(jnp.int32),
                             (tail * bm >= ends[-1]).astype(jnp.int32)]).astype(jnp.int32)
    xs = _dispatch(h2, pos, zinfo, n_blocks * bm)
    y = _gmm(xs, block_expert, wg, wu, wd, tf)
    return _combine(pos, route, x1, y, gfin)


def _pad_heads(w, heads, dim, at=0):
    k = w.shape[0]
    w = w.reshape(k, heads, dim)
    w = jnp.pad(w, ((0, 0), (0, 0), (at, MLA_HEAD_PAD - at - dim)))
    return w.reshape(k, heads * MLA_HEAD_PAD)


def _block_diag(w):
    nb, bs, _ = w.shape
    eye = jnp.eye(nb, dtype=w.dtype)
    return (eye[:, None, :, None] * w[:, :, None, :]).reshape(nb * bs, nb * bs)


def _rope_tables(S):
    def cs(dim):
        half = dim // 2
        freqs = jnp.power(jnp.float32(ROPE_THETA), -jnp.arange(half, dtype=F32) * 2.0 / dim)
        ang = jnp.arange(S, dtype=F32)[:, None] * freqs[None, :]
        c, s = jnp.cos(ang), jnp.sin(ang)
        return jnp.concatenate([c, c], -1), jnp.concatenate([-s, s], -1)

    cd, sd = cs(DIL_HEAD_DIM)
    cm, sm = cs(MLA_ROPE)
    ones = jnp.ones((S, MLA_NOPE), F32)
    pad = MLA_HEAD_PAD - MLA_NOPE - MLA_ROPE
    return dict(
        cosd=jnp.tile(cd, (1, DIL_HEADS)), sind=jnp.tile(sd, (1, DIL_HEADS)),
        cosm=jnp.concatenate([ones, cm, jnp.ones((S, pad), F32)], -1),
        sinm=jnp.concatenate([0.0 * ones, sm, jnp.zeros((S, pad), F32)], -1))


def _prep_layer(l, p):
    w_in = p['w_in'][l]
    offs = [0, 256, 512, 768, 1024, 1280, 1536, 1664, 1696]
    xr, yr, qd, kd, vd, cq, ckv, kr = [w_in[:, a:b] for a, b in zip(offs[:-1], offs[1:])]
    kr_pad = _pad_heads(kr, 1, MLA_ROPE, at=MLA_NOPE)
    w1 = jnp.concatenate([xr, yr, qd, kd, vd, cq, ckv, kr_pad], axis=1)
    w_uq = p['mla_w_uq'][l].reshape(MLA_Q_RANK, MLA_HEADS, MLA_NOPE + MLA_ROPE)
    qn, qr = w_uq[:, :, :MLA_NOPE], w_uq[:, :, MLA_NOPE:]
    qr_sw = qr.reshape(MLA_Q_RANK, MLA_HEADS, 2, MLA_ROPE // 2)[:, :, ::-1, :].reshape(qr.shape)
    flat = lambda a, b: jnp.concatenate([a, b], -1).reshape(MLA_Q_RANK, MLA_HEADS * (MLA_NOPE + MLA_ROPE))
    w_uq2 = jnp.concatenate([_pad_heads(flat(qn, qr), MLA_HEADS, MLA_NOPE + MLA_ROPE),
                             _pad_heads(flat(jnp.zeros_like(qn), qr_sw), MLA_HEADS, MLA_NOPE + MLA_ROPE)], axis=1)

    w_ukv = p['mla_w_ukv'][l].reshape(MLA_KV_RANK, MLA_HEADS, MLA_NOPE + MLA_V)
    kn = _pad_heads(w_ukv[:, :, :MLA_NOPE].reshape(MLA_KV_RANK, -1), MLA_HEADS, MLA_NOPE)
    vm = w_ukv[:, :, MLA_NOPE:].reshape(MLA_KV_RANK, MLA_WIDTH)
    w_ukv2 = jnp.concatenate([kn, vm], axis=1)

    rg_w = jnp.stack([jnp.concatenate([_block_diag(p['rg_w_a'][l, d]), _block_diag(p['rg_w_i'][l, d])], axis=1)
                      for d in range(2)])
    rg_b = jnp.stack([jnp.concatenate([p['rg_b_a'][l, d], p['rg_b_i'][l, d]])[None, :] for d in range(2)])
    g_out = p['mix_out_norm_g'][l]
    row = lambda v: v.reshape(1, -1).astype(F32)
    return dict(
        g_mix=row(p['norm_mix_g'][l]), w1=w1.astype(BF16), g_q=row(p['mla_q_norm_g'][l]), w_uq=w_uq2.astype(BF16),
        g_kv=row(p['mla_kv_norm_g'][l]), w_ukv=w_ukv2.astype(BF16),
        conv_w=p['conv_w'][l].astype(F32), conv_b=row(p['conv_b'][l]), rg_w=rg_w.astype(BF16), rg_b=rg_b.astype(F32),
        rg_lam=p['rg_lambda'][l].reshape(2, 1, LRU_WIDTH).astype(F32),
        g_l=row(g_out[:LRU_WIDTH]), g_d=row(g_out[LRU_WIDTH:LRU_WIDTH + DIL_WIDTH]),
        g_m=row(g_out[LRU_WIDTH + DIL_WIDTH:]), w_out=p['w_out'][l].astype(BF16), g_ffn=row(p['norm_ffn_g'][l]))


def _mixers(x2d, lw, tabs, B, S, tm, moe, gfin=None):
    xr, gy, q, k, v, qm, km, vm = _in_proj(x2d, lw, tabs, B, S, tm)
    hf, hb = _lru(xr, lw, B, S)
    dil = _dilated(q, k, v, B, S)
    mo = _mla(qm, km, vm, B, S, 256)
    return _mix_out(x2d, hf, hb, gy, dil, mo, lw, B, S, tm, moe, gfin)


def _trunk(x, layers, p, depth):
    B, S, _ = x.shape
    tabs = _rope_tables(S)
    x2d = x.reshape(B * S, D_MODEL)
    tm = 512
    for l in range(depth):
        lw = layers[l]
        moe = l % 2 == 1
        last = l == depth - 1
        gfin = p['final_norm_g'].reshape(1, D_MODEL) if last else None
        j = l // 2
        if moe:
            rw = jnp.pad(p['router_w'][j].astype(F32), ((0, 0), (0, LANES - N_EXPERTS)))
            rw_hi = rw.astype(BF16)
            lw = dict(lw, router_w=jnp.stack([rw_hi, (rw - rw_hi.astype(F32)).astype(BF16)]))
            x1, h2, route, counts = _mixers(x2d, lw, tabs, B, S, tm, True)
            x2d = _moe(h2, x1, route, counts, lw['moe_wg'], lw['moe_wu'], lw['moe_wd'], gfin, 1792)
        else:
            x2d = _mixers(x2d, lw, tabs, B, S, tm, False, gfin)
    return x2d.reshape(B, S, D_MODEL)


def kernel(x_prompt, x_sample, norm_mix_g, w_in, conv_w, conv_b, rg_w_a, rg_b_a, rg_w_i, rg_b_i, rg_lambda, mla_q_norm_g, mla_w_uq, mla_kv_norm_g, mla_w_ukv, mix_out_norm_g, w_out, norm_ffn_g, ffn_w_gate, ffn_w_up, ffn_w_down, router_w, moe_w_gate, moe_w_up, moe_w_down, final_norm_g):
    p = dict(norm_mix_g=norm_mix_g, w_in=w_in, conv_w=conv_w, conv_b=conv_b, rg_w_a=rg_w_a, rg_b_a=rg_b_a,
             rg_w_i=rg_w_i, rg_b_i=rg_b_i, rg_lambda=rg_lambda, mla_q_norm_g=mla_q_norm_g, mla_w_uq=mla_w_uq,
             mla_kv_norm_g=mla_kv_norm_g, mla_w_ukv=mla_w_ukv, mix_out_norm_g=mix_out_norm_g, w_out=w_out,
             norm_ffn_g=norm_ffn_g, router_w=router_w, final_norm_g=final_norm_g)
    depth = w_in.shape[0]
    layers = []
    for l in range(depth):
        lw = _prep_layer(l, p)
        j = l // 2
        if l % 2 == 1:
            lw.update(moe_wg=moe_w_gate[j].astype(BF16), moe_wu=moe_w_up[j].astype(BF16),
                      moe_wd=moe_w_down[j].astype(BF16))
        else:
            lw.update(ffn_wg=ffn_w_gate[j].astype(BF16), ffn_wu=ffn_w_up[j].astype(BF16),
                      ffn_wd=ffn_w_down[j].astype(BF16))
        layers.append(lw)
    return (_trunk(x_prompt, layers, p, depth), _trunk(x_sample, layers, p, depth))
```

```python
import functools
import math

import jax
import jax.numpy as jnp
from jax import lax
from jax.experimental import pallas as pl
from jax.experimental.pallas import tpu as pltpu

F32 = jnp.float32
BF16 = jnp.bfloat16

D_MODEL = 1024
NORM_EPS = 1e-6
ROPE_THETA = 10000.0
LRU_WIDTH = 256
LRU_BLOCKS = 4
CONV_WIDTH = 4
LRU_C = 8.0
DIL_HEADS = 4
DIL_HEAD_DIM = 64
DIL_WIDTH = DIL_HEADS * DIL_HEAD_DIM
DIL_PATTERNS = ((128, 1), (512, 4), (2048, 16))
MLA_HEADS = 8
MLA_NOPE = 64
MLA_ROPE = 32
MLA_V = 64
MLA_Q_RANK = 256
MLA_KV_RANK = 128
MLA_WIDTH = MLA_HEADS * MLA_V
N_EXPERTS = 8
TOP_K = 2

LANES = 128
SUBLANES = 8
MLA_HEAD_PAD = LANES
VMEM_LIMIT = 56 * 1024 * 1024
LOG2E = math.log2(math.e)

_C_XR, _C_YR, _C_Q, _C_K, _C_V, _C_CQ, _C_CKV, _C_KR, _C_END = (0, 256, 512, 768, 1024, 1280, 1536, 1664, 1792)


def _cparams(sem):
    return pltpu.CompilerParams(dimension_semantics=sem, vmem_limit_bytes=VMEM_LIMIT)


def _rms(x, g, n):
    ms = jnp.sum(x * x, axis=-1, keepdims=True) * (1.0 / n)
    return x * lax.rsqrt(ms + NORM_EPS) * g


def _gelu_tanh(x):
    return 0.5 * x * (1.0 + jnp.tanh(0.7978845608028654 * (x + 0.044715 * (x * x * x))))


def _dot(a, b):
    return jnp.dot(a, b, preferred_element_type=F32)


def _dot_nt(a, b):
    return lax.dot_general(a, b, (((1,), (1,)), ((), ())), preferred_element_type=F32)


def _lane_groups(width):
    return [slice(c * LANES, (c + 1) * LANES) for c in range(width // LANES)]


def _swap_rotary_halves(x, half, period, start):
    lane = lax.broadcasted_iota(jnp.int32, (1, LANES), 1) % period
    first = lane < start + half
    outs = []
    for cs in _lane_groups(x.shape[1]):
        xs = x[:, cs]
        outs.append(jnp.where(first, pltpu.roll(xs, LANES - half, 1), pltpu.roll(xs, half, 1)))
    return outs[0] if len(outs) == 1 else jnp.concatenate(outs, axis=-1)


def _in_proj_kernel(x_ref, g_ref, w1_ref, gq_ref, wuq_ref, gkv_ref, wukv_ref,
                    cosd_ref, sind_ref, cosm_ref, sinm_ref,
                    xr_ref, gy_ref, q_ref, k_ref, v_ref, qm_ref, km_ref, vm_ref):
    h = _rms(x_ref[...], g_ref[...], D_MODEL).astype(BF16)
    p = _dot(h, w1_ref[...])
    xr_ref[...] = p[:, _C_XR:_C_YR]
    gy_ref[...] = _gelu_tanh(p[:, _C_YR:_C_Q]).astype(BF16)
    cosd, sind = cosd_ref[...], sind_ref[...]
    half_d = DIL_HEAD_DIM // 2
    q, k = p[:, _C_Q:_C_K], p[:, _C_K:_C_V]
    q = q * cosd + _swap_rotary_halves(q, half_d, DIL_HEAD_DIM, 0) * sind
    q_ref[...] = (q * (DIL_HEAD_DIM ** -0.5 * LOG2E)).astype(BF16)
    k_ref[...] = (k * cosd + _swap_rotary_halves(k, half_d, DIL_HEAD_DIM, 0) * sind).astype(BF16)
    v_ref[...] = p[:, _C_V:_C_CQ].astype(BF16)

    cosm, sinm = cosm_ref[...], sinm_ref[...]
    half_m = MLA_ROPE // 2
    cqn = _rms(p[:, _C_CQ:_C_CKV], gq_ref[...], MLA_Q_RANK).astype(BF16)
    nq = MLA_HEADS * MLA_HEAD_PAD
    qq2 = _dot(cqn, wuq_ref[...])
    qq, qq_sw = qq2[:, :nq], qq2[:, nq:]
    ckvn = _rms(p[:, _C_CKV:_C_KR], gkv_ref[...], MLA_KV_RANK).astype(BF16)
    kv = _dot(ckvn, wukv_ref[...])
    kr = p[:, _C_KR:_C_END]
    kr = kr * cosm + _swap_rotary_halves(kr, half_m, MLA_HEAD_PAD, MLA_NOPE) * sinm
    scale = (MLA_NOPE + MLA_ROPE) ** -0.5 * LOG2E
    for hh in range(MLA_HEADS):
        lo, hi = hh * MLA_HEAD_PAD, (hh + 1) * MLA_HEAD_PAD
        qh = qq[:, lo:hi] * cosm + qq_sw[:, lo:hi] * sinm
        qm_ref[:, lo:hi] = (qh * scale).astype(BF16)
        km_ref[:, lo:hi] = (kv[:, lo:hi] + kr).astype(BF16)
    vm_ref[...] = kv[:, nq:].astype(BF16)


def _in_proj(x2d, lw, tabs, B, S, tm):
    T = B * S
    ns = S // tm
    tok = lambda b, s: (b * ns + s, 0)
    pos = lambda b, s: (s, 0)
    const = lambda b, s: (0, 0)

    def full(a):
        return pl.BlockSpec(a.shape, const)

    def tokspec(c):
        return pl.BlockSpec((tm, c), tok)

    out_shape = (
        jax.ShapeDtypeStruct((T, LRU_WIDTH), F32),
        jax.ShapeDtypeStruct((T, LRU_WIDTH), BF16),
        jax.ShapeDtypeStruct((T, DIL_WIDTH), BF16),
        jax.ShapeDtypeStruct((T, DIL_WIDTH), BF16),
        jax.ShapeDtypeStruct((T, DIL_WIDTH), BF16),
        jax.ShapeDtypeStruct((T, MLA_HEADS * MLA_HEAD_PAD), BF16),
        jax.ShapeDtypeStruct((T, MLA_HEADS * MLA_HEAD_PAD), BF16),
        jax.ShapeDtypeStruct((T, MLA_WIDTH), BF16),
    )
    out_specs = (
        tokspec(LRU_WIDTH), tokspec(LRU_WIDTH), tokspec(DIL_WIDTH), tokspec(DIL_WIDTH), tokspec(DIL_WIDTH),
        tokspec(MLA_HEADS * MLA_HEAD_PAD), tokspec(MLA_HEADS * MLA_HEAD_PAD), tokspec(MLA_WIDTH),
    )
    in_specs = [
        tokspec(D_MODEL), full(lw['g_mix']), full(lw['w1']), full(lw['g_q']), full(lw['w_uq']),
        full(lw['g_kv']), full(lw['w_ukv']),
        pl.BlockSpec((tm, DIL_WIDTH), pos), pl.BlockSpec((tm, DIL_WIDTH), pos),
        pl.BlockSpec((tm, LANES), pos), pl.BlockSpec((tm, LANES), pos),
    ]
    return pl.pallas_call(
        _in_proj_kernel, grid=(B, ns), in_specs=in_specs, out_specs=out_specs, out_shape=out_shape,
        compiler_params=_cparams(("parallel", "parallel")), name="in_proj",
    )(x2d, lw['g_mix'], lw['w1'], lw['g_q'], lw['w_uq'], lw['g_kv'], lw['w_ukv'],
      tabs['cosd'], tabs['sind'], tabs['cosm'], tabs['sinm'])


LRU_CHUNK_ROWS = 1024
LRU_HALO = SUBLANES


def _lru_kernel(xf_ref, xfp_ref, xfn_ref, xb_ref, xbp_ref, xbn_ref, cw_ref, cb_ref, wg_ref, bg_ref,
                lam_ref, hf_ref, hb_ref, xpad, a_f, b_f, a_b, b_b, hs_f, hs_b, hcar, *, B, tc):
    i = pl.program_id(0)
    n = pl.num_programs(0)
    R = tc * B
    left = CONV_WIDTH // 2
    groups = _lane_groups(LRU_WIDTH)

    @pl.when(i == 0)
    def _():
        hcar[...] = jnp.zeros_like(hcar)

    def prep(x_ref, xp_ref, xn_ref, ci, d, a_s, b_s):
        for c, cs in enumerate(groups):
            for b in range(B):
                xpad[c, pl.ds(b, left, stride=B), :] = jnp.where(ci > 0, xp_ref[b, LRU_HALO - left:LRU_HALO, cs], 0.0)
                xpad[c, pl.ds(left * B + b, tc, stride=B), :] = x_ref[b, :, cs]
                xpad[c, pl.ds((left + tc) * B + b, 1), :] = jnp.where(ci < n - 1, xn_ref[b, 0:1, cs], 0.0)
        halves = []
        for c, cs in enumerate(groups):
            xc = cb_ref[:, cs] + xpad[c, 0:R, :] * cw_ref[0:1, cs]
            for kk in range(1, CONV_WIDTH):
                xc = xc + xpad[c, kk * B:kk * B + R, :] * cw_ref[kk:kk + 1, cs]
            halves.append(xc)
        xc = jnp.concatenate(halves, axis=-1)
        g = _dot(xc.astype(BF16), wg_ref[d]) + bg_ref[d]
        r = jax.nn.sigmoid(g[:, :LRU_WIDTH])
        ig = jax.nn.sigmoid(g[:, LRU_WIDTH:])
        lam = lam_ref[d]
        softplus = jnp.maximum(-lam, 0.0) + jnp.log(1.0 + jnp.exp(-jnp.abs(lam)))
        log_a = (-LRU_C) * r * softplus
        a = jnp.exp(log_a)
        a_s[...] = a
        b_s[...] = jnp.sqrt(1.0 - a * a) * (ig * xc)

    prep(xf_ref, xfp_ref, xfn_ref, i, 0, a_f, b_f)
    prep(xb_ref, xbp_ref, xbn_ref, n - 1 - i, 1, a_b, b_b)

    def body(s, carry):
        hf, hb = carry
        rf = pl.multiple_of(s * B, B)
        hf = a_f[pl.ds(rf, B), :] * hf + b_f[pl.ds(rf, B), :]
        rb = pl.multiple_of((tc - 1 - s) * B, B)
        hb = a_b[pl.ds(rb, B), :] * hb + b_b[pl.ds(rb, B), :]
        for c, cs in enumerate(groups):
            hs_f[c, pl.ds(rf, B), :] = hf[:, cs]
            hs_b[c, pl.ds(rb, B), :] = hb[:, cs]
        return hf, hb

    hf, hb = lax.fori_loop(0, tc, body, (hcar[0], hcar[1]), unroll=8)
    hcar[0] = hf
    hcar[1] = hb
    for c, cs in enumerate(groups):
        for b in range(B):
            hf_ref[b, :, cs] = hs_f[c, pl.ds(b, tc, stride=B), :]
            hb_ref[b, :, cs] = hs_b[c, pl.ds(b, tc, stride=B), :]


def _lru(xr, lw, B, S):
    tc = LRU_CHUNK_ROWS // B
    n = S // tc
    nh = S // LRU_HALO
    hb_per = tc // LRU_HALO
    x3 = xr.reshape(B, S, LRU_WIDTH)
    W = LRU_WIDTH

    main_f = pl.BlockSpec((B, tc, W), lambda i: (0, i, 0))
    prev_f = pl.BlockSpec((B, LRU_HALO, W), lambda i: (0, jnp.maximum(i * hb_per - 1, 0), 0))
    next_f = pl.BlockSpec((B, LRU_HALO, W), lambda i: (0, jnp.minimum((i + 1) * hb_per, nh - 1), 0))
    main_b = pl.BlockSpec((B, tc, W), lambda i: (0, n - 1 - i, 0))
    prev_b = pl.BlockSpec((B, LRU_HALO, W), lambda i: (0, jnp.maximum((n - 1 - i) * hb_per - 1, 0), 0))
    next_b = pl.BlockSpec((B, LRU_HALO, W), lambda i: (0, jnp.minimum((n - i) * hb_per, nh - 1), 0))

    def full(a):
        nd = a.ndim
        return pl.BlockSpec(a.shape, lambda i: (0,) * nd)

    R = tc * B
    ng = W // LANES
    hf, hb = pl.pallas_call(
        functools.partial(_lru_kernel, B=B, tc=tc),
        grid=(n,),
        in_specs=[main_f, prev_f, next_f, main_b, prev_b, next_b,
                  full(lw['conv_w']), full(lw['conv_b']), full(lw['rg_w']), full(lw['rg_b']), full(lw['rg_lam'])],
        out_specs=(main_f, main_b),
        out_shape=(jax.ShapeDtypeStruct((B, S, W), F32),) * 2,
        scratch_shapes=[pltpu.VMEM((ng, R + (CONV_WIDTH - 1) * B, LANES), F32)] + [pltpu.VMEM((R, W), F32)] * 4
        + [pltpu.VMEM((ng, R, LANES), F32)] * 2 + [pltpu.VMEM((2, B, W), F32)],
        compiler_params=_cparams(("arbitrary",)), name="rglru",
    )(x3, x3, x3, x3, x3, x3, lw['conv_w'], lw['conv_b'], lw['rg_w'], lw['rg_b'], lw['rg_lam'])
    return hf.reshape(B * S, W), hb.reshape(B * S, W)


BAND_HALF = 64
DIL_TILE = 1024
DIL_QBLOCK = 128
DIL_STACK_ROWS = 256


def _dil_kernel(q_ref, kp_ref, k_ref, kn_ref, vp_ref, v_ref, vn_ref, o_ref,
                qs, kcat, vcat, acc_m, acc_l, acc_o, *, S):
    TT = DIL_TILE
    tile_start = pl.program_id(1) * TT
    groups = _lane_groups(DIL_WIDTH)
    for c, cs in enumerate(groups):
        qs[c] = q_ref[0, :, cs].astype(F32)
        for j, (kr, vr) in enumerate(((kp_ref, vp_ref), (k_ref, v_ref), (kn_ref, vn_ref))):
            kcat[c, j * TT:(j + 1) * TT, :] = kr[0, :, cs].astype(F32)
            vcat[c, j * TT:(j + 1) * TT, :] = vr[0, :, cs].astype(F32)
    head = lax.broadcasted_iota(jnp.int32, (1, DIL_WIDTH), 1) // DIL_HEAD_DIM

    order = sorted((dil for _, dil in DIL_PATTERNS), reverse=True)
    for dil in order:
        first = dil == order[0]
        nq = min(DIL_QBLOCK, TT // dil)
        nsub = TT // dil // nq
        nk = nq + 2 * BAND_HALF
        row = lax.broadcasted_iota(jnp.int32, (nq, nk), 0)
        col = lax.broadcasted_iota(jnp.int32, (nq, nk), 1)
        band = jnp.abs(col - BAND_HALF - row) <= BAND_HALF

        def rows(start, n, dil=dil):
            return pl.ds(pl.multiple_of(start, SUBLANES), n) if dil == 1 else pl.ds(start, n, stride=dil)

        def ld(ref, idx):
            return jnp.concatenate([ref[c, idx, :] for c in range(len(groups))], axis=-1)

        def st(ref, idx, val):
            for c, cs in enumerate(groups):
                ref[c, idx, :] = val[:, cs]

        def block(it, carry, dil=dil, nq=nq, nsub=nsub, nk=nk, band=band, col=col, rows=rows, first=first):
            r, j = (0, it) if dil == 1 else (it % dil, it // dil)
            off = j * nq * dil + r
            qi = rows(off, nq)
            ki = rows(off + TT - BAND_HALF * dil, nk)
            kpos = tile_start + off - BAND_HALF * dil + col * dil
            valid = band & (kpos >= 0) & (kpos < S)
            q = ld(qs, qi).astype(BF16)
            k = ld(kcat, ki).astype(BF16)
            v = ld(vcat, ki).astype(BF16)
            m_b = jnp.zeros((nq, DIL_WIDTH), F32)
            l_b = jnp.zeros((nq, DIL_WIDTH), F32)
            o_b = jnp.zeros((nq, DIL_WIDTH), F32)
            per = min(DIL_HEADS, max(1, DIL_STACK_ROWS // nq))
            groups_h = [range(h0, h0 + per) for h0 in range(0, DIL_HEADS, per)]
            valid_g = jnp.concatenate([valid] * per, axis=0)
            scores = [_dot_nt(jnp.concatenate([jnp.where(head == h, q, jnp.zeros_like(q)) for h in hs], axis=0), k)
                      for hs in groups_h]
            stats = []
            for s in scores:
                s = jnp.where(valid_g, s, -1e30)
                m = jnp.max(s, axis=-1, keepdims=True)
                e = jnp.exp2(s - m)
                stats.append((m, jnp.sum(e, axis=-1, keepdims=True), e.astype(BF16)))
            for hs, (m, l, e) in zip(groups_h, stats):
                o = _dot(e, v)
                for n_, h in enumerate(hs):
                    hm = head == h
                    sl = slice(n_ * nq, (n_ + 1) * nq)
                    m_b = jnp.where(hm, m[sl], m_b)
                    l_b = jnp.where(hm, l[sl], l_b)
                    o_b = jnp.where(hm, o[sl], o_b)
            if first:
                st(acc_l, qi, l_b)
                st(acc_o, qi, o_b)
                st(acc_m, qi, m_b)
                return carry
            m_old = ld(acc_m, qi)
            m_new = jnp.maximum(m_old, m_b)
            w_old = jnp.exp2(m_old - m_new)
            w_b = jnp.exp2(m_b - m_new)
            st(acc_l, qi, w_old * ld(acc_l, qi) + w_b * l_b)
            st(acc_o, qi, w_old * ld(acc_o, qi) + w_b * o_b)
            st(acc_m, qi, m_new)
            return carry

        lax.fori_loop(0, dil * nsub, block, 0, unroll=8 if nq < DIL_QBLOCK else 2)

    for c, cs in enumerate(groups):
        o_ref[0, :, cs] = (acc_o[c] / acc_l[c]).astype(BF16)


def _dilated(q, k, v, B, S):
    TT = DIL_TILE
    nt = S // TT
    W = DIL_WIDTH
    ng = W // LANES
    q3, k3, v3 = (a.reshape(B, S, W) for a in (q, k, v))
    own = pl.BlockSpec((1, TT, W), lambda b, i: (b, i, 0))
    prev = pl.BlockSpec((1, TT, W), lambda b, i: (b, jnp.maximum(i - 1, 0), 0))
    nxt = pl.BlockSpec((1, TT, W), lambda b, i: (b, jnp.minimum(i + 1, nt - 1), 0))
    o = pl.pallas_call(
        functools.partial(_dil_kernel, S=S), grid=(B, nt),
        in_specs=[own, prev, own, nxt, prev, own, nxt], out_specs=own,
        out_shape=jax.ShapeDtypeStruct((B, S, W), BF16),
        scratch_shapes=[pltpu.VMEM((ng, TT, LANES), F32), pltpu.VMEM((ng, 3 * TT, LANES), F32),
                        pltpu.VMEM((ng, 3 * TT, LANES), F32)] + [pltpu.VMEM((ng, TT, LANES), F32)] * 3,
        compiler_params=_cparams(("parallel", "parallel")), name="dilated_attn",
    )(q3, k3, k3, k3, v3, v3, v3)
    return o.reshape(B * S, W)


MLA_KEY_CHUNK = 256


def _mla_kernel(q_ref, k_ref, v_ref, o_ref, vt):
    S = k_ref.shape[1]
    tq = q_ref.shape[1]
    ck = MLA_KEY_CHUNK
    chunks = [slice(c * ck, (c + 1) * ck) for c in range(S // ck)]

    @pl.when(pl.program_id(1) == 0)
    def _():
        vt[...] = v_ref[0].T

    def fold(x, op):
        return op(x.reshape(ck // SUBLANES, SUBLANES, tq), axis=0)

    def score_pass(h):
        lo, hi = h * MLA_HEAD_PAD, (h + 1) * MLA_HEAD_PAD
        q_h = q_ref[0, :, lo:hi]
        m8 = None
        parts = []
        for cs in chunks:
            s_c = _dot_nt(k_ref[0, cs, lo:hi], q_h)
            parts.append(s_c)
            m8 = fold(s_c, jnp.max) if m8 is None else jnp.maximum(m8, fold(s_c, jnp.max))
        return parts, jnp.max(m8, axis=0, keepdims=True)

    def value_pass(h, sm):
        parts, m = sm
        acc = jnp.zeros((MLA_V, tq), F32)
        l8 = jnp.zeros((SUBLANES, tq), F32)
        for cs, s_c in zip(chunks, parts):
            e = jnp.exp2(s_c - m)
            l8 = l8 + fold(e, jnp.sum)
            acc = acc + _dot(vt[h * MLA_V:(h + 1) * MLA_V, cs], e.astype(BF16))
        return acc * (1.0 / jnp.sum(l8, axis=0, keepdims=True))

    m_next = score_pass(0)
    outs = []
    for h in range(MLA_HEADS):
        m = m_next
        if h + 1 < MLA_HEADS:
            m_next = score_pass(h + 1)
        outs.append(value_pass(h, m))
        if h % 2 == 1:
            pair = h // 2
            o_ref[0, :, pair * LANES:(pair + 1) * LANES] = jnp.concatenate(outs[h - 1:h + 1], axis=0).T.astype(BF16)


def _mla(qm, km, vm, B, S, tq):
    QW = MLA_HEADS * MLA_HEAD_PAD
    q3, k3, v3 = qm.reshape(B, S, QW), km.reshape(B, S, QW), vm.reshape(B, S, MLA_WIDTH)
    o = pl.pallas_call(
        _mla_kernel, grid=(B, S // tq),
        in_specs=[pl.BlockSpec((1, tq, QW), lambda b, i: (b, i, 0)),
                  pl.BlockSpec((1, S, QW), lambda b, i: (b, 0, 0)),
                  pl.BlockSpec((1, S, MLA_WIDTH), lambda b, i: (b, 0, 0))],
        out_specs=pl.BlockSpec((1, tq, MLA_WIDTH), lambda b, i: (b, i, 0)),
        out_shape=jax.ShapeDtypeStruct((B, S, MLA_WIDTH), BF16),
        scratch_shapes=[pltpu.VMEM((MLA_WIDTH, S), BF16)],
        compiler_params=_cparams(("parallel", "arbitrary")), name="mla_attn",
    )(q3, k3, v3)
    return o.reshape(B * S, MLA_WIDTH)


def _mix_out_kernel(*refs, moe, final):
    (x_ref, hf_ref, hb_ref, gy_ref, dil_ref, mo_ref, gl_ref, gd_ref, gm_ref, wout_ref, gffn_ref) = refs[:11]
    if moe:
        rw_ref, x1_ref, h2_ref, route_ref, cnt_ref, carry = refs[11:]
    elif final:
        wg_ref, wu_ref, wd_ref, gfin_ref, o_ref = refs[11:]
    else:
        wg_ref, wu_ref, wd_ref, o_ref = refs[11:]
    lru = (hf_ref[...] + hb_ref[...]) * gy_ref[...].astype(F32)
    mix = jnp.concatenate([
        _rms(lru, gl_ref[...], LRU_WIDTH), _rms(dil_ref[...].astype(F32), gd_ref[...], DIL_WIDTH),
        _rms(mo_ref[...].astype(F32), gm_ref[...], MLA_WIDTH)], axis=-1).astype(BF16)
    x1 = x_ref[...] + _dot(mix, wout_ref[...])
    h2 = _rms(x1, gffn_ref[...], D_MODEL)
    if not moe:
        h = h2.astype(BF16)
        g = _dot(h, wg_ref[...])
        u = _dot(h, wu_ref[...])
        out = x1 + _dot((g * jax.nn.sigmoid(g) * u).astype(BF16), wd_ref[...])
        o_ref[...] = _rms(out, gfin_ref[...], D_MODEL) if final else out
        return
    x1_ref[...] = x1
    h2_ref[...] = h2
    if moe:
        first = (pl.program_id(0) == 0) & (pl.program_id(1) == 0)

        @pl.when(first)
        def _():
            carry[...] = jnp.zeros_like(carry)

        h_hi = h2.astype(BF16)
        h_lo = (h2 - h_hi.astype(F32)).astype(BF16)
        logits = _dot(h_hi, rw_ref[0]) + (_dot(h_hi, rw_ref[1]) + _dot(h_lo, rw_ref[0]))
        tm = logits.shape[0]
        lane = lax.broadcasted_iota(jnp.int32, logits.shape, 1)
        neg = -jnp.inf
        lg = jnp.where(lane < N_EXPERTS, logits, neg)
        m1 = jnp.max(lg, axis=-1, keepdims=True)
        i1 = jnp.min(jnp.where(lg == m1, lane, LANES), axis=-1, keepdims=True)
        lg2 = jnp.where(lane == i1, neg, lg)
        m2 = jnp.max(lg2, axis=-1, keepdims=True)
        i2 = jnp.min(jnp.where(lg2 == m2, lane, LANES), axis=-1, keepdims=True)
        e2 = jnp.exp(m2 - m1)
        den = 1.0 + e2
        sel1, sel2 = lane == i1, lane == i2
        onehot = jnp.where(sel1 | sel2, 1.0, 0.0)
        tri = (lax.broadcasted_iota(jnp.int32, (tm, tm), 0) > lax.broadcasted_iota(jnp.int32, (tm, tm), 1))
        before = _dot(jnp.where(tri, 1.0, 0.0).astype(BF16), onehot.astype(BF16)) + carry[0:1, :]
        rank1 = jnp.sum(jnp.where(sel1, before, 0.0), axis=-1, keepdims=True)
        rank2 = jnp.sum(jnp.where(sel2, before, 0.0), axis=-1, keepdims=True)
        carry[...] = carry[...] + jnp.sum(onehot, axis=0, keepdims=True)
        cnt_ref[...] = carry[...]
        cols = (i1.astype(F32), i2.astype(F32), 1.0 / den, e2 / den, rank1, rank2)
        route = jnp.zeros(logits.shape, F32)
        for c, val in enumerate(cols):
            route = jnp.where(lane == c, val, route)
        route_ref[...] = route


def _mix_out(x2d, hf, hb, gy, dil, mo, lw, B, S, tm, moe, gfin=None):
    T = B * S
    ns = S // tm
    tok = lambda b, s: (b * ns + s, 0)
    const = lambda b, s: (0, 0)
    tokspec = lambda c: pl.BlockSpec((tm, c), tok)
    full = lambda a: pl.BlockSpec(a.shape, const)
    once = lambda a: pl.BlockSpec(a.shape, const, pipeline_mode=pl.Buffered(1))
    args = [x2d, hf, hb, gy, dil, mo, lw['g_l'], lw['g_d'], lw['g_m'], lw['w_out'], lw['g_ffn']]
    in_specs = [tokspec(D_MODEL), tokspec(LRU_WIDTH), tokspec(LRU_WIDTH), tokspec(LRU_WIDTH), tokspec(DIL_WIDTH),
                tokspec(MLA_WIDTH), full(lw['g_l']), full(lw['g_d']), full(lw['g_m']), once(lw['w_out']),
                full(lw['g_ffn'])]
    scratch = []
    if moe:
        args.append(lw['router_w'])
        in_specs.append(pl.BlockSpec(lw['router_w'].shape, lambda b, s: (0, 0, 0)))
        out_shape = [jax.ShapeDtypeStruct((T, D_MODEL), F32), jax.ShapeDtypeStruct((T, D_MODEL), F32),
                     jax.ShapeDtypeStruct((T, LANES), F32), jax.ShapeDtypeStruct((SUBLANES, LANES), F32)]
        out_specs = [tokspec(D_MODEL), tokspec(D_MODEL), tokspec(LANES), pl.BlockSpec((SUBLANES, LANES), const)]
        scratch = [pltpu.VMEM((SUBLANES, LANES), F32)]
    else:
        ffn_w = [lw['ffn_wg'], lw['ffn_wu'], lw['ffn_wd']]
        args += ffn_w + ([gfin] if gfin is not None else [])
        in_specs += [once(w) for w in ffn_w] + ([full(gfin)] if gfin is not None else [])
        out_shape = [jax.ShapeDtypeStruct((T, D_MODEL), F32)]
        out_specs = [tokspec(D_MODEL)]
    sem = ("arbitrary", "arbitrary") if moe else ("parallel", "parallel")
    res = pl.pallas_call(
        functools.partial(_mix_out_kernel, moe=moe, final=gfin is not None and not moe), grid=(B, ns),
        in_specs=in_specs, out_specs=tuple(out_specs), out_shape=tuple(out_shape), scratch_shapes=scratch,
        compiler_params=_cparams(sem), name="mix_out_moe" if moe else "mix_out_ffn",
    )(*args)
    return res if moe else res[0]


MOE_BLOCK_ROWS = 512
DISPATCH_TOKENS = 1024
COMBINE_TOKENS = 512


ZERO_BLOCKS = 2 * N_EXPERTS


def _dispatch_kernel(pos_ref, zinfo_ref, h_ref, xs_out, zbuf, sem, zsem, *, tmd):
    bm = MOE_BLOCK_ROWS

    @pl.when(pl.program_id(0) == 0)
    def _():
        zbuf[...] = jnp.zeros_like(zbuf)

        def zero_copy(i):
            start = pl.multiple_of(zinfo_ref[0, 0, i] * bm, bm)
            return pltpu.make_async_copy(zbuf, xs_out.at[pl.ds(start, bm), :], zsem)

        for i in range(ZERO_BLOCKS):
            @pl.when(zinfo_ref[0, 0, ZERO_BLOCKS + i] != 0)
            def _():
                zero_copy(i).start()
        for i in range(ZERO_BLOCKS):
            @pl.when(zinfo_ref[0, 0, ZERO_BLOCKS + i] != 0)
            def _():
                zero_copy(i).wait()

    def body(t, c):
        src = h_ref.at[pl.ds(t, 1), :]
        for k in range(TOP_K):
            pltpu.make_async_copy(src, xs_out.at[pl.ds(pos_ref[0, 0, TOP_K * t + k], 1), :], sem).start()
        return c

    lax.fori_loop(0, tmd, body, 0, unroll=4)
    for _ in range(TOP_K):
        pltpu.make_async_copy(h_ref, xs_out.at[pl.ds(0, tmd), :], sem).wait()


def _dispatch(h2, pos, zinfo, n_rows):
    T = h2.shape[0]
    tmd = DISPATCH_TOKENS
    pos3 = pos.reshape(T // tmd, 1, TOP_K * tmd)
    return pl.pallas_call(
        functools.partial(_dispatch_kernel, tmd=tmd), grid=(T // tmd,),
        in_specs=[pl.BlockSpec((1, 1, TOP_K * tmd), lambda i: (i, 0, 0), memory_space=pltpu.SMEM),
                  pl.BlockSpec((1, 1, 2 * ZERO_BLOCKS), lambda i: (0, 0, 0), memory_space=pltpu.SMEM),
                  pl.BlockSpec((tmd, D_MODEL), lambda i: (i, 0))],
        out_specs=pl.BlockSpec(memory_space=pl.ANY),
        out_shape=jax.ShapeDtypeStruct((n_rows, D_MODEL), F32),
        scratch_shapes=[pltpu.VMEM((MOE_BLOCK_ROWS, D_MODEL), F32), pltpu.SemaphoreType.DMA(()),
                        pltpu.SemaphoreType.DMA(())],
        compiler_params=_cparams(("arbitrary",)), name="moe_dispatch",
    )(pos3, zinfo.reshape(1, 1, 2 * ZERO_BLOCKS), h2)


def _gmm_kernel(be_ref, used_ref, xs_ref, wg_ref, wu_ref, wd_ref, y_ref, xb):
    del be_ref
    f = pl.program_id(1)
    used = used_ref[pl.program_id(0)] != 0

    @pl.when(used)
    def _():
        @pl.when(f == 0)
        def _():
            xb[...] = xs_ref[...].astype(BF16)

        h = xb[...]
        g = _dot(h, wg_ref[0])
        u = _dot(h, wu_ref[0])
        part = _dot((g * jax.nn.sigmoid(g) * u).astype(BF16), wd_ref[0])

        @pl.when(f == 0)
        def _():
            y_ref[...] = part

        @pl.when(f > 0)
        def _():
            y_ref[...] += part

    @pl.when(jnp.logical_not(used) & (f == 0))
    def _():
        y_ref[...] = jnp.zeros_like(y_ref)


def _gmm(xs, block_expert, block_used, wg, wu, wd, tf):
    P = xs.shape[0]
    bm = MOE_BLOCK_ROWS
    F = wg.shape[2]
    grid_spec = pltpu.PrefetchScalarGridSpec(
        num_scalar_prefetch=2, grid=(P // bm, F // tf),
        in_specs=[pl.BlockSpec((bm, D_MODEL), lambda j, f, be, bu: (j, 0)),
                  pl.BlockSpec((1, D_MODEL, tf), lambda j, f, be, bu: (be[j], 0, f)),
                  pl.BlockSpec((1, D_MODEL, tf), lambda j, f, be, bu: (be[j], 0, f)),
                  pl.BlockSpec((1, tf, D_MODEL), lambda j, f, be, bu: (be[j], f, 0))],
        out_specs=pl.BlockSpec((bm, D_MODEL), lambda j, f, be, bu: (j, 0)),
        scratch_shapes=[pltpu.VMEM((bm, D_MODEL), BF16)])
    return pl.pallas_call(
        _gmm_kernel, grid_spec=grid_spec, out_shape=jax.ShapeDtypeStruct((P, D_MODEL), F32),
        compiler_params=_cparams(("parallel", "arbitrary")), name="moe_gmm",
    )(block_expert, block_used, xs, wg, wu, wd)


def _combine_kernel(*refs, tmc, final):
    if final:
        pos_ref, route_ref, x1_ref, y_hbm, gfin_ref, o_ref, ybuf, sem = refs
    else:
        pos_ref, route_ref, x1_ref, y_hbm, o_ref, ybuf, sem = refs

    def body(t, c):
        for k in range(TOP_K):
            pltpu.make_async_copy(y_hbm.at[pl.ds(pos_ref[0, 0, TOP_K * t + k], 1), :],
                                  ybuf.at[k, pl.ds(t, 1), :], sem).start()
        return c

    lax.fori_loop(0, tmc, body, 0, unroll=4)
    for k in range(TOP_K):
        pltpu.make_async_copy(y_hbm.at[pl.ds(0, tmc), :], ybuf.at[k], sem).wait()
    route = route_ref[...]
    out = x1_ref[...] + route[:, 2:3] * ybuf[0] + route[:, 3:4] * ybuf[1]
    if final:
        out = _rms(out, gfin_ref[...], D_MODEL)
    o_ref[...] = out


def _combine(pos, route, x1, y, gfin):
    T = x1.shape[0]
    tmc = COMBINE_TOKENS
    final = gfin is not None
    pos3 = pos.reshape(T // tmc, 1, TOP_K * tmc)
    args = [pos3, route, x1, y] + ([gfin] if final else [])
    in_specs = [pl.BlockSpec((1, 1, TOP_K * tmc), lambda i: (i, 0, 0), memory_space=pltpu.SMEM),
                pl.BlockSpec((tmc, LANES), lambda i: (i, 0)), pl.BlockSpec((tmc, D_MODEL), lambda i: (i, 0)),
                pl.BlockSpec(memory_space=pl.ANY)]
    if final:
        in_specs.append(pl.BlockSpec(gfin.shape, lambda i: (0, 0)))
    return pl.pallas_call(
        functools.partial(_combine_kernel, tmc=tmc, final=final), grid=(T // tmc,), in_specs=in_specs,
        out_specs=pl.BlockSpec((tmc, D_MODEL), lambda i: (i, 0)),
        out_shape=jax.ShapeDtypeStruct((T, D_MODEL), F32),
        scratch_shapes=[pltpu.VMEM((TOP_K, tmc, D_MODEL), F32), pltpu.SemaphoreType.DMA(())],
        compiler_params=_cparams(("arbitrary",)), name="moe_combine",
    )(*args)


def _moe(h2, x1, route, counts, wg, wu, wd, gfin, tf):
    T = h2.shape[0]
    bm = MOE_BLOCK_ROWS
    n_blocks = TOP_K * T // bm + N_EXPERTS
    cnt = counts[0, :N_EXPERTS].astype(jnp.int32)
    padded = (cnt + bm - 1) // bm * bm
    ends = jnp.cumsum(padded)
    starts = ends - padded
    experts = route[:, 0:TOP_K].astype(jnp.int32)
    pos = starts[experts] + route[:, 4:4 + TOP_K].astype(jnp.int32)
    block_start = jnp.arange(n_blocks, dtype=jnp.int32) * bm
    block_expert = jnp.minimum(jnp.sum((block_start[:, None] >= ends[None, :]).astype(jnp.int32), axis=1),
                               N_EXPERTS - 1)
    tail = n_blocks - N_EXPERTS + jnp.arange(N_EXPERTS, dtype=jnp.int32)
    zinfo = jnp.concatenate([ends // bm - 1, tail, (padded > 0).astype(jnp.int32),
                             (tail * bm >= ends[-1]).astype(jnp.int32)]).astype(jnp.int32)
    xs = _dispatch(h2, pos, zinfo, n_blocks * bm)
    y = _gmm(xs, block_expert, (block_start < ends[-1]).astype(jnp.int32), wg, wu, wd, tf)
    return _combine(pos, route, x1, y, gfin)


def _pad_heads(w, heads, dim, at=0):
    k = w.shape[0]
    w = w.reshape(k, heads, dim)
    w = jnp.pad(w, ((0, 0), (0, 0), (at, MLA_HEAD_PAD - at - dim)))
    return w.reshape(k, heads * MLA_HEAD_PAD)


def _block_diag(w):
    nb, bs, _ = w.shape
    eye = jnp.eye(nb, dtype=w.dtype)
    return (eye[:, None, :, None] * w[:, :, None, :]).reshape(nb * bs, nb * bs)


def _rope_tables(S):
    def cs(dim):
        half = dim // 2
        freqs = jnp.power(jnp.float32(ROPE_THETA), -jnp.arange(half, dtype=F32) * 2.0 / dim)
        ang = jnp.arange(S, dtype=F32)[:, None] * freqs[None, :]
        c, s = jnp.cos(ang), jnp.sin(ang)
        return jnp.concatenate([c, c], -1), jnp.concatenate([-s, s], -1)

    cd, sd = cs(DIL_HEAD_DIM)
    cm, sm = cs(MLA_ROPE)
    ones = jnp.ones((S, MLA_NOPE), F32)
    pad = MLA_HEAD_PAD - MLA_NOPE - MLA_ROPE
    return dict(
        cosd=jnp.tile(cd, (1, DIL_HEADS)), sind=jnp.tile(sd, (1, DIL_HEADS)),
        cosm=jnp.concatenate([ones, cm, jnp.ones((S, pad), F32)], -1),
        sinm=jnp.concatenate([0.0 * ones, sm, jnp.zeros((S, pad), F32)], -1))


def _prep_layer(l, p):
    w_in = p['w_in'][l]
    offs = [0, 256, 512, 768, 1024, 1280, 1536, 1664, 1696]
    xr, yr, qd, kd, vd, cq, ckv, kr = [w_in[:, a:b] for a, b in zip(offs[:-1], offs[1:])]
    kr_pad = _pad_heads(kr, 1, MLA_ROPE, at=MLA_NOPE)
    w1 = jnp.concatenate([xr, yr, qd, kd, vd, cq, ckv, kr_pad], axis=1)
    w_uq = p['mla_w_uq'][l].reshape(MLA_Q_RANK, MLA_HEADS, MLA_NOPE + MLA_ROPE)
    qn, qr = w_uq[:, :, :MLA_NOPE], w_uq[:, :, MLA_NOPE:]
    qr_sw = qr.reshape(MLA_Q_RANK, MLA_HEADS, 2, MLA_ROPE // 2)[:, :, ::-1, :].reshape(qr.shape)
    flat = lambda a, b: jnp.concatenate([a, b], -1).reshape(MLA_Q_RANK, MLA_HEADS * (MLA_NOPE + MLA_ROPE))
    w_uq2 = jnp.concatenate([_pad_heads(flat(qn, qr), MLA_HEADS, MLA_NOPE + MLA_ROPE),
                             _pad_heads(flat(jnp.zeros_like(qn), qr_sw), MLA_HEADS, MLA_NOPE + MLA_ROPE)], axis=1)

    w_ukv = p['mla_w_ukv'][l].reshape(MLA_KV_RANK, MLA_HEADS, MLA_NOPE + MLA_V)
    kn = _pad_heads(w_ukv[:, :, :MLA_NOPE].reshape(MLA_KV_RANK, -1), MLA_HEADS, MLA_NOPE)
    vm = w_ukv[:, :, MLA_NOPE:].reshape(MLA_KV_RANK, MLA_WIDTH)
    w_ukv2 = jnp.concatenate([kn, vm], axis=1)

    rg_w = jnp.stack([jnp.concatenate([_block_diag(p['rg_w_a'][l, d]), _block_diag(p['rg_w_i'][l, d])], axis=1)
                      for d in range(2)])
    rg_b = jnp.stack([jnp.concatenate([p['rg_b_a'][l, d], p['rg_b_i'][l, d]])[None, :] for d in range(2)])
    g_out = p['mix_out_norm_g'][l]
    row = lambda v: v.reshape(1, -1).astype(F32)
    return dict(
        g_mix=row(p['norm_mix_g'][l]), w1=w1.astype(BF16), g_q=row(p['mla_q_norm_g'][l]), w_uq=w_uq2.astype(BF16),
        g_kv=row(p['mla_kv_norm_g'][l]), w_ukv=w_ukv2.astype(BF16),
        conv_w=p['conv_w'][l].astype(F32), conv_b=row(p['conv_b'][l]), rg_w=rg_w.astype(BF16), rg_b=rg_b.astype(F32),
        rg_lam=p['rg_lambda'][l].reshape(2, 1, LRU_WIDTH).astype(F32),
        g_l=row(g_out[:LRU_WIDTH]), g_d=row(g_out[LRU_WIDTH:LRU_WIDTH + DIL_WIDTH]),
        g_m=row(g_out[LRU_WIDTH + DIL_WIDTH:]), w_out=p['w_out'][l].astype(BF16), g_ffn=row(p['norm_ffn_g'][l]))


def _mixers(x2d, lw, tabs, B, S, tm, moe, gfin=None):
    xr, gy, q, k, v, qm, km, vm = _in_proj(x2d, lw, tabs, B, S, tm)
    hf, hb = _lru(xr, lw, B, S)
    dil = _dilated(q, k, v, B, S)
    mo = _mla(qm, km, vm, B, S, 256)
    return _mix_out(x2d, hf, hb, gy, dil, mo, lw, B, S, 2 * tm if moe else tm, moe, gfin)


def _trunk(x, layers, p, depth):
    B, S, _ = x.shape
    tabs = _rope_tables(S)
    x2d = x.reshape(B * S, D_MODEL)
    tm = 512
    for l in range(depth):
        lw = layers[l]
        moe = l % 2 == 1
        last = l == depth - 1
        gfin = p['final_norm_g'].reshape(1, D_MODEL) if last else None
        j = l // 2
        if moe:
            rw = jnp.pad(p['router_w'][j].astype(F32), ((0, 0), (0, LANES - N_EXPERTS)))
            rw_hi = rw.astype(BF16)
            lw = dict(lw, router_w=jnp.stack([rw_hi, (rw - rw_hi.astype(F32)).astype(BF16)]))
            x1, h2, route, counts = _mixers(x2d, lw, tabs, B, S, tm, True)
            x2d = _moe(h2, x1, route, counts, lw['moe_wg'], lw['moe_wu'], lw['moe_wd'], gfin, 1792)
        else:
            x2d = _mixers(x2d, lw, tabs, B, S, tm, False, gfin)
    return x2d.reshape(B, S, D_MODEL)


def kernel(x_prompt, x_sample, norm_mix_g, w_in, conv_w, conv_b, rg_w_a, rg_b_a, rg_w_i, rg_b_i, rg_lambda, mla_q_norm_g, mla_w_uq, mla_kv_norm_g, mla_w_ukv, mix_out_norm_g, w_out, norm_ffn_g, ffn_w_gate, ffn_w_up, ffn_w_down, router_w, moe_w_gate, moe_w_up, moe_w_down, final_norm_g):
    p = dict(norm_mix_g=norm_mix_g, w_in=w_in, conv_w=conv_w, conv_b=conv_b, rg_w_a=rg_w_a, rg_b_a=rg_b_a,
             rg_w_i=rg_w_i, rg_b_i=rg_b_i, rg_lambda=rg_lambda, mla_q_norm_g=mla_q_norm_g, mla_w_uq=mla_w_uq,
             mla_kv_norm_g=mla_kv_norm_g, mla_w_ukv=mla_w_ukv, mix_out_norm_g=mix_out_norm_g, w_out=w_out,
             norm_ffn_g=norm_ffn_g, router_w=router_w, final_norm_g=final_norm_g)
    depth = w_in.shape[0]
    layers = []
    for l in range(depth):
        lw = _prep_layer(l, p)
        j = l // 2
        if l % 2 == 1:
            lw.update(moe_wg=moe_w_gate[j].astype(BF16), moe_wu=moe_w_up[j].astype(BF16),
                      moe_wd=moe_w_down[j].astype(BF16))
        else:
            lw.update(ffn_wg=ffn_w_gate[j].astype(BF16), ffn_wu=ffn_w_up[j].astype(BF16),
                      ffn_wd=ffn_w_down[j].astype(BF16))
        layers.append(lw)
    return (_trunk(x_prompt, layers, p, depth), _trunk(x_sample, layers, p, depth))
```

```python
import functools
import math

import jax
import jax.numpy as jnp
from jax import lax
from jax.experimental import pallas as pl
from jax.experimental.pallas import tpu as pltpu

F32 = jnp.float32
BF16 = jnp.bfloat16

D_MODEL = 1024
NORM_EPS = 1e-6
ROPE_THETA = 10000.0
LRU_WIDTH = 256
LRU_BLOCKS = 4
CONV_WIDTH = 4
LRU_C = 8.0
DIL_HEADS = 4
DIL_HEAD_DIM = 64
DIL_WIDTH = DIL_HEADS * DIL_HEAD_DIM
DIL_PATTERNS = ((128, 1), (512, 4), (2048, 16))
MLA_HEADS = 8
MLA_NOPE = 64
MLA_ROPE = 32
MLA_V = 64
MLA_Q_RANK = 256
MLA_KV_RANK = 128
MLA_WIDTH = MLA_HEADS * MLA_V
N_EXPERTS = 8
TOP_K = 2

LANES = 128
SUBLANES = 8
MLA_HEAD_PAD = LANES
VMEM_LIMIT = 56 * 1024 * 1024
LOG2E = math.log2(math.e)

_C_XR, _C_YR, _C_Q, _C_K, _C_V, _C_CQ, _C_CKV, _C_KR, _C_END = (0, 256, 512, 768, 1024, 1280, 1536, 1664, 1792)


def _cparams(sem):
    return pltpu.CompilerParams(dimension_semantics=sem, vmem_limit_bytes=VMEM_LIMIT)


def _rms(x, g, n):
    ms = jnp.sum(x * x, axis=-1, keepdims=True) * (1.0 / n)
    return x * lax.rsqrt(ms + NORM_EPS) * g


def _gelu_tanh(x):
    return 0.5 * x * (1.0 + jnp.tanh(0.7978845608028654 * (x + 0.044715 * (x * x * x))))


def _dot(a, b):
    return jnp.dot(a, b, preferred_element_type=F32)


def _dot_nt(a, b):
    return lax.dot_general(a, b, (((1,), (1,)), ((), ())), preferred_element_type=F32)


def _lane_groups(width):
    return [slice(c * LANES, (c + 1) * LANES) for c in range(width // LANES)]


def _swap_rotary_halves(x, half, period, start):
    lane = lax.broadcasted_iota(jnp.int32, (1, LANES), 1) % period
    first = lane < start + half
    outs = []
    for cs in _lane_groups(x.shape[1]):
        xs = x[:, cs]
        outs.append(jnp.where(first, pltpu.roll(xs, LANES - half, 1), pltpu.roll(xs, half, 1)))
    return outs[0] if len(outs) == 1 else jnp.concatenate(outs, axis=-1)


def _in_proj_kernel(x_ref, g_ref, w1_ref, gq_ref, wuq_ref, gkv_ref, wukv_ref,
                    cosd_ref, sind_ref, cosm_ref, sinm_ref,
                    xr_ref, gy_ref, q_ref, k_ref, v_ref, qm_ref, km_ref, vm_ref):
    h = _rms(x_ref[...], g_ref[...], D_MODEL).astype(BF16)
    p = _dot(h, w1_ref[...])
    xr_ref[...] = p[:, _C_XR:_C_YR]
    gy_ref[...] = _gelu_tanh(p[:, _C_YR:_C_Q]).astype(BF16)
    cosd, sind = cosd_ref[...], sind_ref[...]
    half_d = DIL_HEAD_DIM // 2
    q, k = p[:, _C_Q:_C_K], p[:, _C_K:_C_V]
    q = q * cosd + _swap_rotary_halves(q, half_d, DIL_HEAD_DIM, 0) * sind
    q_ref[...] = (q * (DIL_HEAD_DIM ** -0.5 * LOG2E)).astype(BF16)
    k_ref[...] = (k * cosd + _swap_rotary_halves(k, half_d, DIL_HEAD_DIM, 0) * sind).astype(BF16)
    v_ref[...] = p[:, _C_V:_C_CQ].astype(BF16)

    cosm, sinm = cosm_ref[...], sinm_ref[...]
    half_m = MLA_ROPE // 2
    cqn = _rms(p[:, _C_CQ:_C_CKV], gq_ref[...], MLA_Q_RANK).astype(BF16)
    nq = MLA_HEADS * MLA_HEAD_PAD
    qq2 = _dot(cqn, wuq_ref[...])
    qq, qq_sw = qq2[:, :nq], qq2[:, nq:]
    ckvn = _rms(p[:, _C_CKV:_C_KR], gkv_ref[...], MLA_KV_RANK).astype(BF16)
    kv = _dot(ckvn, wukv_ref[...])
    kr = p[:, _C_KR:_C_END]
    kr = kr * cosm + _swap_rotary_halves(kr, half_m, MLA_HEAD_PAD, MLA_NOPE) * sinm
    scale = (MLA_NOPE + MLA_ROPE) ** -0.5 * LOG2E
    for hh in range(MLA_HEADS):
        lo, hi = hh * MLA_HEAD_PAD, (hh + 1) * MLA_HEAD_PAD
        qh = qq[:, lo:hi] * cosm + qq_sw[:, lo:hi] * sinm
        qm_ref[:, lo:hi] = (qh * scale).astype(BF16)
        km_ref[:, lo:hi] = (kv[:, lo:hi] + kr).astype(BF16)
    vm_ref[...] = kv[:, nq:].astype(BF16)


def _in_proj(x2d, lw, tabs, B, S, tm):
    T = B * S
    ns = S // tm
    tok = lambda b, s: (b * ns + s, 0)
    pos = lambda b, s: (s, 0)
    const = lambda b, s: (0, 0)

    def full(a):
        return pl.BlockSpec(a.shape, const)

    def tokspec(c):
        return pl.BlockSpec((tm, c), tok)

    out_shape = (
        jax.ShapeDtypeStruct((T, LRU_WIDTH), F32),
        jax.ShapeDtypeStruct((T, LRU_WIDTH), BF16),
        jax.ShapeDtypeStruct((T, DIL_WIDTH), BF16),
        jax.ShapeDtypeStruct((T, DIL_WIDTH), BF16),
        jax.ShapeDtypeStruct((T, DIL_WIDTH), BF16),
        jax.ShapeDtypeStruct((T, MLA_HEADS * MLA_HEAD_PAD), BF16),
        jax.ShapeDtypeStruct((T, MLA_HEADS * MLA_HEAD_PAD), BF16),
        jax.ShapeDtypeStruct((T, MLA_WIDTH), BF16),
    )
    out_specs = (
        tokspec(LRU_WIDTH), tokspec(LRU_WIDTH), tokspec(DIL_WIDTH), tokspec(DIL_WIDTH), tokspec(DIL_WIDTH),
        tokspec(MLA_HEADS * MLA_HEAD_PAD), tokspec(MLA_HEADS * MLA_HEAD_PAD), tokspec(MLA_WIDTH),
    )
    in_specs = [
        tokspec(D_MODEL), full(lw['g_mix']), full(lw['w1']), full(lw['g_q']), full(lw['w_uq']),
        full(lw['g_kv']), full(lw['w_ukv']),
        pl.BlockSpec((tm, DIL_WIDTH), pos), pl.BlockSpec((tm, DIL_WIDTH), pos),
        pl.BlockSpec((tm, LANES), pos), pl.BlockSpec((tm, LANES), pos),
    ]
    return pl.pallas_call(
        _in_proj_kernel, grid=(B, ns), in_specs=in_specs, out_specs=out_specs, out_shape=out_shape,
        compiler_params=_cparams(("parallel", "parallel")), name="in_proj",
    )(x2d, lw['g_mix'], lw['w1'], lw['g_q'], lw['w_uq'], lw['g_kv'], lw['w_ukv'],
      tabs['cosd'], tabs['sind'], tabs['cosm'], tabs['sinm'])


LRU_CHUNK_ROWS = 1024
LRU_HALO = SUBLANES


def _lru_kernel(xf_ref, xfp_ref, xfn_ref, xb_ref, xbp_ref, xbn_ref, cw_ref, cb_ref, wg_ref, bg_ref,
                lam_ref, hf_ref, hb_ref, xpad, a_f, b_f, a_b, b_b, hs_f, hs_b, hcar, *, B, tc):
    i = pl.program_id(0)
    n = pl.num_programs(0)
    R = tc * B
    left = CONV_WIDTH // 2
    groups = _lane_groups(LRU_WIDTH)

    @pl.when(i == 0)
    def _():
        hcar[...] = jnp.zeros_like(hcar)

    def prep(x_ref, xp_ref, xn_ref, ci, d, a_s, b_s):
        for c, cs in enumerate(groups):
            for b in range(B):
                xpad[c, pl.ds(b, left, stride=B), :] = jnp.where(ci > 0, xp_ref[b, LRU_HALO - left:LRU_HALO, cs], 0.0)
                xpad[c, pl.ds(left * B + b, tc, stride=B), :] = x_ref[b, :, cs]
                xpad[c, pl.ds((left + tc) * B + b, 1), :] = jnp.where(ci < n - 1, xn_ref[b, 0:1, cs], 0.0)
        halves = []
        for c, cs in enumerate(groups):
            xc = cb_ref[:, cs] + xpad[c, 0:R, :] * cw_ref[0:1, cs]
            for kk in range(1, CONV_WIDTH):
                xc = xc + xpad[c, kk * B:kk * B + R, :] * cw_ref[kk:kk + 1, cs]
            halves.append(xc)
        xc = jnp.concatenate(halves, axis=-1)
        g = _dot(xc.astype(BF16), wg_ref[d]) + bg_ref[d]
        r = jax.nn.sigmoid(g[:, :LRU_WIDTH])
        ig = jax.nn.sigmoid(g[:, LRU_WIDTH:])
        lam = lam_ref[d]
        softplus = jnp.maximum(-lam, 0.0) + jnp.log(1.0 + jnp.exp(-jnp.abs(lam)))
        log_a = (-LRU_C) * r * softplus
        a = jnp.exp(log_a)
        a_s[...] = a
        b_s[...] = jnp.sqrt(1.0 - a * a) * (ig * xc)

    prep(xf_ref, xfp_ref, xfn_ref, i, 0, a_f, b_f)
    prep(xb_ref, xbp_ref, xbn_ref, n - 1 - i, 1, a_b, b_b)

    def body(s, carry):
        hf, hb = carry
        rf = pl.multiple_of(s * B, B)
        hf = a_f[pl.ds(rf, B), :] * hf + b_f[pl.ds(rf, B), :]
        rb = pl.multiple_of((tc - 1 - s) * B, B)
        hb = a_b[pl.ds(rb, B), :] * hb + b_b[pl.ds(rb, B), :]
        for c, cs in enumerate(groups):
            hs_f[c, pl.ds(rf, B), :] = hf[:, cs]
            hs_b[c, pl.ds(rb, B), :] = hb[:, cs]
        return hf, hb

    hf, hb = lax.fori_loop(0, tc, body, (hcar[0], hcar[1]), unroll=8)
    hcar[0] = hf
    hcar[1] = hb
    for c, cs in enumerate(groups):
        for b in range(B):
            hf_ref[b, :, cs] = hs_f[c, pl.ds(b, tc, stride=B), :]
            hb_ref[b, :, cs] = hs_b[c, pl.ds(b, tc, stride=B), :]


def _lru(xr, lw, B, S):
    tc = LRU_CHUNK_ROWS // B
    n = S // tc
    nh = S // LRU_HALO
    hb_per = tc // LRU_HALO
    x3 = xr.reshape(B, S, LRU_WIDTH)
    W = LRU_WIDTH

    main_f = pl.BlockSpec((B, tc, W), lambda i: (0, i, 0))
    prev_f = pl.BlockSpec((B, LRU_HALO, W), lambda i: (0, jnp.maximum(i * hb_per - 1, 0), 0))
    next_f = pl.BlockSpec((B, LRU_HALO, W), lambda i: (0, jnp.minimum((i + 1) * hb_per, nh - 1), 0))
    main_b = pl.BlockSpec((B, tc, W), lambda i: (0, n - 1 - i, 0))
    prev_b = pl.BlockSpec((B, LRU_HALO, W), lambda i: (0, jnp.maximum((n - 1 - i) * hb_per - 1, 0), 0))
    next_b = pl.BlockSpec((B, LRU_HALO, W), lambda i: (0, jnp.minimum((n - i) * hb_per, nh - 1), 0))

    def full(a):
        nd = a.ndim
        return pl.BlockSpec(a.shape, lambda i: (0,) * nd)

    R = tc * B
    ng = W // LANES
    hf, hb = pl.pallas_call(
        functools.partial(_lru_kernel, B=B, tc=tc),
        grid=(n,),
        in_specs=[main_f, prev_f, next_f, main_b, prev_b, next_b,
                  full(lw['conv_w']), full(lw['conv_b']), full(lw['rg_w']), full(lw['rg_b']), full(lw['rg_lam'])],
        out_specs=(main_f, main_b),
        out_shape=(jax.ShapeDtypeStruct((B, S, W), F32),) * 2,
        scratch_shapes=[pltpu.VMEM((ng, R + (CONV_WIDTH - 1) * B, LANES), F32)] + [pltpu.VMEM((R, W), F32)] * 4
        + [pltpu.VMEM((ng, R, LANES), F32)] * 2 + [pltpu.VMEM((2, B, W), F32)],
        compiler_params=_cparams(("arbitrary",)), name="rglru",
    )(x3, x3, x3, x3, x3, x3, lw['conv_w'], lw['conv_b'], lw['rg_w'], lw['rg_b'], lw['rg_lam'])
    return hf.reshape(B * S, W), hb.reshape(B * S, W)


BAND_HALF = 64
DIL_TILE = 2048
DIL_QBLOCK = 128
DIL_STACK_ROWS = 256


def _dil_kernel(q_ref, kp_ref, k_ref, kn_ref, vp_ref, v_ref, vn_ref, o_ref,
                qs, kcat, vcat, acc_m, acc_l, acc_o, *, S):
    TT = DIL_TILE
    tile_start = pl.program_id(1) * TT
    groups = _lane_groups(DIL_WIDTH)
    for c, cs in enumerate(groups):
        qs[c] = q_ref[0, :, cs].astype(F32)
        for j, (kr, vr) in enumerate(((kp_ref, vp_ref), (k_ref, v_ref), (kn_ref, vn_ref))):
            kcat[c, j * TT:(j + 1) * TT, :] = kr[0, :, cs].astype(F32)
            vcat[c, j * TT:(j + 1) * TT, :] = vr[0, :, cs].astype(F32)
    head = lax.broadcasted_iota(jnp.int32, (1, DIL_WIDTH), 1) // DIL_HEAD_DIM

    order = sorted((dil for _, dil in DIL_PATTERNS), reverse=True)
    for dil in order:
        first = dil == order[0]
        nq = min(DIL_QBLOCK, TT // dil)
        nsub = TT // dil // nq
        nk = nq + 2 * BAND_HALF
        row = lax.broadcasted_iota(jnp.int32, (nq, nk), 0)
        col = lax.broadcasted_iota(jnp.int32, (nq, nk), 1)
        band = jnp.abs(col - BAND_HALF - row) <= BAND_HALF

        def rows(start, n, dil=dil):
            return pl.ds(pl.multiple_of(start, SUBLANES), n) if dil == 1 else pl.ds(start, n, stride=dil)

        def ld(ref, idx):
            return jnp.concatenate([ref[c, idx, :] for c in range(len(groups))], axis=-1)

        def st(ref, idx, val):
            for c, cs in enumerate(groups):
                ref[c, idx, :] = val[:, cs]

        def block(it, carry, dil=dil, nq=nq, nsub=nsub, nk=nk, band=band, col=col, rows=rows, first=first):
            r, j = (0, it) if dil == 1 else (it % dil, it // dil)
            off = j * nq * dil + r
            qi = rows(off, nq)
            ki = rows(off + TT - BAND_HALF * dil, nk)
            kpos = tile_start + off - BAND_HALF * dil + col * dil
            valid = band & (kpos >= 0) & (kpos < S)
            q = ld(qs, qi).astype(BF16)
            k = ld(kcat, ki).astype(BF16)
            v = ld(vcat, ki).astype(BF16)
            m_b = jnp.zeros((nq, DIL_WIDTH), F32)
            l_b = jnp.zeros((nq, DIL_WIDTH), F32)
            o_b = jnp.zeros((nq, DIL_WIDTH), F32)
            per = min(DIL_HEADS, max(1, DIL_STACK_ROWS // nq))
            groups_h = [range(h0, h0 + per) for h0 in range(0, DIL_HEADS, per)]
            valid_g = jnp.concatenate([valid] * per, axis=0)
            scores = [_dot_nt(jnp.concatenate([jnp.where(head == h, q, jnp.zeros_like(q)) for h in hs], axis=0), k)
                      for hs in groups_h]
            stats = []
            for s in scores:
                s = jnp.where(valid_g, s, -1e30)
                m = jnp.max(s, axis=-1, keepdims=True)
                e = jnp.exp2(s - m)
                stats.append((m, jnp.sum(e, axis=-1, keepdims=True), e.astype(BF16)))
            for hs, (m, l, e) in zip(groups_h, stats):
                o = _dot(e, v)
                for n_, h in enumerate(hs):
                    hm = head == h
                    sl = slice(n_ * nq, (n_ + 1) * nq)
                    m_b = jnp.where(hm, m[sl], m_b)
                    l_b = jnp.where(hm, l[sl], l_b)
                    o_b = jnp.where(hm, o[sl], o_b)
            if first:
                st(acc_l, qi, l_b)
                st(acc_o, qi, o_b)
                st(acc_m, qi, m_b)
                return carry
            m_old = ld(acc_m, qi)
            m_new = jnp.maximum(m_old, m_b)
            w_old = jnp.exp2(m_old - m_new)
            w_b = jnp.exp2(m_b - m_new)
            st(acc_l, qi, w_old * ld(acc_l, qi) + w_b * l_b)
            st(acc_o, qi, w_old * ld(acc_o, qi) + w_b * o_b)
            st(acc_m, qi, m_new)
            return carry

        lax.fori_loop(0, dil * nsub, block, 0, unroll=8 if nq < DIL_QBLOCK else 2)

    for c, cs in enumerate(groups):
        o_ref[0, :, cs] = (acc_o[c] / acc_l[c]).astype(BF16)


def _dilated(q, k, v, B, S):
    TT = DIL_TILE
    nt = S // TT
    W = DIL_WIDTH
    ng = W // LANES
    q3, k3, v3 = (a.reshape(B, S, W) for a in (q, k, v))
    own = pl.BlockSpec((1, TT, W), lambda b, i: (b, i, 0))
    prev = pl.BlockSpec((1, TT, W), lambda b, i: (b, jnp.maximum(i - 1, 0), 0))
    nxt = pl.BlockSpec((1, TT, W), lambda b, i: (b, jnp.minimum(i + 1, nt - 1), 0))
    o = pl.pallas_call(
        functools.partial(_dil_kernel, S=S), grid=(B, nt),
        in_specs=[own, prev, own, nxt, prev, own, nxt], out_specs=own,
        out_shape=jax.ShapeDtypeStruct((B, S, W), BF16),
        scratch_shapes=[pltpu.VMEM((ng, TT, LANES), F32), pltpu.VMEM((ng, 3 * TT, LANES), F32),
                        pltpu.VMEM((ng, 3 * TT, LANES), F32)] + [pltpu.VMEM((ng, TT, LANES), F32)] * 3,
        compiler_params=_cparams(("parallel", "parallel")), name="dilated_attn",
    )(q3, k3, k3, k3, v3, v3, v3)
    return o.reshape(B * S, W)


MLA_KEY_CHUNK = 256


def _mla_kernel(q_ref, k_ref, v_ref, o_ref, vt):
    S = k_ref.shape[1]
    tq = q_ref.shape[1]
    ck = MLA_KEY_CHUNK
    chunks = [slice(c * ck, (c + 1) * ck) for c in range(S // ck)]

    @pl.when(pl.program_id(1) == 0)
    def _():
        vt[...] = v_ref[0].T

    def fold(x, op):
        return op(x.reshape(ck // SUBLANES, SUBLANES, tq), axis=0)

    def score_pass(h):
        lo, hi = h * MLA_HEAD_PAD, (h + 1) * MLA_HEAD_PAD
        q_h = q_ref[0, :, lo:hi]
        m8 = None
        parts = []
        for cs in chunks:
            s_c = _dot_nt(k_ref[0, cs, lo:hi], q_h)
            parts.append(s_c)
            m8 = fold(s_c, jnp.max) if m8 is None else jnp.maximum(m8, fold(s_c, jnp.max))
        return parts, jnp.max(m8, axis=0, keepdims=True)

    def value_pass(h, sm):
        parts, m = sm
        acc = jnp.zeros((MLA_V, tq), F32)
        l8 = jnp.zeros((SUBLANES, tq), F32)
        for cs, s_c in zip(chunks, parts):
            e = jnp.exp2(s_c - m)
            l8 = l8 + fold(e, jnp.sum)
            acc = acc + _dot(vt[h * MLA_V:(h + 1) * MLA_V, cs], e.astype(BF16))
        return acc * (1.0 / jnp.sum(l8, axis=0, keepdims=True))

    m_next = score_pass(0)
    outs = []
    for h in range(MLA_HEADS):
        m = m_next
        if h + 1 < MLA_HEADS:
            m_next = score_pass(h + 1)
        outs.append(value_pass(h, m))
        if h % 2 == 1:
            pair = h // 2
            o_ref[0, :, pair * LANES:(pair + 1) * LANES] = jnp.concatenate(outs[h - 1:h + 1], axis=0).T.astype(BF16)


def _mla(qm, km, vm, B, S, tq):
    QW = MLA_HEADS * MLA_HEAD_PAD
    q3, k3, v3 = qm.reshape(B, S, QW), km.reshape(B, S, QW), vm.reshape(B, S, MLA_WIDTH)
    o = pl.pallas_call(
        _mla_kernel, grid=(B, S // tq),
        in_specs=[pl.BlockSpec((1, tq, QW), lambda b, i: (b, i, 0)),
                  pl.BlockSpec((1, S, QW), lambda b, i: (b, 0, 0)),
                  pl.BlockSpec((1, S, MLA_WIDTH), lambda b, i: (b, 0, 0))],
        out_specs=pl.BlockSpec((1, tq, MLA_WIDTH), lambda b, i: (b, i, 0)),
        out_shape=jax.ShapeDtypeStruct((B, S, MLA_WIDTH), BF16),
        scratch_shapes=[pltpu.VMEM((MLA_WIDTH, S), BF16)],
        compiler_params=_cparams(("parallel", "arbitrary")), name="mla_attn",
    )(q3, k3, v3)
    return o.reshape(B * S, MLA_WIDTH)


def _mix_out_kernel(*refs, moe, final):
    (x_ref, hf_ref, hb_ref, gy_ref, dil_ref, mo_ref, gl_ref, gd_ref, gm_ref, wout_ref, gffn_ref) = refs[:11]
    if moe:
        rw_ref, x1_ref, h2_ref, route_ref, cnt_ref, carry = refs[11:]
    elif final:
        wg_ref, wu_ref, wd_ref, gfin_ref, o_ref = refs[11:]
    else:
        wg_ref, wu_ref, wd_ref, o_ref = refs[11:]
    lru = (hf_ref[...] + hb_ref[...]) * gy_ref[...].astype(F32)
    mix = jnp.concatenate([
        _rms(lru, gl_ref[...], LRU_WIDTH), _rms(dil_ref[...].astype(F32), gd_ref[...], DIL_WIDTH),
        _rms(mo_ref[...].astype(F32), gm_ref[...], MLA_WIDTH)], axis=-1).astype(BF16)
    x1 = x_ref[...] + _dot(mix, wout_ref[...])
    h2 = _rms(x1, gffn_ref[...], D_MODEL)
    if not moe:
        h = h2.astype(BF16)
        g = _dot(h, wg_ref[...])
        u = _dot(h, wu_ref[...])
        out = x1 + _dot((g * jax.nn.sigmoid(g) * u).astype(BF16), wd_ref[...])
        o_ref[...] = _rms(out, gfin_ref[...], D_MODEL) if final else out
        return
    x1_ref[...] = x1
    h2_ref[...] = h2
    if moe:
        first = (pl.program_id(0) == 0) & (pl.program_id(1) == 0)

        @pl.when(first)
        def _():
            carry[...] = jnp.zeros_like(carry)

        h_hi = h2.astype(BF16)
        h_lo = (h2 - h_hi.astype(F32)).astype(BF16)
        logits = _dot(h_hi, rw_ref[0]) + (_dot(h_hi, rw_ref[1]) + _dot(h_lo, rw_ref[0]))
        tm = logits.shape[0]
        lane = lax.broadcasted_iota(jnp.int32, logits.shape, 1)
        neg = -jnp.inf
        lg = jnp.where(lane < N_EXPERTS, logits, neg)
        m1 = jnp.max(lg, axis=-1, keepdims=True)
        i1 = jnp.min(jnp.where(lg == m1, lane, LANES), axis=-1, keepdims=True)
        lg2 = jnp.where(lane == i1, neg, lg)
        m2 = jnp.max(lg2, axis=-1, keepdims=True)
        i2 = jnp.min(jnp.where(lg2 == m2, lane, LANES), axis=-1, keepdims=True)
        e2 = jnp.exp(m2 - m1)
        den = 1.0 + e2
        sel1, sel2 = lane == i1, lane == i2
        onehot = jnp.where(sel1 | sel2, 1.0, 0.0)
        tri = (lax.broadcasted_iota(jnp.int32, (tm, tm), 0) > lax.broadcasted_iota(jnp.int32, (tm, tm), 1))
        before = _dot(jnp.where(tri, 1.0, 0.0).astype(BF16), onehot.astype(BF16)) + carry[0:1, :]
        rank1 = jnp.sum(jnp.where(sel1, before, 0.0), axis=-1, keepdims=True)
        rank2 = jnp.sum(jnp.where(sel2, before, 0.0), axis=-1, keepdims=True)
        carry[...] = carry[...] + jnp.sum(onehot, axis=0, keepdims=True)
        cnt_ref[...] = carry[...]
        cols = (i1.astype(F32), i2.astype(F32), 1.0 / den, e2 / den, rank1, rank2)
        route = jnp.zeros(logits.shape, F32)
        for c, val in enumerate(cols):
            route = jnp.where(lane == c, val, route)
        route_ref[...] = route


def _mix_out(x2d, hf, hb, gy, dil, mo, lw, B, S, tm, moe, gfin=None):
    T = B * S
    ns = S // tm
    tok = lambda b, s: (b * ns + s, 0)
    const = lambda b, s: (0, 0)
    tokspec = lambda c: pl.BlockSpec((tm, c), tok)
    full = lambda a: pl.BlockSpec(a.shape, const)
    once = lambda a: pl.BlockSpec(a.shape, const, pipeline_mode=pl.Buffered(1))
    args = [x2d, hf, hb, gy, dil, mo, lw['g_l'], lw['g_d'], lw['g_m'], lw['w_out'], lw['g_ffn']]
    in_specs = [tokspec(D_MODEL), tokspec(LRU_WIDTH), tokspec(LRU_WIDTH), tokspec(LRU_WIDTH), tokspec(DIL_WIDTH),
                tokspec(MLA_WIDTH), full(lw['g_l']), full(lw['g_d']), full(lw['g_m']), once(lw['w_out']),
                full(lw['g_ffn'])]
    scratch = []
    if moe:
        args.append(lw['router_w'])
        in_specs.append(pl.BlockSpec(lw['router_w'].shape, lambda b, s: (0, 0, 0)))
        out_shape = [jax.ShapeDtypeStruct((T, D_MODEL), F32), jax.ShapeDtypeStruct((T, D_MODEL), F32),
                     jax.ShapeDtypeStruct((T, LANES), F32), jax.ShapeDtypeStruct((SUBLANES, LANES), F32)]
        out_specs = [tokspec(D_MODEL), tokspec(D_MODEL), tokspec(LANES), pl.BlockSpec((SUBLANES, LANES), const)]
        scratch = [pltpu.VMEM((SUBLANES, LANES), F32)]
    else:
        ffn_w = [lw['ffn_wg'], lw['ffn_wu'], lw['ffn_wd']]
        args += ffn_w + ([gfin] if gfin is not None else [])
        in_specs += [once(w) for w in ffn_w] + ([full(gfin)] if gfin is not None else [])
        out_shape = [jax.ShapeDtypeStruct((T, D_MODEL), F32)]
        out_specs = [tokspec(D_MODEL)]
    sem = ("arbitrary", "arbitrary") if moe else ("parallel", "parallel")
    res = pl.pallas_call(
        functools.partial(_mix_out_kernel, moe=moe, final=gfin is not None and not moe), grid=(B, ns),
        in_specs=in_specs, out_specs=tuple(out_specs), out_shape=tuple(out_shape), scratch_shapes=scratch,
        compiler_params=_cparams(sem), name="mix_out_moe" if moe else "mix_out_ffn",
    )(*args)
    return res if moe else res[0]


MOE_BLOCK_ROWS = 512
DISPATCH_TOKENS = 1024
COMBINE_TOKENS = 512


ZERO_BLOCKS = 2 * N_EXPERTS


def _dispatch_kernel(pos_ref, zinfo_ref, h_ref, xs_out, zbuf, sem, zsem, *, tmd):
    bm = MOE_BLOCK_ROWS

    @pl.when(pl.program_id(0) == 0)
    def _():
        zbuf[...] = jnp.zeros_like(zbuf)

        def zero_copy(i):
            start = pl.multiple_of(zinfo_ref[0, 0, i] * bm, bm)
            return pltpu.make_async_copy(zbuf, xs_out.at[pl.ds(start, bm), :], zsem)

        for i in range(ZERO_BLOCKS):
            @pl.when(zinfo_ref[0, 0, ZERO_BLOCKS + i] != 0)
            def _():
                zero_copy(i).start()
        for i in range(ZERO_BLOCKS):
            @pl.when(zinfo_ref[0, 0, ZERO_BLOCKS + i] != 0)
            def _():
                zero_copy(i).wait()

    def body(t, c):
        src = h_ref.at[pl.ds(t, 1), :]
        for k in range(TOP_K):
            pltpu.make_async_copy(src, xs_out.at[pl.ds(pos_ref[0, 0, TOP_K * t + k], 1), :], sem).start()
        return c

    lax.fori_loop(0, tmd, body, 0, unroll=4)
    for _ in range(TOP_K):
        pltpu.make_async_copy(h_ref, xs_out.at[pl.ds(0, tmd), :], sem).wait()


def _dispatch(h2, pos, zinfo, n_rows):
    T = h2.shape[0]
    tmd = DISPATCH_TOKENS
    pos3 = pos.reshape(T // tmd, 1, TOP_K * tmd)
    return pl.pallas_call(
        functools.partial(_dispatch_kernel, tmd=tmd), grid=(T // tmd,),
        in_specs=[pl.BlockSpec((1, 1, TOP_K * tmd), lambda i: (i, 0, 0), memory_space=pltpu.SMEM),
                  pl.BlockSpec((1, 1, 2 * ZERO_BLOCKS), lambda i: (0, 0, 0), memory_space=pltpu.SMEM),
                  pl.BlockSpec((tmd, D_MODEL), lambda i: (i, 0))],
        out_specs=pl.BlockSpec(memory_space=pl.ANY),
        out_shape=jax.ShapeDtypeStruct((n_rows, D_MODEL), F32),
        scratch_shapes=[pltpu.VMEM((MOE_BLOCK_ROWS, D_MODEL), F32), pltpu.SemaphoreType.DMA(()),
                        pltpu.SemaphoreType.DMA(())],
        compiler_params=_cparams(("arbitrary",)), name="moe_dispatch",
    )(pos3, zinfo.reshape(1, 1, 2 * ZERO_BLOCKS), h2)


def _gmm_kernel(be_ref, used_ref, xs_ref, wg_ref, wu_ref, wd_ref, y_ref, xb):
    del be_ref
    f = pl.program_id(1)
    used = used_ref[pl.program_id(0)] != 0

    @pl.when(used)
    def _():
        @pl.when(f == 0)
        def _():
            xb[...] = xs_ref[...].astype(BF16)

        h = xb[...]
        g = _dot(h, wg_ref[0])
        u = _dot(h, wu_ref[0])
        part = _dot((g * jax.nn.sigmoid(g) * u).astype(BF16), wd_ref[0])

        @pl.when(f == 0)
        def _():
            y_ref[...] = part

        @pl.when(f > 0)
        def _():
            y_ref[...] += part

    @pl.when(jnp.logical_not(used) & (f == 0))
    def _():
        y_ref[...] = jnp.zeros_like(y_ref)


def _gmm(xs, block_expert, block_used, wg, wu, wd, tf):
    P = xs.shape[0]
    bm = MOE_BLOCK_ROWS
    F = wg.shape[2]
    grid_spec = pltpu.PrefetchScalarGridSpec(
        num_scalar_prefetch=2, grid=(P // bm, F // tf),
        in_specs=[pl.BlockSpec((bm, D_MODEL), lambda j, f, be, bu: (j, 0)),
                  pl.BlockSpec((1, D_MODEL, tf), lambda j, f, be, bu: (be[j], 0, f)),
                  pl.BlockSpec((1, D_MODEL, tf), lambda j, f, be, bu: (be[j], 0, f)),
                  pl.BlockSpec((1, tf, D_MODEL), lambda j, f, be, bu: (be[j], f, 0))],
        out_specs=pl.BlockSpec((bm, D_MODEL), lambda j, f, be, bu: (j, 0)),
        scratch_shapes=[pltpu.VMEM((bm, D_MODEL), BF16)])
    return pl.pallas_call(
        _gmm_kernel, grid_spec=grid_spec, out_shape=jax.ShapeDtypeStruct((P, D_MODEL), F32),
        compiler_params=_cparams(("parallel", "arbitrary")), name="moe_gmm",
    )(block_expert, block_used, xs, wg, wu, wd)


def _combine_kernel(*refs, tmc, final):
    if final:
        pos_ref, route_ref, x1_ref, y_hbm, gfin_ref, o_ref, ybuf, sem = refs
    else:
        pos_ref, route_ref, x1_ref, y_hbm, o_ref, ybuf, sem = refs

    def body(t, c):
        for k in range(TOP_K):
            pltpu.make_async_copy(y_hbm.at[pl.ds(pos_ref[0, 0, TOP_K * t + k], 1), :],
                                  ybuf.at[k, pl.ds(t, 1), :], sem).start()
        return c

    lax.fori_loop(0, tmc, body, 0, unroll=4)
    for k in range(TOP_K):
        pltpu.make_async_copy(y_hbm.at[pl.ds(0, tmc), :], ybuf.at[k], sem).wait()
    route = route_ref[...]
    out = x1_ref[...] + route[:, 2:3] * ybuf[0] + route[:, 3:4] * ybuf[1]
    if final:
        out = _rms(out, gfin_ref[...], D_MODEL)
    o_ref[...] = out


def _combine(pos, route, x1, y, gfin):
    T = x1.shape[0]
    tmc = COMBINE_TOKENS
    final = gfin is not None
    pos3 = pos.reshape(T // tmc, 1, TOP_K * tmc)
    args = [pos3, route, x1, y] + ([gfin] if final else [])
    in_specs = [pl.BlockSpec((1, 1, TOP_K * tmc), lambda i: (i, 0, 0), memory_space=pltpu.SMEM),
                pl.BlockSpec((tmc, LANES), lambda i: (i, 0)), pl.BlockSpec((tmc, D_MODEL), lambda i: (i, 0)),
                pl.BlockSpec(memory_space=pl.ANY)]
    if final:
        in_specs.append(pl.BlockSpec(gfin.shape, lambda i: (0, 0)))
    return pl.pallas_call(
        functools.partial(_combine_kernel, tmc=tmc, final=final), grid=(T // tmc,), in_specs=in_specs,
        out_specs=pl.BlockSpec((tmc, D_MODEL), lambda i: (i, 0)),
        out_shape=jax.ShapeDtypeStruct((T, D_MODEL), F32),
        scratch_shapes=[pltpu.VMEM((TOP_K, tmc, D_MODEL), F32), pltpu.SemaphoreType.DMA(())],
        compiler_params=_cparams(("arbitrary",)), name="moe_combine",
    )(*args)


def _moe(h2, x1, route, counts, wg, wu, wd, gfin, tf):
    T = h2.shape[0]
    bm = MOE_BLOCK_ROWS
    n_blocks = TOP_K * T // bm + N_EXPERTS
    cnt = counts[0, :N_EXPERTS].astype(jnp.int32)
    padded = (cnt + bm - 1) // bm * bm
    ends = jnp.cumsum(padded)
    starts = ends - padded
    experts = route[:, 0:TOP_K].astype(jnp.int32)
    pos = starts[experts] + route[:, 4:4 + TOP_K].astype(jnp.int32)
    block_start = jnp.arange(n_blocks, dtype=jnp.int32) * bm
    block_expert = jnp.minimum(jnp.sum((block_start[:, None] >= ends[None, :]).astype(jnp.int32), axis=1),
                               N_EXPERTS - 1)
    tail = n_blocks - N_EXPERTS + jnp.arange(N_EXPERTS, dtype=jnp.int32)
    zinfo = jnp.concatenate([ends // bm - 1, tail, (padded > 0).astype(jnp.int32),
                             (tail * bm >= ends[-1]).astype(jnp.int32)]).astype(jnp.int32)
    xs = _dispatch(h2, pos, zinfo, n_blocks * bm)
    y = _gmm(xs, block_expert, (block_start < ends[-1]).astype(jnp.int32), wg, wu, wd, tf)
    return _combine(pos, route, x1, y, gfin)


def _pad_heads(w, heads, dim, at=0):
    k = w.shape[0]
    w = w.reshape(k, heads, dim)
    w = jnp.pad(w, ((0, 0), (0, 0), (at, MLA_HEAD_PAD - at - dim)))
    return w.reshape(k, heads * MLA_HEAD_PAD)


def _block_diag(w):
    nb, bs, _ = w.shape
    eye = jnp.eye(nb, dtype=w.dtype)
    return (eye[:, None, :, None] * w[:, :, None, :]).reshape(nb * bs, nb * bs)


def _rope_tables(S):
    def cs(dim):
        half = dim // 2
        freqs = jnp.power(jnp.float32(ROPE_THETA), -jnp.arange(half, dtype=F32) * 2.0 / dim)
        ang = jnp.arange(S, dtype=F32)[:, None] * freqs[None, :]
        c, s = jnp.cos(ang), jnp.sin(ang)
        return jnp.concatenate([c, c], -1), jnp.concatenate([-s, s], -1)

    cd, sd = cs(DIL_HEAD_DIM)
    cm, sm = cs(MLA_ROPE)
    ones = jnp.ones((S, MLA_NOPE), F32)
    pad = MLA_HEAD_PAD - MLA_NOPE - MLA_ROPE
    return dict(
        cosd=jnp.tile(cd, (1, DIL_HEADS)), sind=jnp.tile(sd, (1, DIL_HEADS)),
        cosm=jnp.concatenate([ones, cm, jnp.ones((S, pad), F32)], -1),
        sinm=jnp.concatenate([0.0 * ones, sm, jnp.zeros((S, pad), F32)], -1))


def _prep_layer(l, p):
    w_in = p['w_in'][l]
    offs = [0, 256, 512, 768, 1024, 1280, 1536, 1664, 1696]
    xr, yr, qd, kd, vd, cq, ckv, kr = [w_in[:, a:b] for a, b in zip(offs[:-1], offs[1:])]
    kr_pad = _pad_heads(kr, 1, MLA_ROPE, at=MLA_NOPE)
    w1 = jnp.concatenate([xr, yr, qd, kd, vd, cq, ckv, kr_pad], axis=1)
    w_uq = p['mla_w_uq'][l].reshape(MLA_Q_RANK, MLA_HEADS, MLA_NOPE + MLA_ROPE)
    qn, qr = w_uq[:, :, :MLA_NOPE], w_uq[:, :, MLA_NOPE:]
    qr_sw = qr.reshape(MLA_Q_RANK, MLA_HEADS, 2, MLA_ROPE // 2)[:, :, ::-1, :].reshape(qr.shape)
    flat = lambda a, b: jnp.concatenate([a, b], -1).reshape(MLA_Q_RANK, MLA_HEADS * (MLA_NOPE + MLA_ROPE))
    w_uq2 = jnp.concatenate([_pad_heads(flat(qn, qr), MLA_HEADS, MLA_NOPE + MLA_ROPE),
                             _pad_heads(flat(jnp.zeros_like(qn), qr_sw), MLA_HEADS, MLA_NOPE + MLA_ROPE)], axis=1)

    w_ukv = p['mla_w_ukv'][l].reshape(MLA_KV_RANK, MLA_HEADS, MLA_NOPE + MLA_V)
    kn = _pad_heads(w_ukv[:, :, :MLA_NOPE].reshape(MLA_KV_RANK, -1), MLA_HEADS, MLA_NOPE)
    vm = w_ukv[:, :, MLA_NOPE:].reshape(MLA_KV_RANK, MLA_WIDTH)
    w_ukv2 = jnp.concatenate([kn, vm], axis=1)

    rg_w = jnp.stack([jnp.concatenate([_block_diag(p['rg_w_a'][l, d]), _block_diag(p['rg_w_i'][l, d])], axis=1)
                      for d in range(2)])
    rg_b = jnp.stack([jnp.concatenate([p['rg_b_a'][l, d], p['rg_b_i'][l, d]])[None, :] for d in range(2)])
    g_out = p['mix_out_norm_g'][l]
    row = lambda v: v.reshape(1, -1).astype(F32)
    return dict(
        g_mix=row(p['norm_mix_g'][l]), w1=w1.astype(BF16), g_q=row(p['mla_q_norm_g'][l]), w_uq=w_uq2.astype(BF16),
        g_kv=row(p['mla_kv_norm_g'][l]), w_ukv=w_ukv2.astype(BF16),
        conv_w=p['conv_w'][l].astype(F32), conv_b=row(p['conv_b'][l]), rg_w=rg_w.astype(BF16), rg_b=rg_b.astype(F32),
        rg_lam=p['rg_lambda'][l].reshape(2, 1, LRU_WIDTH).astype(F32),
        g_l=row(g_out[:LRU_WIDTH]), g_d=row(g_out[LRU_WIDTH:LRU_WIDTH + DIL_WIDTH]),
        g_m=row(g_out[LRU_WIDTH + DIL_WIDTH:]), w_out=p['w_out'][l].astype(BF16), g_ffn=row(p['norm_ffn_g'][l]))


def _mixers(x2d, lw, tabs, B, S, tm, moe, gfin=None):
    xr, gy, q, k, v, qm, km, vm = _in_proj(x2d, lw, tabs, B, S, tm)
    hf, hb = _lru(xr, lw, B, S)
    dil = _dilated(q, k, v, B, S)
    mo = _mla(qm, km, vm, B, S, 256)
    return _mix_out(x2d, hf, hb, gy, dil, mo, lw, B, S, 2 * tm if moe else tm, moe, gfin)


def _trunk(x, layers, p, depth):
    B, S, _ = x.shape
    tabs = _rope_tables(S)
    x2d = x.reshape(B * S, D_MODEL)
    tm = 512
    for l in range(depth):
        lw = layers[l]
        moe = l % 2 == 1
        last = l == depth - 1
        gfin = p['final_norm_g'].reshape(1, D_MODEL) if last else None
        j = l // 2
        if moe:
            rw = jnp.pad(p['router_w'][j].astype(F32), ((0, 0), (0, LANES - N_EXPERTS)))
            rw_hi = rw.astype(BF16)
            lw = dict(lw, router_w=jnp.stack([rw_hi, (rw - rw_hi.astype(F32)).astype(BF16)]))
            x1, h2, route, counts = _mixers(x2d, lw, tabs, B, S, tm, True)
            x2d = _moe(h2, x1, route, counts, lw['moe_wg'], lw['moe_wu'], lw['moe_wd'], gfin, 1792)
        else:
            x2d = _mixers(x2d, lw, tabs, B, S, tm, False, gfin)
    return x2d.reshape(B, S, D_MODEL)


def kernel(x_prompt, x_sample, norm_mix_g, w_in, conv_w, conv_b, rg_w_a, rg_b_a, rg_w_i, rg_b_i, rg_lambda, mla_q_norm_g, mla_w_uq, mla_kv_norm_g, mla_w_ukv, mix_out_norm_g, w_out, norm_ffn_g, ffn_w_gate, ffn_w_up, ffn_w_down, router_w, moe_w_gate, moe_w_up, moe_w_down, final_norm_g):
    p = dict(norm_mix_g=norm_mix_g, w_in=w_in, conv_w=conv_w, conv_b=conv_b, rg_w_a=rg_w_a, rg_b_a=rg_b_a,
             rg_w_i=rg_w_i, rg_b_i=rg_b_i, rg_lambda=rg_lambda, mla_q_norm_g=mla_q_norm_g, mla_w_uq=mla_w_uq,
             mla_kv_norm_g=mla_kv_norm_g, mla_w_ukv=mla_w_ukv, mix_out_norm_g=mix_out_norm_g, w_out=w_out,
             norm_ffn_g=norm_ffn_g, router_w=router_w, final_norm_g=final_norm_g)
    depth = w_in.shape[0]
    layers = []
    for l in range(depth):
        lw = _prep_layer(l, p)
        j = l // 2
        if l % 2 == 1:
            lw.update(moe_wg=moe_w_gate[j].astype(BF16), moe_wu=moe_w_up[j].astype(BF16),
                      moe_wd=moe_w_down[j].astype(BF16))
        else:
            lw.update(ffn_wg=ffn_w_gate[j].astype(BF16), ffn_wu=ffn_w_up[j].astype(BF16),
                      ffn_wd=ffn_w_down[j].astype(BF16))
        layers.append(lw)
    return (_trunk(x_prompt, layers, p, depth), _trunk(x_sample, layers, p, depth))
```

```python
import functools
import math

import jax
import jax.numpy as jnp
from jax import lax
from jax.experimental import pallas as pl
from jax.experimental.pallas import tpu as pltpu

F32 = jnp.float32
BF16 = jnp.bfloat16

D_MODEL = 1024
NORM_EPS = 1e-6
ROPE_THETA = 10000.0
LRU_WIDTH = 256
LRU_BLOCKS = 4
CONV_WIDTH = 4
LRU_C = 8.0
DIL_HEADS = 4
DIL_HEAD_DIM = 64
DIL_WIDTH = DIL_HEADS * DIL_HEAD_DIM
DIL_PATTERNS = ((128, 1), (512, 4), (2048, 16))
MLA_HEADS = 8
MLA_NOPE = 64
MLA_ROPE = 32
MLA_V = 64
MLA_Q_RANK = 256
MLA_KV_RANK = 128
MLA_WIDTH = MLA_HEADS * MLA_V
N_EXPERTS = 8
TOP_K = 2

LANES = 128
SUBLANES = 8
MLA_HEAD_PAD = LANES
VMEM_LIMIT = 56 * 1024 * 1024
LOG2E = math.log2(math.e)

_C_XR, _C_YR, _C_Q, _C_K, _C_V, _C_CQ, _C_CKV, _C_KR, _C_END = (0, 256, 512, 768, 1024, 1280, 1536, 1664, 1792)


def _cparams(sem):
    return pltpu.CompilerParams(dimension_semantics=sem, vmem_limit_bytes=VMEM_LIMIT)


def _rms(x, g, n):
    ms = jnp.sum(x * x, axis=-1, keepdims=True) * (1.0 / n)
    return x * lax.rsqrt(ms + NORM_EPS) * g


def _gelu_tanh(x):
    return 0.5 * x * (1.0 + jnp.tanh(0.7978845608028654 * (x + 0.044715 * (x * x * x))))


def _dot(a, b):
    return jnp.dot(a, b, preferred_element_type=F32)


def _dot_nt(a, b):
    return lax.dot_general(a, b, (((1,), (1,)), ((), ())), preferred_element_type=F32)


def _lane_groups(width):
    return [slice(c * LANES, (c + 1) * LANES) for c in range(width // LANES)]


def _swap_rotary_halves(x, half, period, start):
    lane = lax.broadcasted_iota(jnp.int32, (1, LANES), 1) % period
    first = lane < start + half
    outs = []
    for cs in _lane_groups(x.shape[1]):
        xs = x[:, cs]
        outs.append(jnp.where(first, pltpu.roll(xs, LANES - half, 1), pltpu.roll(xs, half, 1)))
    return outs[0] if len(outs) == 1 else jnp.concatenate(outs, axis=-1)


def _in_proj_kernel(x_ref, g_ref, w1_ref, gq_ref, wuq_ref, gkv_ref, wukv_ref,
                    cosd_ref, sind_ref, cosm_ref, sinm_ref,
                    xr_ref, gy_ref, q_ref, k_ref, v_ref, qm_ref, km_ref, vm_ref):
    h = _rms(x_ref[...], g_ref[...], D_MODEL).astype(BF16)
    p = _dot(h, w1_ref[...])
    xr_ref[...] = p[:, _C_XR:_C_YR]
    gy_ref[...] = _gelu_tanh(p[:, _C_YR:_C_Q]).astype(BF16)
    cosd, sind = cosd_ref[...], sind_ref[...]
    half_d = DIL_HEAD_DIM // 2
    q, k = p[:, _C_Q:_C_K], p[:, _C_K:_C_V]
    q = q * cosd + _swap_rotary_halves(q, half_d, DIL_HEAD_DIM, 0) * sind
    q_ref[...] = (q * (DIL_HEAD_DIM ** -0.5 * LOG2E)).astype(BF16)
    k_ref[...] = (k * cosd + _swap_rotary_halves(k, half_d, DIL_HEAD_DIM, 0) * sind).astype(BF16)
    v_ref[...] = p[:, _C_V:_C_CQ].astype(BF16)

    cosm, sinm = cosm_ref[...], sinm_ref[...]
    half_m = MLA_ROPE // 2
    cqn = _rms(p[:, _C_CQ:_C_CKV], gq_ref[...], MLA_Q_RANK).astype(BF16)
    nq = MLA_HEADS * MLA_HEAD_PAD
    qq2 = _dot(cqn, wuq_ref[...])
    qq, qq_sw = qq2[:, :nq], qq2[:, nq:]
    ckvn = _rms(p[:, _C_CKV:_C_KR], gkv_ref[...], MLA_KV_RANK).astype(BF16)
    kv = _dot(ckvn, wukv_ref[...])
    kr = p[:, _C_KR:_C_END]
    kr = kr * cosm + _swap_rotary_halves(kr, half_m, MLA_HEAD_PAD, MLA_NOPE) * sinm
    scale = (MLA_NOPE + MLA_ROPE) ** -0.5 * LOG2E
    for hh in range(MLA_HEADS):
        lo, hi = hh * MLA_HEAD_PAD, (hh + 1) * MLA_HEAD_PAD
        qh = qq[:, lo:hi] * cosm + qq_sw[:, lo:hi] * sinm
        qm_ref[:, lo:hi] = (qh * scale).astype(BF16)
        km_ref[:, lo:hi] = (kv[:, lo:hi] + kr).astype(BF16)
    vm_ref[...] = kv[:, nq:].astype(BF16)


def _in_proj(x2d, lw, tabs, B, S, tm):
    T = B * S
    ns = S // tm
    tok = lambda b, s: (b * ns + s, 0)
    pos = lambda b, s: (s, 0)
    const = lambda b, s: (0, 0)

    def full(a):
        return pl.BlockSpec(a.shape, const)

    def tokspec(c):
        return pl.BlockSpec((tm, c), tok)

    out_shape = (
        jax.ShapeDtypeStruct((T, LRU_WIDTH), F32),
        jax.ShapeDtypeStruct((T, LRU_WIDTH), BF16),
        jax.ShapeDtypeStruct((T, DIL_WIDTH), BF16),
        jax.ShapeDtypeStruct((T, DIL_WIDTH), BF16),
        jax.ShapeDtypeStruct((T, DIL_WIDTH), BF16),
        jax.ShapeDtypeStruct((T, MLA_HEADS * MLA_HEAD_PAD), BF16),
        jax.ShapeDtypeStruct((T, MLA_HEADS * MLA_HEAD_PAD), BF16),
        jax.ShapeDtypeStruct((T, MLA_WIDTH), BF16),
    )
    out_specs = (
        tokspec(LRU_WIDTH), tokspec(LRU_WIDTH), tokspec(DIL_WIDTH), tokspec(DIL_WIDTH), tokspec(DIL_WIDTH),
        tokspec(MLA_HEADS * MLA_HEAD_PAD), tokspec(MLA_HEADS * MLA_HEAD_PAD), tokspec(MLA_WIDTH),
    )
    in_specs = [
        tokspec(D_MODEL), full(lw['g_mix']), full(lw['w1']), full(lw['g_q']), full(lw['w_uq']),
        full(lw['g_kv']), full(lw['w_ukv']),
        pl.BlockSpec((tm, DIL_WIDTH), pos), pl.BlockSpec((tm, DIL_WIDTH), pos),
        pl.BlockSpec((tm, LANES), pos), pl.BlockSpec((tm, LANES), pos),
    ]
    return pl.pallas_call(
        _in_proj_kernel, grid=(B, ns), in_specs=in_specs, out_specs=out_specs, out_shape=out_shape,
        compiler_params=_cparams(("parallel", "parallel")), name="in_proj",
    )(x2d, lw['g_mix'], lw['w1'], lw['g_q'], lw['w_uq'], lw['g_kv'], lw['w_ukv'],
      tabs['cosd'], tabs['sind'], tabs['cosm'], tabs['sinm'])


LRU_CHUNK_ROWS = 1024
LRU_HALO = SUBLANES


def _lru_kernel(xf_ref, xfp_ref, xfn_ref, xb_ref, xbp_ref, xbn_ref, cw_ref, cb_ref, wg_ref, bg_ref,
                lam_ref, hf_ref, hb_ref, xpad, a_f, b_f, a_b, b_b, hs_f, hs_b, hcar, *, B, tc):
    i = pl.program_id(0)
    n = pl.num_programs(0)
    R = tc * B
    left = CONV_WIDTH // 2
    groups = _lane_groups(LRU_WIDTH)

    @pl.when(i == 0)
    def _():
        hcar[...] = jnp.zeros_like(hcar)

    def prep(x_ref, xp_ref, xn_ref, ci, d, a_s, b_s):
        for c, cs in enumerate(groups):
            for b in range(B):
                xpad[c, pl.ds(b, left, stride=B), :] = jnp.where(ci > 0, xp_ref[b, LRU_HALO - left:LRU_HALO, cs], 0.0)
                xpad[c, pl.ds(left * B + b, tc, stride=B), :] = x_ref[b, :, cs]
                xpad[c, pl.ds((left + tc) * B + b, 1), :] = jnp.where(ci < n - 1, xn_ref[b, 0:1, cs], 0.0)
        halves = []
        for c, cs in enumerate(groups):
            xc = cb_ref[:, cs] + xpad[c, 0:R, :] * cw_ref[0:1, cs]
            for kk in range(1, CONV_WIDTH):
                xc = xc + xpad[c, kk * B:kk * B + R, :] * cw_ref[kk:kk + 1, cs]
            halves.append(xc)
        xc = jnp.concatenate(halves, axis=-1)
        g = _dot(xc.astype(BF16), wg_ref[d]) + bg_ref[d]
        r = jax.nn.sigmoid(g[:, :LRU_WIDTH])
        ig = jax.nn.sigmoid(g[:, LRU_WIDTH:])
        lam = lam_ref[d]
        softplus = jnp.maximum(-lam, 0.0) + jnp.log(1.0 + jnp.exp(-jnp.abs(lam)))
        log_a = (-LRU_C) * r * softplus
        a = jnp.exp(log_a)
        a_s[...] = a
        b_s[...] = jnp.sqrt(1.0 - a * a) * (ig * xc)

    prep(xf_ref, xfp_ref, xfn_ref, i, 0, a_f, b_f)
    prep(xb_ref, xbp_ref, xbn_ref, n - 1 - i, 1, a_b, b_b)

    def body(s, carry):
        hf, hb = carry
        rf = pl.multiple_of(s * B, B)
        hf = a_f[pl.ds(rf, B), :] * hf + b_f[pl.ds(rf, B), :]
        rb = pl.multiple_of((tc - 1 - s) * B, B)
        hb = a_b[pl.ds(rb, B), :] * hb + b_b[pl.ds(rb, B), :]
        for c, cs in enumerate(groups):
            hs_f[c, pl.ds(rf, B), :] = hf[:, cs]
            hs_b[c, pl.ds(rb, B), :] = hb[:, cs]
        return hf, hb

    hf, hb = lax.fori_loop(0, tc, body, (hcar[0], hcar[1]), unroll=8)
    hcar[0] = hf
    hcar[1] = hb
    for c, cs in enumerate(groups):
        for b in range(B):
            hf_ref[b, :, cs] = hs_f[c, pl.ds(b, tc, stride=B), :]
            hb_ref[b, :, cs] = hs_b[c, pl.ds(b, tc, stride=B), :]


def _lru(xr, lw, B, S):
    tc = LRU_CHUNK_ROWS // B
    n = S // tc
    nh = S // LRU_HALO
    hb_per = tc // LRU_HALO
    x3 = xr.reshape(B, S, LRU_WIDTH)
    W = LRU_WIDTH

    main_f = pl.BlockSpec((B, tc, W), lambda i: (0, i, 0))
    prev_f = pl.BlockSpec((B, LRU_HALO, W), lambda i: (0, jnp.maximum(i * hb_per - 1, 0), 0))
    next_f = pl.BlockSpec((B, LRU_HALO, W), lambda i: (0, jnp.minimum((i + 1) * hb_per, nh - 1), 0))
    main_b = pl.BlockSpec((B, tc, W), lambda i: (0, n - 1 - i, 0))
    prev_b = pl.BlockSpec((B, LRU_HALO, W), lambda i: (0, jnp.maximum((n - 1 - i) * hb_per - 1, 0), 0))
    next_b = pl.BlockSpec((B, LRU_HALO, W), lambda i: (0, jnp.minimum((n - i) * hb_per, nh - 1), 0))

    def full(a):
        nd = a.ndim
        return pl.BlockSpec(a.shape, lambda i: (0,) * nd)

    R = tc * B
    ng = W // LANES
    hf, hb = pl.pallas_call(
        functools.partial(_lru_kernel, B=B, tc=tc),
        grid=(n,),
        in_specs=[main_f, prev_f, next_f, main_b, prev_b, next_b,
                  full(lw['conv_w']), full(lw['conv_b']), full(lw['rg_w']), full(lw['rg_b']), full(lw['rg_lam'])],
        out_specs=(main_f, main_b),
        out_shape=(jax.ShapeDtypeStruct((B, S, W), F32),) * 2,
        scratch_shapes=[pltpu.VMEM((ng, R + (CONV_WIDTH - 1) * B, LANES), F32)] + [pltpu.VMEM((R, W), F32)] * 4
        + [pltpu.VMEM((ng, R, LANES), F32)] * 2 + [pltpu.VMEM((2, B, W), F32)],
        compiler_params=_cparams(("arbitrary",)), name="rglru",
    )(x3, x3, x3, x3, x3, x3, lw['conv_w'], lw['conv_b'], lw['rg_w'], lw['rg_b'], lw['rg_lam'])
    return hf.reshape(B * S, W), hb.reshape(B * S, W)


BAND_HALF = 64
DIL_TILE = 2048
DIL_QBLOCK = 128
DIL_STACK_ROWS = 256


def _dil_kernel(q_ref, kp_ref, k_ref, kn_ref, vp_ref, v_ref, vn_ref, o_ref,
                qs, kcat, vcat, acc_m, acc_l, acc_o, *, S):
    TT = DIL_TILE
    tile_start = pl.program_id(1) * TT
    groups = _lane_groups(DIL_WIDTH)
    for c, cs in enumerate(groups):
        qs[c] = q_ref[0, :, cs].astype(F32)
        for j, (kr, vr) in enumerate(((kp_ref, vp_ref), (k_ref, v_ref), (kn_ref, vn_ref))):
            kcat[c, j * TT:(j + 1) * TT, :] = kr[0, :, cs].astype(F32)
            vcat[c, j * TT:(j + 1) * TT, :] = vr[0, :, cs].astype(F32)
    head = lax.broadcasted_iota(jnp.int32, (1, DIL_WIDTH), 1) // DIL_HEAD_DIM

    order = sorted((dil for _, dil in DIL_PATTERNS), reverse=True)
    for dil in order:
        first = dil == order[0]
        nq = min(DIL_QBLOCK, TT // dil)
        nsub = TT // dil // nq
        nk = nq + 2 * BAND_HALF
        row = lax.broadcasted_iota(jnp.int32, (nq, nk), 0)
        col = lax.broadcasted_iota(jnp.int32, (nq, nk), 1)
        band = jnp.abs(col - BAND_HALF - row) <= BAND_HALF

        def rows(start, n, dil=dil):
            return pl.ds(pl.multiple_of(start, SUBLANES), n) if dil == 1 else pl.ds(start, n, stride=dil)

        def ld(ref, idx):
            return jnp.concatenate([ref[c, idx, :] for c in range(len(groups))], axis=-1)

        def st(ref, idx, val):
            for c, cs in enumerate(groups):
                ref[c, idx, :] = val[:, cs]

        def block(it, carry, dil=dil, nq=nq, nsub=nsub, nk=nk, band=band, col=col, rows=rows, first=first):
            r, j = (0, it) if dil == 1 else (it % dil, it // dil)
            off = j * nq * dil + r
            qi = rows(off, nq)
            ki = rows(off + TT - BAND_HALF * dil, nk)
            kpos = tile_start + off - BAND_HALF * dil + col * dil
            valid = band & (kpos >= 0) & (kpos < S)
            q = ld(qs, qi).astype(BF16)
            k = ld(kcat, ki).astype(BF16)
            v = ld(vcat, ki).astype(BF16)
            m_b = jnp.zeros((nq, DIL_WIDTH), F32)
            l_b = jnp.zeros((nq, DIL_WIDTH), F32)
            o_b = jnp.zeros((nq, DIL_WIDTH), F32)
            per = min(DIL_HEADS, max(1, DIL_STACK_ROWS // nq))
            groups_h = [range(h0, h0 + per) for h0 in range(0, DIL_HEADS, per)]
            valid_g = jnp.concatenate([valid] * per, axis=0)
            scores = [_dot_nt(jnp.concatenate([jnp.where(head == h, q, jnp.zeros_like(q)) for h in hs], axis=0), k)
                      for hs in groups_h]
            stats = []
            for s in scores:
                s = jnp.where(valid_g, s, -1e30)
                m = jnp.max(s, axis=-1, keepdims=True)
                e = jnp.exp2(s - m)
                stats.append((m, jnp.sum(e, axis=-1, keepdims=True), e.astype(BF16)))
            for hs, (m, l, e) in zip(groups_h, stats):
                o = _dot(e, v)
                for n_, h in enumerate(hs):
                    hm = head == h
                    sl = slice(n_ * nq, (n_ + 1) * nq)
                    m_b = jnp.where(hm, m[sl], m_b)
                    l_b = jnp.where(hm, l[sl], l_b)
                    o_b = jnp.where(hm, o[sl], o_b)
            if first:
                st(acc_l, qi, l_b)
                st(acc_o, qi, o_b)
                st(acc_m, qi, m_b)
                return carry
            m_old = ld(acc_m, qi)
            m_new = jnp.maximum(m_old, m_b)
            w_old = jnp.exp2(m_old - m_new)
            w_b = jnp.exp2(m_b - m_new)
            st(acc_l, qi, w_old * ld(acc_l, qi) + w_b * l_b)
            st(acc_o, qi, w_old * ld(acc_o, qi) + w_b * o_b)
            st(acc_m, qi, m_new)
            return carry

        lax.fori_loop(0, dil * nsub, block, 0, unroll=4)

    for c, cs in enumerate(groups):
        o_ref[0, :, cs] = (acc_o[c] / acc_l[c]).astype(BF16)


def _dilated(q, k, v, B, S):
    TT = DIL_TILE
    nt = S // TT
    W = DIL_WIDTH
    ng = W // LANES
    q3, k3, v3 = (a.reshape(B, S, W) for a in (q, k, v))
    own = pl.BlockSpec((1, TT, W), lambda b, i: (b, i, 0))
    prev = pl.BlockSpec((1, TT, W), lambda b, i: (b, jnp.maximum(i - 1, 0), 0))
    nxt = pl.BlockSpec((1, TT, W), lambda b, i: (b, jnp.minimum(i + 1, nt - 1), 0))
    o = pl.pallas_call(
        functools.partial(_dil_kernel, S=S), grid=(B, nt),
        in_specs=[own, prev, own, nxt, prev, own, nxt], out_specs=own,
        out_shape=jax.ShapeDtypeStruct((B, S, W), BF16),
        scratch_shapes=[pltpu.VMEM((ng, TT, LANES), F32), pltpu.VMEM((ng, 3 * TT, LANES), F32),
                        pltpu.VMEM((ng, 3 * TT, LANES), F32)] + [pltpu.VMEM((ng, TT, LANES), F32)] * 3,
        compiler_params=_cparams(("parallel", "parallel")), name="dilated_attn",
    )(q3, k3, k3, k3, v3, v3, v3)
    return o.reshape(B * S, W)


MLA_KEY_CHUNK = 256


def _mla_kernel(q_ref, k_ref, v_ref, o_ref, vt):
    S = k_ref.shape[1]
    tq = q_ref.shape[1]
    ck = MLA_KEY_CHUNK
    chunks = [slice(c * ck, (c + 1) * ck) for c in range(S // ck)]

    @pl.when(pl.program_id(1) == 0)
    def _():
        vt[...] = v_ref[0].T

    def fold(x, op):
        return op(x.reshape(ck // SUBLANES, SUBLANES, tq), axis=0)

    def score_pass(h):
        lo, hi = h * MLA_HEAD_PAD, (h + 1) * MLA_HEAD_PAD
        q_h = q_ref[0, :, lo:hi]
        m8 = None
        parts = []
        for cs in chunks:
            s_c = _dot_nt(k_ref[0, cs, lo:hi], q_h)
            parts.append(s_c)
            m8 = fold(s_c, jnp.max) if m8 is None else jnp.maximum(m8, fold(s_c, jnp.max))
        return parts, jnp.max(m8, axis=0, keepdims=True)

    def value_pass(h, sm):
        parts, m = sm
        acc = jnp.zeros((MLA_V, tq), F32)
        l8 = jnp.zeros((SUBLANES, tq), F32)
        for cs, s_c in zip(chunks, parts):
            e = jnp.exp2(s_c - m)
            l8 = l8 + fold(e, jnp.sum)
            acc = acc + _dot(vt[h * MLA_V:(h + 1) * MLA_V, cs], e.astype(BF16))
        return acc * (1.0 / jnp.sum(l8, axis=0, keepdims=True))

    m_next = score_pass(0)
    outs = []
    for h in range(MLA_HEADS):
        m = m_next
        if h + 1 < MLA_HEADS:
            m_next = score_pass(h + 1)
        outs.append(value_pass(h, m))
        if h % 2 == 1:
            pair = h // 2
            o_ref[0, :, pair * LANES:(pair + 1) * LANES] = jnp.concatenate(outs[h - 1:h + 1], axis=0).T.astype(BF16)


def _mla(qm, km, vm, B, S, tq):
    QW = MLA_HEADS * MLA_HEAD_PAD
    q3, k3, v3 = qm.reshape(B, S, QW), km.reshape(B, S, QW), vm.reshape(B, S, MLA_WIDTH)
    o = pl.pallas_call(
        _mla_kernel, grid=(B, S // tq),
        in_specs=[pl.BlockSpec((1, tq, QW), lambda b, i: (b, i, 0)),
                  pl.BlockSpec((1, S, QW), lambda b, i: (b, 0, 0)),
                  pl.BlockSpec((1, S, MLA_WIDTH), lambda b, i: (b, 0, 0))],
        out_specs=pl.BlockSpec((1, tq, MLA_WIDTH), lambda b, i: (b, i, 0)),
        out_shape=jax.ShapeDtypeStruct((B, S, MLA_WIDTH), BF16),
        scratch_shapes=[pltpu.VMEM((MLA_WIDTH, S), BF16)],
        compiler_params=_cparams(("parallel", "arbitrary")), name="mla_attn",
    )(q3, k3, v3)
    return o.reshape(B * S, MLA_WIDTH)


def _mix_out_kernel(*refs, moe, final):
    (x_ref, hf_ref, hb_ref, gy_ref, dil_ref, mo_ref, gl_ref, gd_ref, gm_ref, wout_ref, gffn_ref) = refs[:11]
    if moe:
        rw_ref, x1_ref, h2_ref, route_ref, cnt_ref, carry = refs[11:]
    elif final:
        wg_ref, wu_ref, wd_ref, gfin_ref, o_ref = refs[11:]
    else:
        wg_ref, wu_ref, wd_ref, o_ref = refs[11:]
    lru = (hf_ref[...] + hb_ref[...]) * gy_ref[...].astype(F32)
    mix = jnp.concatenate([
        _rms(lru, gl_ref[...], LRU_WIDTH), _rms(dil_ref[...].astype(F32), gd_ref[...], DIL_WIDTH),
        _rms(mo_ref[...].astype(F32), gm_ref[...], MLA_WIDTH)], axis=-1).astype(BF16)
    x1 = x_ref[...] + _dot(mix, wout_ref[...])
    h2 = _rms(x1, gffn_ref[...], D_MODEL)
    if not moe:
        h = h2.astype(BF16)
        g = _dot(h, wg_ref[...])
        u = _dot(h, wu_ref[...])
        out = x1 + _dot((g * jax.nn.sigmoid(g) * u).astype(BF16), wd_ref[...])
        o_ref[...] = _rms(out, gfin_ref[...], D_MODEL) if final else out
        return
    x1_ref[...] = x1
    h2_ref[...] = h2
    if moe:
        first = (pl.program_id(0) == 0) & (pl.program_id(1) == 0)

        @pl.when(first)
        def _():
            carry[...] = jnp.zeros_like(carry)

        h_hi = h2.astype(BF16)
        h_lo = (h2 - h_hi.astype(F32)).astype(BF16)
        logits = _dot(h_hi, rw_ref[0]) + (_dot(h_hi, rw_ref[1]) + _dot(h_lo, rw_ref[0]))
        tm = logits.shape[0]
        lane = lax.broadcasted_iota(jnp.int32, logits.shape, 1)
        neg = -jnp.inf
        lg = jnp.where(lane < N_EXPERTS, logits, neg)
        m1 = jnp.max(lg, axis=-1, keepdims=True)
        i1 = jnp.min(jnp.where(lg == m1, lane, LANES), axis=-1, keepdims=True)
        lg2 = jnp.where(lane == i1, neg, lg)
        m2 = jnp.max(lg2, axis=-1, keepdims=True)
        i2 = jnp.min(jnp.where(lg2 == m2, lane, LANES), axis=-1, keepdims=True)
        e2 = jnp.exp(m2 - m1)
        den = 1.0 + e2
        sel1, sel2 = lane == i1, lane == i2
        onehot = jnp.where(sel1 | sel2, 1.0, 0.0)
        tri = (lax.broadcasted_iota(jnp.int32, (tm, tm), 0) > lax.broadcasted_iota(jnp.int32, (tm, tm), 1))
        before = _dot(jnp.where(tri, 1.0, 0.0).astype(BF16), onehot.astype(BF16)) + carry[0:1, :]
        rank1 = jnp.sum(jnp.where(sel1, before, 0.0), axis=-1, keepdims=True)
        rank2 = jnp.sum(jnp.where(sel2, before, 0.0), axis=-1, keepdims=True)
        carry[...] = carry[...] + jnp.sum(onehot, axis=0, keepdims=True)
        cnt_ref[...] = carry[...]
        cols = (i1.astype(F32), i2.astype(F32), 1.0 / den, e2 / den, rank1, rank2)
        route = jnp.zeros(logits.shape, F32)
        for c, val in enumerate(cols):
            route = jnp.where(lane == c, val, route)
        route_ref[...] = route


def _mix_out(x2d, hf, hb, gy, dil, mo, lw, B, S, tm, moe, gfin=None):
    T = B * S
    ns = S // tm
    tok = lambda b, s: (b * ns + s, 0)
    const = lambda b, s: (0, 0)
    tokspec = lambda c: pl.BlockSpec((tm, c), tok)
    full = lambda a: pl.BlockSpec(a.shape, const)
    once = lambda a: pl.BlockSpec(a.shape, const, pipeline_mode=pl.Buffered(1))
    args = [x2d, hf, hb, gy, dil, mo, lw['g_l'], lw['g_d'], lw['g_m'], lw['w_out'], lw['g_ffn']]
    in_specs = [tokspec(D_MODEL), tokspec(LRU_WIDTH), tokspec(LRU_WIDTH), tokspec(LRU_WIDTH), tokspec(DIL_WIDTH),
                tokspec(MLA_WIDTH), full(lw['g_l']), full(lw['g_d']), full(lw['g_m']), once(lw['w_out']),
                full(lw['g_ffn'])]
    scratch = []
    if moe:
        args.append(lw['router_w'])
        in_specs.append(pl.BlockSpec(lw['router_w'].shape, lambda b, s: (0, 0, 0)))
        out_shape = [jax.ShapeDtypeStruct((T, D_MODEL), F32), jax.ShapeDtypeStruct((T, D_MODEL), F32),
                     jax.ShapeDtypeStruct((T, LANES), F32), jax.ShapeDtypeStruct((SUBLANES, LANES), F32)]
        out_specs = [tokspec(D_MODEL), tokspec(D_MODEL), tokspec(LANES), pl.BlockSpec((SUBLANES, LANES), const)]
        scratch = [pltpu.VMEM((SUBLANES, LANES), F32)]
    else:
        ffn_w = [lw['ffn_wg'], lw['ffn_wu'], lw['ffn_wd']]
        args += ffn_w + ([gfin] if gfin is not None else [])
        in_specs += [once(w) for w in ffn_w] + ([full(gfin)] if gfin is not None else [])
        out_shape = [jax.ShapeDtypeStruct((T, D_MODEL), F32)]
        out_specs = [tokspec(D_MODEL)]
    sem = ("arbitrary", "arbitrary") if moe else ("parallel", "parallel")
    res = pl.pallas_call(
        functools.partial(_mix_out_kernel, moe=moe, final=gfin is not None and not moe), grid=(B, ns),
        in_specs=in_specs, out_specs=tuple(out_specs), out_shape=tuple(out_shape), scratch_shapes=scratch,
        compiler_params=_cparams(sem), name="mix_out_moe" if moe else "mix_out_ffn",
    )(*args)
    return res if moe else res[0]


MOE_BLOCK_ROWS = 512
DISPATCH_TOKENS = 1024
COMBINE_TOKENS = 512


ZERO_BLOCKS = 2 * N_EXPERTS


def _dispatch_kernel(pos_ref, zinfo_ref, h_ref, xs_out, zbuf, sem, zsem, *, tmd):
    bm = MOE_BLOCK_ROWS

    @pl.when(pl.program_id(0) == 0)
    def _():
        zbuf[...] = jnp.zeros_like(zbuf)

        def zero_copy(i):
            start = pl.multiple_of(zinfo_ref[0, 0, i] * bm, bm)
            return pltpu.make_async_copy(zbuf, xs_out.at[pl.ds(start, bm), :], zsem)

        for i in range(ZERO_BLOCKS):
            @pl.when(zinfo_ref[0, 0, ZERO_BLOCKS + i] != 0)
            def _():
                zero_copy(i).start()
        for i in range(ZERO_BLOCKS):
            @pl.when(zinfo_ref[0, 0, ZERO_BLOCKS + i] != 0)
            def _():
                zero_copy(i).wait()

    def body(t, c):
        src = h_ref.at[pl.ds(t, 1), :]
        for k in range(TOP_K):
            pltpu.make_async_copy(src, xs_out.at[pl.ds(pos_ref[0, 0, TOP_K * t + k], 1), :], sem).start()
        return c

    lax.fori_loop(0, tmd, body, 0, unroll=4)
    for _ in range(TOP_K):
        pltpu.make_async_copy(h_ref, xs_out.at[pl.ds(0, tmd), :], sem).wait()


def _dispatch(h2, pos, zinfo, n_rows):
    T = h2.shape[0]
    tmd = DISPATCH_TOKENS
    pos3 = pos.reshape(T // tmd, 1, TOP_K * tmd)
    return pl.pallas_call(
        functools.partial(_dispatch_kernel, tmd=tmd), grid=(T // tmd,),
        in_specs=[pl.BlockSpec((1, 1, TOP_K * tmd), lambda i: (i, 0, 0), memory_space=pltpu.SMEM),
                  pl.BlockSpec((1, 1, 2 * ZERO_BLOCKS), lambda i: (0, 0, 0), memory_space=pltpu.SMEM),
                  pl.BlockSpec((tmd, D_MODEL), lambda i: (i, 0))],
        out_specs=pl.BlockSpec(memory_space=pl.ANY),
        out_shape=jax.ShapeDtypeStruct((n_rows, D_MODEL), F32),
        scratch_shapes=[pltpu.VMEM((MOE_BLOCK_ROWS, D_MODEL), F32), pltpu.SemaphoreType.DMA(()),
                        pltpu.SemaphoreType.DMA(())],
        compiler_params=_cparams(("arbitrary",)), name="moe_dispatch",
    )(pos3, zinfo.reshape(1, 1, 2 * ZERO_BLOCKS), h2)


def _gmm_kernel(be_ref, used_ref, xs_ref, wg_ref, wu_ref, wd_ref, y_ref, xb):
    del be_ref
    f = pl.program_id(1)
    used = used_ref[pl.program_id(0)] != 0

    @pl.when(used)
    def _():
        @pl.when(f == 0)
        def _():
            xb[...] = xs_ref[...].astype(BF16)

        h = xb[...]
        g = _dot(h, wg_ref[0])
        u = _dot(h, wu_ref[0])
        part = _dot((g * jax.nn.sigmoid(g) * u).astype(BF16), wd_ref[0])

        @pl.when(f == 0)
        def _():
            y_ref[...] = part

        @pl.when(f > 0)
        def _():
            y_ref[...] += part

    @pl.when(jnp.logical_not(used) & (f == 0))
    def _():
        y_ref[...] = jnp.zeros_like(y_ref)


def _gmm(xs, block_expert, block_used, wg, wu, wd, tf):
    P = xs.shape[0]
    bm = MOE_BLOCK_ROWS
    F = wg.shape[2]
    grid_spec = pltpu.PrefetchScalarGridSpec(
        num_scalar_prefetch=2, grid=(P // bm, F // tf),
        in_specs=[pl.BlockSpec((bm, D_MODEL), lambda j, f, be, bu: (j, 0)),
                  pl.BlockSpec((1, D_MODEL, tf), lambda j, f, be, bu: (be[j], 0, f)),
                  pl.BlockSpec((1, D_MODEL, tf), lambda j, f, be, bu: (be[j], 0, f)),
                  pl.BlockSpec((1, tf, D_MODEL), lambda j, f, be, bu: (be[j], f, 0))],
        out_specs=pl.BlockSpec((bm, D_MODEL), lambda j, f, be, bu: (j, 0)),
        scratch_shapes=[pltpu.VMEM((bm, D_MODEL), BF16)])
    return pl.pallas_call(
        _gmm_kernel, grid_spec=grid_spec, out_shape=jax.ShapeDtypeStruct((P, D_MODEL), F32),
        compiler_params=_cparams(("parallel", "arbitrary")), name="moe_gmm",
    )(block_expert, block_used, xs, wg, wu, wd)


def _combine_kernel(*refs, tmc, final):
    if final:
        pos_ref, route_ref, x1_ref, y_hbm, gfin_ref, o_ref, ybuf, sem = refs
    else:
        pos_ref, route_ref, x1_ref, y_hbm, o_ref, ybuf, sem = refs

    def body(t, c):
        for k in range(TOP_K):
            pltpu.make_async_copy(y_hbm.at[pl.ds(pos_ref[0, 0, TOP_K * t + k], 1), :],
                                  ybuf.at[k, pl.ds(t, 1), :], sem).start()
        return c

    lax.fori_loop(0, tmc, body, 0, unroll=4)
    for k in range(TOP_K):
        pltpu.make_async_copy(y_hbm.at[pl.ds(0, tmc), :], ybuf.at[k], sem).wait()
    route = route_ref[...]
    out = x1_ref[...] + route[:, 2:3] * ybuf[0] + route[:, 3:4] * ybuf[1]
    if final:
        out = _rms(out, gfin_ref[...], D_MODEL)
    o_ref[...] = out


def _combine(pos, route, x1, y, gfin):
    T = x1.shape[0]
    tmc = COMBINE_TOKENS
    final = gfin is not None
    pos3 = pos.reshape(T // tmc, 1, TOP_K * tmc)
    args = [pos3, route, x1, y] + ([gfin] if final else [])
    in_specs = [pl.BlockSpec((1, 1, TOP_K * tmc), lambda i: (i, 0, 0), memory_space=pltpu.SMEM),
                pl.BlockSpec((tmc, LANES), lambda i: (i, 0)), pl.BlockSpec((tmc, D_MODEL), lambda i: (i, 0)),
                pl.BlockSpec(memory_space=pl.ANY)]
    if final:
        in_specs.append(pl.BlockSpec(gfin.shape, lambda i: (0, 0)))
    return pl.pallas_call(
        functools.partial(_combine_kernel, tmc=tmc, final=final), grid=(T // tmc,), in_specs=in_specs,
        out_specs=pl.BlockSpec((tmc, D_MODEL), lambda i: (i, 0)),
        out_shape=jax.ShapeDtypeStruct((T, D_MODEL), F32),
        scratch_shapes=[pltpu.VMEM((TOP_K, tmc, D_MODEL), F32), pltpu.SemaphoreType.DMA(())],
        compiler_params=_cparams(("arbitrary",)), name="moe_combine",
    )(*args)


def _moe(h2, x1, route, counts, wg, wu, wd, gfin, tf):
    T = h2.shape[0]
    bm = MOE_BLOCK_ROWS
    n_blocks = TOP_K * T // bm + N_EXPERTS
    cnt = counts[0, :N_EXPERTS].astype(jnp.int32)
    padded = (cnt + bm - 1) // bm * bm
    ends = jnp.cumsum(padded)
    starts = ends - padded
    experts = route[:, 0:TOP_K].astype(jnp.int32)
    pos = starts[experts] + route[:, 4:4 + TOP_K].astype(jnp.int32)
    block_start = jnp.arange(n_blocks, dtype=jnp.int32) * bm
    block_expert = jnp.minimum(jnp.sum((block_start[:, None] >= ends[None, :]).astype(jnp.int32), axis=1),
                               N_EXPERTS - 1)
    tail = n_blocks - N_EXPERTS + jnp.arange(N_EXPERTS, dtype=jnp.int32)
    zinfo = jnp.concatenate([ends // bm - 1, tail, (padded > 0).astype(jnp.int32),
                             (tail * bm >= ends[-1]).astype(jnp.int32)]).astype(jnp.int32)
    xs = _dispatch(h2, pos, zinfo, n_blocks * bm)
    y = _gmm(xs, block_expert, (block_start < ends[-1]).astype(jnp.int32), wg, wu, wd, tf)
    return _combine(pos, route, x1, y, gfin)


def _pad_heads(w, heads, dim, at=0):
    k = w.shape[0]
    w = w.reshape(k, heads, dim)
    w = jnp.pad(w, ((0, 0), (0, 0), (at, MLA_HEAD_PAD - at - dim)))
    return w.reshape(k, heads * MLA_HEAD_PAD)


def _block_diag(w):
    nb, bs, _ = w.shape
    eye = jnp.eye(nb, dtype=w.dtype)
    return (eye[:, None, :, None] * w[:, :, None, :]).reshape(nb * bs, nb * bs)


def _rope_tables(S):
    def cs(dim):
        half = dim // 2
        freqs = jnp.power(jnp.float32(ROPE_THETA), -jnp.arange(half, dtype=F32) * 2.0 / dim)
        ang = jnp.arange(S, dtype=F32)[:, None] * freqs[None, :]
        c, s = jnp.cos(ang), jnp.sin(ang)
        return jnp.concatenate([c, c], -1), jnp.concatenate([-s, s], -1)

    cd, sd = cs(DIL_HEAD_DIM)
    cm, sm = cs(MLA_ROPE)
    ones = jnp.ones((S, MLA_NOPE), F32)
    pad = MLA_HEAD_PAD - MLA_NOPE - MLA_ROPE
    return dict(
        cosd=jnp.tile(cd, (1, DIL_HEADS)), sind=jnp.tile(sd, (1, DIL_HEADS)),
        cosm=jnp.concatenate([ones, cm, jnp.ones((S, pad), F32)], -1),
        sinm=jnp.concatenate([0.0 * ones, sm, jnp.zeros((S, pad), F32)], -1))


def _prep_layer(l, p):
    w_in = p['w_in'][l]
    offs = [0, 256, 512, 768, 1024, 1280, 1536, 1664, 1696]
    xr, yr, qd, kd, vd, cq, ckv, kr = [w_in[:, a:b] for a, b in zip(offs[:-1], offs[1:])]
    kr_pad = _pad_heads(kr, 1, MLA_ROPE, at=MLA_NOPE)
    w1 = jnp.concatenate([xr, yr, qd, kd, vd, cq, ckv, kr_pad], axis=1)
    w_uq = p['mla_w_uq'][l].reshape(MLA_Q_RANK, MLA_HEADS, MLA_NOPE + MLA_ROPE)
    qn, qr = w_uq[:, :, :MLA_NOPE], w_uq[:, :, MLA_NOPE:]
    qr_sw = qr.reshape(MLA_Q_RANK, MLA_HEADS, 2, MLA_ROPE // 2)[:, :, ::-1, :].reshape(qr.shape)
    flat = lambda a, b: jnp.concatenate([a, b], -1).reshape(MLA_Q_RANK, MLA_HEADS * (MLA_NOPE + MLA_ROPE))
    w_uq2 = jnp.concatenate([_pad_heads(flat(qn, qr), MLA_HEADS, MLA_NOPE + MLA_ROPE),
                             _pad_heads(flat(jnp.zeros_like(qn), qr_sw), MLA_HEADS, MLA_NOPE + MLA_ROPE)], axis=1)

    w_ukv = p['mla_w_ukv'][l].reshape(MLA_KV_RANK, MLA_HEADS, MLA_NOPE + MLA_V)
    kn = _pad_heads(w_ukv[:, :, :MLA_NOPE].reshape(MLA_KV_RANK, -1), MLA_HEADS, MLA_NOPE)
    vm = w_ukv[:, :, MLA_NOPE:].reshape(MLA_KV_RANK, MLA_WIDTH)
    w_ukv2 = jnp.concatenate([kn, vm], axis=1)

    rg_w = jnp.stack([jnp.concatenate([_block_diag(p['rg_w_a'][l, d]), _block_diag(p['rg_w_i'][l, d])], axis=1)
                      for d in range(2)])
    rg_b = jnp.stack([jnp.concatenate([p['rg_b_a'][l, d], p['rg_b_i'][l, d]])[None, :] for d in range(2)])
    g_out = p['mix_out_norm_g'][l]
    row = lambda v: v.reshape(1, -1).astype(F32)
    return dict(
        g_mix=row(p['norm_mix_g'][l]), w1=w1.astype(BF16), g_q=row(p['mla_q_norm_g'][l]), w_uq=w_uq2.astype(BF16),
        g_kv=row(p['mla_kv_norm_g'][l]), w_ukv=w_ukv2.astype(BF16),
        conv_w=p['conv_w'][l].astype(F32), conv_b=row(p['conv_b'][l]), rg_w=rg_w.astype(BF16), rg_b=rg_b.astype(F32),
        rg_lam=p['rg_lambda'][l].reshape(2, 1, LRU_WIDTH).astype(F32),
        g_l=row(g_out[:LRU_WIDTH]), g_d=row(g_out[LRU_WIDTH:LRU_WIDTH + DIL_WIDTH]),
        g_m=row(g_out[LRU_WIDTH + DIL_WIDTH:]), w_out=p['w_out'][l].astype(BF16), g_ffn=row(p['norm_ffn_g'][l]))


def _mixers(x2d, lw, tabs, B, S, tm, moe, gfin=None):
    xr, gy, q, k, v, qm, km, vm = _in_proj(x2d, lw, tabs, B, S, tm)
    hf, hb = _lru(xr, lw, B, S)
    dil = _dilated(q, k, v, B, S)
    mo = _mla(qm, km, vm, B, S, 256)
    return _mix_out(x2d, hf, hb, gy, dil, mo, lw, B, S, 2 * tm if moe else tm, moe, gfin)


def _trunk(x, layers, p, depth):
    B, S, _ = x.shape
    tabs = _rope_tables(S)
    x2d = x.reshape(B * S, D_MODEL)
    tm = 512
    for l in range(depth):
        lw = layers[l]
        moe = l % 2 == 1
        last = l == depth - 1
        gfin = p['final_norm_g'].reshape(1, D_MODEL) if last else None
        j = l // 2
        if moe:
            rw = jnp.pad(p['router_w'][j].astype(F32), ((0, 0), (0, LANES - N_EXPERTS)))
            rw_hi = rw.astype(BF16)
            lw = dict(lw, router_w=jnp.stack([rw_hi, (rw - rw_hi.astype(F32)).astype(BF16)]))
            x1, h2, route, counts = _mixers(x2d, lw, tabs, B, S, tm, True)
            x2d = _moe(h2, x1, route, counts, lw['moe_wg'], lw['moe_wu'], lw['moe_wd'], gfin, 1792)
        else:
            x2d = _mixers(x2d, lw, tabs, B, S, tm, False, gfin)
    return x2d.reshape(B, S, D_MODEL)


def kernel(x_prompt, x_sample, norm_mix_g, w_in, conv_w, conv_b, rg_w_a, rg_b_a, rg_w_i, rg_b_i, rg_lambda, mla_q_norm_g, mla_w_uq, mla_kv_norm_g, mla_w_ukv, mix_out_norm_g, w_out, norm_ffn_g, ffn_w_gate, ffn_w_up, ffn_w_down, router_w, moe_w_gate, moe_w_up, moe_w_down, final_norm_g):
    p = dict(norm_mix_g=norm_mix_g, w_in=w_in, conv_w=conv_w, conv_b=conv_b, rg_w_a=rg_w_a, rg_b_a=rg_b_a,
             rg_w_i=rg_w_i, rg_b_i=rg_b_i, rg_lambda=rg_lambda, mla_q_norm_g=mla_q_norm_g, mla_w_uq=mla_w_uq,
             mla_kv_norm_g=mla_kv_norm_g, mla_w_ukv=mla_w_ukv, mix_out_norm_g=mix_out_norm_g, w_out=w_out,
             norm_ffn_g=norm_ffn_g, router_w=router_w, final_norm_g=final_norm_g)
    depth = w_in.shape[0]
    layers = []
    for l in range(depth):
        lw = _prep_layer(l, p)
        j = l // 2
        if l % 2 == 1:
            lw.update(moe_wg=moe_w_gate[j].astype(BF16), moe_wu=moe_w_up[j].astype(BF16),
                      moe_wd=moe_w_down[j].astype(BF16))
        else:
            lw.update(ffn_wg=ffn_w_gate[j].astype(BF16), ffn_wu=ffn_w_up[j].astype(BF16),
                      ffn_wd=ffn_w_down[j].astype(BF16))
        layers.append(lw)
    return (_trunk(x_prompt, layers, p, depth), _trunk(x_sample, layers, p, depth))
```
